```python
import math
import jax
import jax.numpy as jnp
from jax import lax
import numpy as np

D_MODEL = 1024
BATCH = 16
SEQ = 256
DEPTH = 2
DEC_BATCH = 4
DEC_SEQ = 2048
PAST_LEN = 512

GRID_W = 64
N_DIR = 2
CHUNK = 64
F_GROUPS = 4
F_GROUP_DIM = D_MODEL // 8
F_W = F_GROUPS * F_GROUP_DIM
G_HEADS = 4
G_DK = D_MODEL // 8
G_DV = D_MODEL // 8
G_KW = G_HEADS * G_DK
G_VW = G_HEADS * G_DV
CONV_W = 3
M_HEADS = 4
M_DK = D_MODEL // 8
M_DV = D_MODEL // 8
M_KW = M_HEADS * M_DK
M_VW = M_HEADS * M_DV
N_EXPERTS = 32
TOP_K = 4
D_FF = D_MODEL
SWIGLU_ALPHA = 1.702
SWIGLU_LIMIT = 7.0
N_MOD = 6
NORM_EPS = 1e-6
POS_BASE = 10000.0
IN_SPLITS = (F_W, G_KW, G_KW, G_VW, G_VW, N_DIR * G_HEADS, N_DIR * G_HEADS,
             M_KW, M_KW, M_VW, M_VW, N_DIR * M_HEADS, N_DIR * M_HEADS, D_MODEL, D_MODEL, D_MODEL)
N_IN = sum(IN_SPLITS)

kernel_name = 'hybrid_fourier_gdn_mlstm_moe_diffusion_step'


def _split(a, sizes):
    idx = [int(s) for s in np.cumsum(sizes)[:-1]]
    return jnp.split(a, idx, axis=-1)


def _rmsnorm(x, g):
    xf = x.astype(jnp.float32)
    y = xf * lax.rsqrt(jnp.mean(xf * xf, axis=-1, keepdims=True) + NORM_EPS)
    return (y * g.astype(jnp.float32)).astype(x.dtype)


def _l2norm(x):
    return x * lax.rsqrt(jnp.sum(x * x, axis=-1, keepdims=True) + NORM_EPS)


def _dwconv(x, w):
    return lax.conv_general_dilated(x, w[:, None, :], window_strides=(1,),
                                    padding=[(CONV_W // 2, CONV_W // 2)],
                                    dimension_numbers=('NWC', 'WIO', 'NWC'),
                                    feature_group_count=x.shape[-1])


def _pos_embed_2d(n_tokens):
    rows = n_tokens // GRID_W
    quarter = D_MODEL // 4
    omega = 1.0 / (POS_BASE ** (jnp.arange(quarter, dtype=jnp.float32) / quarter))
    t = jnp.arange(rows * GRID_W)
    ang_r = (t // GRID_W).astype(jnp.float32)[:, None] * omega
    ang_c = (t % GRID_W).astype(jnp.float32)[:, None] * omega
    return jnp.concatenate([jnp.sin(ang_r), jnp.cos(ang_r), jnp.sin(ang_c), jnp.cos(ang_c)], axis=-1)


def _to_chunks(a):
    b, t, h = a.shape[:3]
    a = a.reshape((b, t // CHUNK, CHUNK, h) + a.shape[3:])
    return jnp.moveaxis(a, 3, 1)


def _from_chunks(a):
    a = jnp.moveaxis(a, 1, 3)
    return a.reshape(a.shape[0], a.shape[1] * a.shape[2], a.shape[3], a.shape[4])


def _gdn_scan(q, k, v, g, beta, s0):
    dv = v.shape[-1]
    q, k, v, g, beta = (_to_chunks(a) for a in (q, k, v, g, beta))
    gc = jnp.cumsum(g, axis=-1)
    incl = jnp.tril(jnp.ones((CHUNK, CHUNK), bool))
    strict = jnp.tril(jnp.ones((CHUNK, CHUNK), bool), -1)
    diff = gc[..., :, None] - gc[..., None, :]
    decay = jnp.where(incl, jnp.exp(jnp.where(incl, diff, 0.0)), 0.0)
    kb = k * beta[..., None]
    lmat = jnp.where(strict, jnp.einsum('bhnid,bhnjd->bhnij', kb, k) * decay, 0.0)
    a = lmat + jnp.eye(CHUNK, dtype=lmat.dtype)
    rhs = jnp.concatenate([v * beta[..., None], kb * jnp.exp(gc)[..., None]], axis=-1)
    sol = lax.linalg.triangular_solve(a, rhs, left_side=True, lower=True, unit_diagonal=True)
    u, w = sol[..., :dv], sol[..., dv:]
    attn = jnp.einsum('bhnid,bhnjd->bhnij', q, k) * decay
    qg = q * jnp.exp(gc)[..., None]
    kg = k * jnp.exp(gc[..., -1:] - gc)[..., None]
    glast = jnp.exp(gc[..., -1])

    def step(s, xs):
        qg_n, kg_n, u_n, w_n, attn_n, gl_n = xs
        v_new = u_n - jnp.einsum('bhld,bhde->bhle', w_n, s)
        o = jnp.einsum('bhld,bhde->bhle', qg_n, s) + jnp.einsum('bhij,bhje->bhie', attn_n, v_new)
        s = s * gl_n[..., None, None] + jnp.einsum('bhld,bhle->bhde', kg_n, v_new)
        return s, o

    xs = tuple(jnp.moveaxis(t, 2, 0) for t in (qg, kg, u, w, attn, glast))
    s_fin, o = lax.scan(step, s0, xs)
    return _from_chunks(jnp.moveaxis(o, 0, 2)), s_fin


def _gdn_bidir(q, k, v, g, beta, s0):
    fl = lambda a: jnp.flip(a, axis=1)
    o_f, s_f = _gdn_scan(q, k, v, g[:, :, 0], beta[:, :, 0], s0[:, 0])
    o_b, s_b = _gdn_scan(fl(q), fl(k), fl(v), fl(g[:, :, 1]), fl(beta[:, :, 1]), s0[:, 1])
    return o_f + fl(o_b), jnp.stack([s_f, s_b], axis=1)


def _mlstm_scan(q, k, v, ig, lf, c0, n0, m0):
    q, k, v, ig, lf = (_to_chunks(a) for a in (q, k, v, ig, lf))
    b = jnp.cumsum(lf, axis=-1)
    incl = jnp.tril(jnp.ones((CHUNK, CHUNK), bool))
    dlog = jnp.where(incl, b[..., :, None] - b[..., None, :] + ig[..., None, :], -jnp.inf)
    dmax = jnp.max(dlog, axis=-1)
    qk = jnp.einsum('bhnid,bhnjd->bhnij', q, k)
    logw = b[..., -1:] - b + ig
    wmax = jnp.max(logw, axis=-1)
    blast = b[..., -1]

    def step(carry, xs):
        c, n, m = carry
        q_n, k_n, v_n, b_n, dlog_n, dmax_n, qk_n, logw_n, wmax_n, bl_n = xs
        m_inter = b_n + m[..., None]
        m_row = jnp.maximum(dmax_n, m_inter)
        s = qk_n * jnp.exp(dlog_n - m_row[..., None])
        e_inter = jnp.exp(m_inter - m_row)
        num = e_inter[..., None] * jnp.einsum('bhld,bhde->bhle', q_n, c) + jnp.einsum('bhij,bhje->bhie', s, v_n)
        den = e_inter * jnp.einsum('bhld,bhd->bhl', q_n, n) + jnp.sum(s, axis=-1)
        h = num / jnp.maximum(jnp.abs(den), jnp.exp(-m_row))[..., None]
        m_new = jnp.maximum(bl_n + m, wmax_n)
        f_c = jnp.exp(bl_n + m - m_new)
        w_c = jnp.exp(logw_n - m_new[..., None])
        c = f_c[..., None, None] * c + jnp.einsum('bhl,bhld,bhle->bhde', w_c, k_n, v_n)
        n = f_c[..., None] * n + jnp.einsum('bhl,bhld->bhd', w_c, k_n)
        return (c, n, m_new), h

    xs = tuple(jnp.moveaxis(t, 2, 0) for t in (q, k, v, b, dlog, dmax, qk, logw, wmax, blast))
    (c_f, n_f, m_f), h = lax.scan(step, (c0, n0, m0), xs)
    return _from_chunks(jnp.moveaxis(h, 0, 2)), c_f, n_f, m_f


def _mlstm_bidir(q, k, v, ig, lf, c0, n0, m0):
    fl = lambda a: jnp.flip(a, axis=1)
    h_f, c_f, n_f, m_f = _mlstm_scan(q, k, v, ig[:, :, 0], lf[:, :, 0], c0[:, 0], n0[:, 0], m0[:, 0])
    h_b, c_b, n_b, m_b = _mlstm_scan(fl(q), fl(k), fl(v), fl(ig[:, :, 1]), fl(lf[:, :, 1]), c0[:, 1], n0[:, 1], m0[:, 1])
    return (h_f + fl(h_b), jnp.stack([c_f, c_b], axis=1), jnp.stack([n_f, n_b], axis=1),
            jnp.stack([m_f, m_b], axis=1))


def _token_mixer(h, w_in, conv_w, a_log, dt_bias, g_norm, i_bias, f_bias, m_norm,
                 wb_f, wb_g, wb_m, w_out, s_gdn, s_c, s_n, s_m):
    bsz, t, _ = h.shape
    dt = h.dtype
    f32 = jnp.float32
    (xf, gq, gk, gv, gz, ga, gb, mq, mk, mv, mo, mi, mf,
     gate_f, gate_g, gate_m) = _split(h @ w_in, IN_SPLITS)
    xf = xf.astype(f32).reshape(bsz, t, F_GROUPS, F_GROUP_DIM)
    yf = jnp.fft.fftn(xf, axes=(1, 3), norm='ortho').real.reshape(bsz, t, F_W).astype(dt)
    qkv = jax.nn.silu(_dwconv(jnp.concatenate([gq, gk, gv], axis=-1).astype(f32), conv_w.astype(f32)))
    gq, gk, gv = _split(qkv, (G_KW, G_KW, G_VW))
    gq = _l2norm(gq.reshape(bsz, t, G_HEADS, G_DK)) * (G_DK ** -0.5)
    gk = _l2norm(gk.reshape(bsz, t, G_HEADS, G_DK))
    gv = gv.reshape(bsz, t, G_HEADS, G_DV)
    g_log = -jnp.exp(a_log.astype(f32)) * jax.nn.softplus(
        ga.astype(f32).reshape(bsz, t, N_DIR, G_HEADS) + dt_bias.astype(f32))
    beta = jax.nn.sigmoid(gb.astype(f32).reshape(bsz, t, N_DIR, G_HEADS))
    og, s_gdn_new = _gdn_bidir(gq, gk, gv, g_log, beta, s_gdn.astype(f32))
    og = _rmsnorm(og, g_norm) * jax.nn.silu(gz.astype(f32).reshape(bsz, t, G_HEADS, G_DV))
    og = og.reshape(bsz, t, G_VW).astype(dt)
    mq = mq.astype(f32).reshape(bsz, t, M_HEADS, M_DK)
    mk = mk.astype(f32).reshape(bsz, t, M_HEADS, M_DK) * (M_DK ** -0.5)
    mv = mv.astype(f32).reshape(bsz, t, M_HEADS, M_DV)
    ig = mi.astype(f32).reshape(bsz, t, N_DIR, M_HEADS) + i_bias.astype(f32)
    lf = jax.nn.log_sigmoid(mf.astype(f32).reshape(bsz, t, N_DIR, M_HEADS) + f_bias.astype(f32))
    hm, c_new, n_new, m_new = _mlstm_bidir(mq, mk, mv, ig, lf, s_c.astype(f32), s_n.astype(f32), s_m.astype(f32))
    hm = _rmsnorm(hm, m_norm.reshape(M_HEADS, M_DV)) * jax.nn.sigmoid(mo.astype(f32).reshape(bsz, t, M_HEADS, M_DV))
    hm = hm.reshape(bsz, t, M_VW).astype(dt)
    y = (jax.nn.sigmoid(gate_f) * (yf @ wb_f) + jax.nn.sigmoid(gate_g) * (og @ wb_g)
         + jax.nn.sigmoid(gate_m) * (hm @ wb_m))
    return y @ w_out, s_gdn_new, c_new, n_new, m_new


def _moe(h, r_w, r_b, wg, bg, wu, bu, wd, bd):
    bsz, t, d = h.shape
    hf = h.reshape(bsz * t, d)
    logits = (hf @ r_w + r_b).astype(jnp.float32)
    top_v, top_i = lax.top_k(logits, TOP_K)
    probs = jax.nn.softmax(top_v, axis=-1)
    gates = jnp.sum(jax.nn.one_hot(top_i, N_EXPERTS, dtype=jnp.float32) * probs[..., None], axis=1).astype(h.dtype)
    out = jnp.zeros_like(hf)
    for e in range(N_EXPERTS):
        gate = jnp.minimum(hf @ wg[e] + bg[e], SWIGLU_LIMIT)
        up = jnp.clip(hf @ wu[e] + bu[e], -SWIGLU_LIMIT, SWIGLU_LIMIT)
        act = (up + 1) * gate * jax.nn.sigmoid(SWIGLU_ALPHA * gate)
        out = out + gates[:, e:e + 1] * (act @ wd[e] + bd[e])
    return out.reshape(bsz, t, d)


def _layer(x, cond, lp, st):
    (w_ada, b_ada, n1, n2, w_in, conv_w, a_log, dt_bias, g_norm, i_bias, f_bias, m_norm,
     wb_f, wb_g, wb_m, w_out, r_w, r_b, e_wg, e_bg, e_wu, e_bu, e_wd, e_bd) = lp
    mod = (jax.nn.silu(cond) @ w_ada + b_ada).reshape(cond.shape[0], 1, N_MOD, D_MODEL)
    sh1, sc1, gt1, sh2, sc2, gt2 = (mod[:, :, i] for i in range(N_MOD))
    h = _rmsnorm(x, n1) * (1 + sc1) + sh1
    mix, s_gdn, s_c, s_n, s_m = _token_mixer(h, w_in, conv_w, a_log, dt_bias, g_norm, i_bias, f_bias, m_norm,
                                             wb_f, wb_g, wb_m, w_out, st[0], st[1], st[2], st[3])
    x = x + gt1 * mix
    h = _rmsnorm(x, n2) * (1 + sc2) + sh2
    x = x + gt2 * _moe(h, r_w, r_b, e_wg, e_bg, e_wu, e_bu, e_wd, e_bd)
    return x, (s_gdn, s_c, s_n, s_m)


def setup_inputs(seed: int = 0) -> dict:
    key = jax.random.key(seed)
    keys = jax.random.split(key, 40)
    counter = [0]

    def nxt():
        k = keys[counter[0]]
        counter[0] += 1
        return k

    def nrm(shape, scale=1.0):
        return jax.random.normal(nxt(), shape, jnp.float32) * scale

    g_qkv = 2 * G_KW + G_VW
    x_prompt = nrm((BATCH, SEQ, D_MODEL))
    x_sample = nrm((DEC_BATCH, DEC_SEQ, D_MODEL))
    state_gdn = nrm((DEC_BATCH, DEPTH, N_DIR, G_HEADS, G_DK, G_DV), 0.2)
    state_mlstm_c = nrm((DEC_BATCH, DEPTH, N_DIR, M_HEADS, M_DK, M_DV), 0.2)
    state_mlstm_n = nrm((DEC_BATCH, DEPTH, N_DIR, M_HEADS, M_DK), 0.2)
    state_mlstm_m = nrm((DEC_BATCH, DEPTH, N_DIR, M_HEADS), 1.0)
    c = nrm((DEC_BATCH, D_MODEL))
    c_ctx = nrm((D_MODEL,))
    w_ada = nrm((DEPTH, D_MODEL, N_MOD * D_MODEL), 0.5 * D_MODEL ** -0.5)
    b_ada = nrm((DEPTH, N_MOD * D_MODEL), 0.02)
    norm1_g = 1.0 + nrm((DEPTH, D_MODEL), 0.1)
    norm2_g = 1.0 + nrm((DEPTH, D_MODEL), 0.1)
    w_in = nrm((DEPTH, D_MODEL, N_IN), D_MODEL ** -0.5)
    gdn_conv_w = nrm((DEPTH, CONV_W, g_qkv), CONV_W ** -0.5)
    gdn_a_log = jnp.log(jax.random.uniform(nxt(), (DEPTH, N_DIR, G_HEADS), jnp.float32, 1.0, 16.0))
    dt0 = jnp.exp(jax.random.uniform(nxt(), (DEPTH, N_DIR, G_HEADS), jnp.float32,
                                     math.log(1e-3), math.log(1e-1)))
    gdn_dt_bias = dt0 + jnp.log(-jnp.expm1(-dt0))
    gdn_norm_g = 1.0 + nrm((DEPTH, G_DV), 0.1)
    mlstm_i_bias = nrm((DEPTH, N_DIR, M_HEADS), 0.1)
    mlstm_f_bias = jnp.linspace(3.0, 6.0, M_HEADS, dtype=jnp.float32) + nrm((DEPTH, N_DIR, M_HEADS), 0.1)
    mlstm_norm_g = 1.0 + nrm((DEPTH, M_VW), 0.1)
    w_branch_f = nrm((DEPTH, F_W, D_MODEL), F_W ** -0.5)
    w_branch_g = nrm((DEPTH, G_VW, D_MODEL), G_VW ** -0.5)
    w_branch_m = nrm((DEPTH, M_VW, D_MODEL), M_VW ** -0.5)
    w_out = nrm((DEPTH, D_MODEL, D_MODEL), D_MODEL ** -0.5)
    router_w = nrm((DEPTH, D_MODEL, N_EXPERTS), D_MODEL ** -0.5)
    router_b = nrm((DEPTH, N_EXPERTS), 0.01)
    exp_w_gate = nrm((DEPTH, N_EXPERTS, D_MODEL, D_FF), D_MODEL ** -0.5)
    exp_b_gate = nrm((DEPTH, N_EXPERTS, D_FF), 0.01)
    exp_w_up = nrm((DEPTH, N_EXPERTS, D_MODEL, D_FF), D_MODEL ** -0.5)
    exp_b_up = nrm((DEPTH, N_EXPERTS, D_FF), 0.01)
    exp_w_down = nrm((DEPTH, N_EXPERTS, D_FF, D_MODEL), D_FF ** -0.5)
    exp_b_down = nrm((DEPTH, N_EXPERTS, D_MODEL), 0.01)
    final_norm_g = 1.0 + nrm((D_MODEL,), 0.1)
    return {'x_prompt': x_prompt, 'x_sample': x_sample, 'state_gdn': state_gdn,
            'state_mlstm_c': state_mlstm_c, 'state_mlstm_n': state_mlstm_n, 'state_mlstm_m': state_mlstm_m,
            'c': c, 'c_ctx': c_ctx, 'w_ada': w_ada, 'b_ada': b_ada, 'norm1_g': norm1_g, 'norm2_g': norm2_g,
            'w_in': w_in, 'gdn_conv_w': gdn_conv_w, 'gdn_a_log': gdn_a_log, 'gdn_dt_bias': gdn_dt_bias,
            'gdn_norm_g': gdn_norm_g, 'mlstm_i_bias': mlstm_i_bias, 'mlstm_f_bias': mlstm_f_bias,
            'mlstm_norm_g': mlstm_norm_g, 'w_branch_f': w_branch_f, 'w_branch_g': w_branch_g,
            'w_branch_m': w_branch_m, 'w_out': w_out, 'router_w': router_w, 'router_b': router_b,
            'exp_w_gate': exp_w_gate, 'exp_b_gate': exp_b_gate, 'exp_w_up': exp_w_up, 'exp_b_up': exp_b_up,
            'exp_w_down': exp_w_down, 'exp_b_down': exp_b_down, 'final_norm_g': final_norm_g}


def reference(x_prompt, x_sample, state_gdn, state_mlstm_c, state_mlstm_n, state_mlstm_m, c, c_ctx,
              w_ada, b_ada, norm1_g, norm2_g, w_in, gdn_conv_w, gdn_a_log, gdn_dt_bias, gdn_norm_g,
              mlstm_i_bias, mlstm_f_bias, mlstm_norm_g, w_branch_f, w_branch_g, w_branch_m, w_out,
              router_w, router_b, exp_w_gate, exp_b_gate, exp_w_up, exp_b_up, exp_w_down, exp_b_down,
              final_norm_g):
    def lp(l):
        return (w_ada[l], b_ada[l], norm1_g[l], norm2_g[l], w_in[l], gdn_conv_w[l], gdn_a_log[l],
                gdn_dt_bias[l], gdn_norm_g[l], mlstm_i_bias[l], mlstm_f_bias[l], mlstm_norm_g[l],
                w_branch_f[l], w_branch_g[l], w_branch_m[l], w_out[l], router_w[l], router_b[l],
                exp_w_gate[l], exp_b_gate[l], exp_w_up[l], exp_b_up[l], exp_w_down[l], exp_b_down[l])

    f32 = jnp.float32
    bp = x_prompt.shape[0]
    st0 = (jnp.zeros((bp, N_DIR, G_HEADS, G_DK, G_DV), f32), jnp.zeros((bp, N_DIR, M_HEADS, M_DK, M_DV), f32),
           jnp.zeros((bp, N_DIR, M_HEADS, M_DK), f32), jnp.zeros((bp, N_DIR, M_HEADS), f32))
    xp = x_prompt
    gdn_l, c_l, n_l, m_l = [], [], [], []
    for l in range(DEPTH):
        xp, st = _layer(xp, c_ctx[None, :], lp(l), st0)
        gdn_l.append(st[0])
        c_l.append(st[1])
        n_l.append(st[2])
        m_l.append(st[3])
    y_prompt = _rmsnorm(xp, final_norm_g)
    new_state_gdn = jnp.stack(gdn_l, axis=1).astype(x_prompt.dtype)
    new_state_mlstm_c = jnp.stack(c_l, axis=1).astype(x_prompt.dtype)
    new_state_mlstm_n = jnp.stack(n_l, axis=1).astype(x_prompt.dtype)
    new_state_mlstm_m = jnp.stack(m_l, axis=1).astype(x_prompt.dtype)
    xs = x_sample + _pos_embed_2d(x_sample.shape[1]).astype(x_sample.dtype)[None]
    for l in range(DEPTH):
        xs, _ = _layer(xs, c, lp(l), (state_gdn[:, l], state_mlstm_c[:, l], state_mlstm_n[:, l], state_mlstm_m[:, l]))
    y_sample = _rmsnorm(xs, final_norm_g)
    return (y_prompt, y_sample, new_state_gdn, new_state_mlstm_c, new_state_mlstm_n, new_state_mlstm_m)
```

```python
import functools
import math

import numpy as np
import jax
import jax.numpy as jnp
from jax import lax
from jax.experimental import pallas as pl
from jax.experimental.pallas import tpu as pltpu

F32 = jnp.float32
BF16 = jnp.bfloat16

D = 1024
DEPTH = 2
B_CTX, T_CTX = 16, 256
B_LAT, T_LAT = 4, 2048
N_CTX = B_CTX * T_CTX
N_LAT = B_LAT * T_LAT
N_TOK = N_CTX + N_LAT
N_SEQ = B_CTX + B_LAT
GRID_W = 64
H = 4
DH = 128
HW = H * DH
CH = 64
TT = 256
N_TILES = N_TOK // TT
CTX_TILES = N_CTX // TT
LAT_TILES_PER_SEQ = T_LAT // TT
N_CHUNKS = N_TOK // CH
N_EXP = 32
TOP_K = 4
N_MOD = 6
EPS = 1e-6
SWIGLU_ALPHA = 1.702
SWIGLU_LIMIT = 7.0
LANES = 128
TM = 256
S_MAX = N_TOK * TOP_K + N_EXP * (TM - 1)
S_MAX = ((S_MAX + TM - 1) // TM) * TM
NT_MAX = S_MAX // TM

W_MAIN = 512 + 1536 + 512 + 1536 + 512 + 3072


def _dot(a, b):
    return jnp.dot(a, b, preferred_element_type=F32)


def _dot_nt(a, b):
    return lax.dot_general(a, b, (((1,), (1,)), ((), ())), preferred_element_type=F32)


def _dot_tn(a, b):
    return lax.dot_general(a, b, (((0,), (0,)), ((), ())), preferred_element_type=F32)


def _split2(a):
    hi = a.astype(BF16)
    lo = (a - hi.astype(F32)).astype(BF16)
    return hi, lo


def _split3(a):
    hi = a.astype(BF16)
    r = a - hi.astype(F32)
    mid = r.astype(BF16)
    lo = (r - mid.astype(F32)).astype(BF16)
    return hi, mid, lo


def _dot3(a, b):
    ah, al = _split2(a)
    bh, bl = _split2(b)
    return _dot(ah, bh) + (_dot(ah, bl) + _dot(al, bh))


def _dot_exact_lhs(a_bf16, b):
    bh, bm, bl = _split3(b)
    return _dot(a_bf16, bh) + (_dot(a_bf16, bm) + _dot(a_bf16, bl))


def _dot_exact_rhs(a, b_bf16):
    ah, am, al = _split3(a)
    return _dot(ah, b_bf16) + (_dot(am, b_bf16) + _dot(al, b_bf16))


def _sigmoid(x):
    return 1.0 / (1.0 + jnp.exp(-x))


def _silu(x):
    return x * _sigmoid(x)


def _softplus(x):
    return jnp.maximum(x, 0.0) + jnp.log(1.0 + jnp.exp(-jnp.abs(x)))


def _rms(x):
    return x * lax.rsqrt(jnp.mean(x * x, axis=-1, keepdims=True) + EPS)


def _cparams(*sem):
    return pltpu.CompilerParams(dimension_semantics=tuple(sem))


def _ada_kernel(c_ref, w_ref, b_ref, o_ref):
    o_ref[0] = _dot3(_silu(c_ref[...]), w_ref[0]) + b_ref[0]


def _ada(cond8, w_ada, b_ada):
    nb = 1536
    return pl.pallas_call(
        _ada_kernel,
        grid=(DEPTH, N_MOD * D // nb),
        in_specs=[pl.BlockSpec((8, D), lambda l, j: (0, 0)),
                  pl.BlockSpec((1, D, nb), lambda l, j: (l, 0, j)),
                  pl.BlockSpec((1, 1, nb), lambda l, j: (l, 0, j))],
        out_specs=pl.BlockSpec((1, 8, nb), lambda l, j: (l, 0, j)),
        out_shape=jax.ShapeDtypeStruct((DEPTH, 8, N_MOD * D), F32),
        compiler_params=_cparams("parallel", "parallel"),
    )(cond8, w_ada, b_ada.reshape(DEPTH, 1, N_MOD * D))


def _pre_body(x, mod_ref, n1_ref, wm_ref, wsh_ref, wsl_ref, outs):
    f_ref, gqkv_ref, gz_ref, mqkv_ref, mo_ref, gates_ref, sm_ref = outs
    h = _rms(x) * n1_ref[...]
    h = h * (1.0 + mod_ref[0, 1:2, :]) + mod_ref[0, 0:1, :]
    hb, hl = _split2(h)
    off = 0
    for ref, width in ((f_ref, 512), (gqkv_ref, 1536), (gz_ref, 512), (mqkv_ref, 1536),
                       (mo_ref, 512), (gates_ref, 3072)):
        ref[...] = _dot(hb, wm_ref[:, off:off + width]).astype(BF16)
        off += width
    sm_ref[...] = _dot(hb, wsh_ref[...]) + (_dot(hb, wsl_ref[...]) + _dot(hl, wsh_ref[...]))


def _pre_first_kernel(tbl_ref, xp_ref, xs_ref, pos_ref, mod_ref, n1_ref, wm_ref, wsh_ref, wsl_ref,
                      x_out_ref, *outs):
    i = pl.program_id(0)
    x = jnp.where(i < CTX_TILES, xp_ref[...], xs_ref[...] + pos_ref[...])
    x_out_ref[...] = x
    _pre_body(x, mod_ref, n1_ref, wm_ref, wsh_ref, wsl_ref, outs)


def _pre_next_kernel(tbl_ref, x_ref, mod_ref, n1_ref, wm_ref, wsh_ref, wsl_ref, *outs):
    _pre_body(x_ref[...], mod_ref, n1_ref, wm_ref, wsh_ref, wsl_ref, outs)


_PRE_OUT_WIDTHS = ((512, BF16), (1536, BF16), (512, BF16), (1536, BF16), (512, BF16), (3072, BF16),
                   (LANES, F32))


def _pre(first, xs_in, mod, n1, wm, wsh, wsl, cond_tbl):
    tile = lambda w: pl.BlockSpec((TT, w), lambda i, t: (i, 0))
    const = lambda shape: pl.BlockSpec(shape, lambda i, t: (0,) * len(shape))
    w_specs = [pl.BlockSpec((1, N_MOD, D), lambda i, t: (t[i], 0, 0)), const((1, D)),
               const((D, W_MAIN)), const((D, LANES)), const((D, LANES))]
    out_specs = [tile(w) for w, _ in _PRE_OUT_WIDTHS]
    out_shape = [jax.ShapeDtypeStruct((N_TOK, w), dt) for w, dt in _PRE_OUT_WIDTHS]
    if first:
        x_prompt2, x_sample2, pos = xs_in
        in_specs = [pl.BlockSpec((TT, D), lambda i, t: (jnp.minimum(i, CTX_TILES - 1), 0)),
                    pl.BlockSpec((TT, D), lambda i, t: (jnp.maximum(i - CTX_TILES, 0), 0)),
                    pl.BlockSpec((TT, D), lambda i, t: (jnp.maximum(i - CTX_TILES, 0) % LAT_TILES_PER_SEQ, 0))]
        kern = _pre_first_kernel
        out_specs = [tile(D)] + out_specs
        out_shape = [jax.ShapeDtypeStruct((N_TOK, D), F32)] + out_shape
        args = (x_prompt2, x_sample2, pos)
    else:
        in_specs = [tile(D)]
        kern = _pre_next_kernel
        args = (xs_in,)
    return pl.pallas_call(
        kern,
        grid_spec=pltpu.PrefetchScalarGridSpec(
            num_scalar_prefetch=1, grid=(N_TILES,), in_specs=in_specs + w_specs, out_specs=out_specs),
        out_shape=out_shape,
        compiler_params=_cparams("parallel"),
    )(cond_tbl, *args, mod, n1, wm, wsh, wsl)


def _fourier_kernel(x_ref, dft_ref, c_ref, s_ref, o_ref, z_scr, *, t_len, scale):
    @pl.when(pl.program_id(1) == 0)
    def _():
        for g in range(H):
            xg = x_ref[:, g * DH:(g + 1) * DH]
            z_scr[0:t_len, g * DH:(g + 1) * DH] = _dot(xg, c_ref[...]).astype(BF16)
            z_scr[t_len:2 * t_len, g * DH:(g + 1) * DH] = _dot(xg, s_ref[...]).astype(BF16)

    o_ref[...] = (_dot(dft_ref[...], z_scr[...]) * scale).astype(BF16)


def _fourier(f_all, dft, c128, s128, n_seq, t_len, row0):
    tr = min(t_len, 512)
    blk0 = row0 // t_len
    return pl.pallas_call(
        functools.partial(_fourier_kernel, t_len=t_len, scale=1.0 / math.sqrt(t_len * DH)),
        grid=(n_seq, t_len // tr),
        in_specs=[pl.BlockSpec((t_len, HW), lambda b, r: (blk0 + b, 0)),
                  pl.BlockSpec((tr, 2 * t_len), lambda b, r: (r, 0)),
                  pl.BlockSpec((DH, DH), lambda b, r: (0, 0)),
                  pl.BlockSpec((DH, DH), lambda b, r: (0, 0))],
        out_specs=pl.BlockSpec((tr, HW), lambda b, r: (b * (t_len // tr) + r, 0)),
        out_shape=jax.ShapeDtypeStruct((n_seq * t_len, HW), BF16),
        scratch_shapes=[pltpu.VMEM((2 * t_len, HW), BF16)],
        compiler_params=_cparams("parallel", "arbitrary"),
    )(f_all, dft, c128, s128)


HALO = 16


def _prep2_kernel(cur_ref, prev_ref, next_ref, sm_ref, cw_ref, prm_ref, qkv_ref, g_ref):
    i = pl.program_id(0)
    j = jnp.maximum(i - CTX_TILES, 0) % LAT_TILES_PER_SEQ
    is_lat = i >= CTX_TILES
    has_prev = jnp.logical_and(is_lat, j > 0)
    has_next = jnp.logical_and(is_lat, j < LAT_TILES_PER_SEQ - 1)
    x = cur_ref[...].astype(F32)
    row = lax.broadcasted_iota(jnp.int32, x.shape, 0)
    prev_row = jnp.where(has_prev, prev_ref[HALO - 1:HALO, :].astype(F32), 0.0)
    next_row = jnp.where(has_next, next_ref[0:1, :].astype(F32), 0.0)
    xp = jnp.where(row == 0, prev_row, pltpu.roll(x, 1, 0))
    xn = jnp.where(row == TT - 1, next_row, pltpu.roll(x, TT - 1, 0))
    y = _silu(cw_ref[0:1, :] * xp + cw_ref[1:2, :] * x + cw_ref[2:3, :] * xn)
    for h in range(2 * H):
        seg = y[:, h * DH:(h + 1) * DH]
        seg = seg * lax.rsqrt(jnp.sum(seg * seg, axis=-1, keepdims=True) + EPS)
        if h < H:
            seg = seg * (DH ** -0.5)
        qkv_ref[:, h * DH:(h + 1) * DH] = seg.astype(BF16)
    qkv_ref[:, 2 * HW:3 * HW] = y[:, 2 * HW:3 * HW].astype(BF16)
    z = sm_ref[...] + prm_ref[1:2, :]
    lane = lax.broadcasted_iota(jnp.int32, z.shape, 1)
    g_log = -jnp.exp(prm_ref[0:1, :]) * _softplus(z)
    out = jnp.where(lane < 8, g_log,
                    jnp.where(lane < 16, _sigmoid(z), jnp.where(lane < 24, z, -_softplus(-z))))
    g_ref[...] = out


def _prep2(gqkv, sm, conv_w, prm):
    nb = N_TOK // HALO
    return pl.pallas_call(
        _prep2_kernel,
        grid=(N_TILES,),
        in_specs=[pl.BlockSpec((TT, 3 * HW), lambda i: (i, 0)),
                  pl.BlockSpec((HALO, 3 * HW), lambda i: (jnp.maximum(i * (TT // HALO) - 1, 0), 0)),
                  pl.BlockSpec((HALO, 3 * HW), lambda i: (jnp.minimum((i + 1) * (TT // HALO), nb - 1), 0)),
                  pl.BlockSpec((TT, LANES), lambda i: (i, 0)),
                  pl.BlockSpec((3, 3 * HW), lambda i: (0, 0)),
                  pl.BlockSpec((8, LANES), lambda i: (0, 0))],
        out_specs=[pl.BlockSpec((TT, 3 * HW), lambda i: (i, 0)),
                   pl.BlockSpec((TT, LANES), lambda i: (i, 0))],
        out_shape=[jax.ShapeDtypeStruct((N_TOK, 3 * HW), BF16),
                   jax.ShapeDtypeStruct((N_TOK, LANES), F32)],
        compiler_params=_cparams("parallel"),
    )(gqkv, gqkv, gqkv, sm, conv_w, prm)


def _scan_tables():
    blk = np.zeros((2, N_CHUNKS), np.int32)
    seq = np.zeros((2, N_CHUNKS), np.int32)
    first = np.zeros((2, N_CHUNKS), np.int32)
    last = np.zeros((2, N_CHUNKS), np.int32)
    step = 0
    base = 0
    sid = 0
    for n_seq, t_len in ((B_CTX, T_CTX), (B_LAT, T_LAT)):
        nc = t_len // CH
        for _ in range(n_seq):
            for c in range(nc):
                blk[0, step] = base + c
                blk[1, step] = base + nc - 1 - c
                seq[:, step] = sid
                first[:, step] = int(c == 0)
                last[:, step] = int(c == nc - 1)
                step += 1
            base += nc
            sid += 1
    return blk.reshape(-1), seq.reshape(-1), first.reshape(-1), last.reshape(-1)


def _chunk_masks(d):
    row = lax.broadcasted_iota(jnp.int32, (CH, CH), 0)
    col = lax.broadcasted_iota(jnp.int32, (CH, CH), 1)
    delta = (col - row) * (1 - 2 * d)
    return delta <= 0, delta < 0, delta >= 0


def _ones_where(mask):
    return jnp.where(mask, 1.0, 0.0).astype(BF16)


def _cumsums(g, incl, incl_t):
    cols = _dot_exact_lhs(_ones_where(incl), g)
    g_t = jnp.concatenate([g, jnp.zeros_like(g)], axis=0).T[:, 0:CH]
    rows = _dot_exact_rhs(g_t, _ones_where(incl_t))
    return cols, rows, g_t


def _pick_col(a, fwd, lane):
    return jnp.where(fwd, a[:, lane:lane + 1], a[:, lane + H:lane + H + 1])


def _pick_row(a, fwd, r):
    return jnp.where(fwd, a[r:r + 1, :], a[r + H:r + H + 1, :])


def _unit_tri_inverse(lmat):
    row = lax.broadcasted_iota(jnp.int32, (CH, CH), 0)
    col = lax.broadcasted_iota(jnp.int32, (CH, CH), 1)
    same = lambda sh: jnp.right_shift(row, sh) == jnp.right_shift(col, sh)
    m = jnp.where(same(3), -lmat, 0.0)
    m2 = _dot3(m, m)
    m4 = _dot3(m2, m2)
    x = jnp.where(row == col, 1.0, 0.0) + m
    x = x + _dot3(x, m2)
    x = x + _dot3(x, m4)
    for sh in (3, 4, 5):
        e = jnp.where(jnp.logical_and(same(sh + 1), jnp.logical_not(same(sh))), lmat, 0.0)
        x = x - _dot3(x, _dot3(e, x))
    return x


def _gdn_kernel(blk_ref, seq_ref, first_ref, last_ref, qkv_ref, g_ref, s0_ref, o_ref, sfin_ref, s_scr):
    d = pl.program_id(0)
    idx = d * N_CHUNKS + pl.program_id(1)
    fwd = d == 0

    @pl.when(first_ref[idx] == 1)
    def _():
        s_scr[...] = s0_ref[0, 0]

    incl, strict, incl_t = _chunk_masks(d)
    g = g_ref[...]
    cols, rows, _ = _cumsums(g, incl, incl_t)
    for h in range(H):
        gc = _pick_col(cols, fwd, h)
        gr = _pick_row(rows, fwd, h)
        beta = _pick_col(g, fwd, 8 + h)
        decay = jnp.where(incl, jnp.exp(jnp.where(incl, gc - gr, 0.0)), 0.0)
        q = qkv_ref[:, h * DH:(h + 1) * DH]
        k = qkv_ref[:, HW + h * DH:HW + (h + 1) * DH]
        v = qkv_ref[:, 2 * HW + h * DH:2 * HW + (h + 1) * DH]
        kf = k.astype(F32)
        kk = _dot_nt(k, k)
        qk = _dot_nt(q, k)
        t_inv = _unit_tri_inverse(jnp.where(strict, beta * kk * decay, 0.0))
        egc = jnp.exp(gc)
        r = _dot3(t_inv, jnp.concatenate([v.astype(F32) * beta, kf * (beta * egc)], axis=1))
        u = r[:, 0:DH]
        w = r[:, DH:2 * DH]
        g_last = jnp.where(fwd, gr[:, CH - 1:CH], gr[:, 0:1])
        s = s_scr[h]
        sb = s.astype(BF16)
        v_new = u - _dot(w.astype(BF16), sb)
        vb = v_new.astype(BF16)
        o = _dot((q.astype(F32) * egc).astype(BF16), sb) + _dot((qk * decay).astype(BF16), vb)
        kg = (kf * jnp.exp(g_last - gc)).astype(BF16)
        s_scr[h] = s * jnp.exp(g_last) + _dot_tn(kg, vb)
        o_ref[0, :, h * DH:(h + 1) * DH] = o

    @pl.when(last_ref[idx] == 1)
    def _():
        sfin_ref[0, 0] = s_scr[...]


def _gdn_scan(tables, qkv, gates, s0):
    blk = lambda d, c, b, s, f, l: (b[d * N_CHUNKS + c], 0)
    st = lambda d, c, b, s, f, l: (s[d * N_CHUNKS + c], d, 0, 0, 0)
    return pl.pallas_call(
        _gdn_kernel,
        grid_spec=pltpu.PrefetchScalarGridSpec(
            num_scalar_prefetch=4, grid=(2, N_CHUNKS),
            in_specs=[pl.BlockSpec((CH, 3 * HW), blk), pl.BlockSpec((CH, LANES), blk),
                      pl.BlockSpec((1, 1, H, DH, DH), st)],
            out_specs=[pl.BlockSpec((1, CH, HW), lambda d, c, b, s, f, l: (d, b[d * N_CHUNKS + c], 0)),
                       pl.BlockSpec((1, 1, H, DH, DH), st)],
            scratch_shapes=[pltpu.VMEM((H, DH, DH), F32)]),
        out_shape=[jax.ShapeDtypeStruct((2, N_TOK, HW), F32),
                   jax.ShapeDtypeStruct((N_SEQ, 2, H, DH, DH), F32)],
        compiler_params=_cparams("arbitrary", "arbitrary"),
    )(*tables, qkv, gates, s0)


def _mlstm_kernel(blk_ref, seq_ref, first_ref, last_ref, qkv_ref, g_ref, c0_ref, n0_ref, m0_ref,
                  o_ref, cfin_ref, nfin_ref, mfin_ref, c_scr, n_scr, m_scr):
    d = pl.program_id(0)
    idx = d * N_CHUNKS + pl.program_id(1)
    fwd = d == 0

    @pl.when(first_ref[idx] == 1)
    def _():
        c_scr[...] = c0_ref[0, 0]
        n_scr[...] = n0_ref[0, 0]
        m_scr[...] = m0_ref[0, 0]

    incl, _, incl_t = _chunk_masks(d)
    g = g_ref[...]
    cols, rows, g_t = _cumsums(g, incl, incl_t)
    scale = DH ** -0.5
    for h in range(H):
        bc = _pick_col(cols, fwd, 24 + h)
        br = _pick_row(rows, fwd, 24 + h)
        ig_c = _pick_col(g, fwd, 16 + h)
        ig_r = _pick_row(g_t, fwd, 16 + h)
        q = qkv_ref[:, h * DH:(h + 1) * DH]
        k = qkv_ref[:, HW + h * DH:HW + (h + 1) * DH]
        v = qkv_ref[:, 2 * HW + h * DH:2 * HW + (h + 1) * DH]
        dlog = jnp.where(incl, bc - br + ig_r, -jnp.inf)
        dmax = jnp.max(dlog, axis=1, keepdims=True)
        qk = _dot_nt(q, k) * scale
        b_last = jnp.where(fwd, br[:, CH - 1:CH], br[:, 0:1])
        w_max = jnp.max(b_last - br + ig_r, axis=1, keepdims=True)
        m = m_scr[h:h + 1, 0:1]
        c = c_scr[h]
        n = n_scr[h:h + 1, :]
        m_inter = bc + m
        m_row = jnp.maximum(dmax, m_inter)
        s = qk * jnp.exp(dlog - m_row)
        e_inter = jnp.exp(m_inter - m_row)
        num = e_inter * _dot(q, c.astype(BF16)) + _dot(s.astype(BF16), v)
        den = (e_inter * jnp.sum(q.astype(F32) * n, axis=1, keepdims=True)
               + jnp.sum(s, axis=1, keepdims=True))
        o_ref[0, :, h * DH:(h + 1) * DH] = num / jnp.maximum(jnp.abs(den), jnp.exp(-m_row))
        m_new = jnp.maximum(b_last + m, w_max)
        f_c = jnp.exp(b_last + m - m_new)
        kw = k.astype(F32) * (jnp.exp(b_last - bc + ig_c - m_new) * scale)
        c_scr[h] = f_c * c + _dot_tn(kw.astype(BF16), v)
        n_scr[h:h + 1, :] = f_c * n + jnp.sum(kw, axis=0, keepdims=True)
        m_scr[h:h + 1, :] = jnp.broadcast_to(m_new, (1, LANES))

    @pl.when(last_ref[idx] == 1)
    def _():
        cfin_ref[0, 0] = c_scr[...]
        nfin_ref[0, 0] = n_scr[...]
        mfin_ref[0, 0] = m_scr[...]


def _mlstm_scan(tables, qkv, gates, c0, n0, m0):
    blk = lambda d, c, b, s, f, l: (b[d * N_CHUNKS + c], 0)
    st5 = lambda d, c, b, s, f, l: (s[d * N_CHUNKS + c], d, 0, 0, 0)
    st4 = lambda d, c, b, s, f, l: (s[d * N_CHUNKS + c], d, 0, 0)
    vec = pl.BlockSpec((1, 1, H, LANES), st4)
    mat = pl.BlockSpec((1, 1, H, DH, DH), st5)
    return pl.pallas_call(
        _mlstm_kernel,
        grid_spec=pltpu.PrefetchScalarGridSpec(
            num_scalar_prefetch=4, grid=(2, N_CHUNKS),
            in_specs=[pl.BlockSpec((CH, 3 * HW), blk), pl.BlockSpec((CH, LANES), blk), mat, vec, vec],
            out_specs=[pl.BlockSpec((1, CH, HW), lambda d, c, b, s, f, l: (d, b[d * N_CHUNKS + c], 0)),
                       mat, vec, vec],
            scratch_shapes=[pltpu.VMEM((H, DH, DH), F32), pltpu.VMEM((H, LANES), F32),
                            pltpu.VMEM((H, LANES), F32)]),
        out_shape=[jax.ShapeDtypeStruct((2, N_TOK, HW), F32),
                   jax.ShapeDtypeStruct((N_SEQ, 2, H, DH, DH), F32),
                   jax.ShapeDtypeStruct((N_SEQ, 2, H, LANES), F32),
                   jax.ShapeDtypeStruct((N_SEQ, 2, H, LANES), F32)],
        compiler_params=_cparams("arbitrary", "arbitrary"),
    )(*tables, qkv, gates, c0, n0, m0)


def _post_kernel(tbl_ref, x_ref, mod_ref, yfc_ref, yfl_ref, of_ref, ob_ref, hf_ref, hb_ref,
                 gz_ref, mo_ref, gates_ref, gn_ref, mn_ref, wbf_ref, wbg_ref, wbm_ref, wo_ref,
                 n2_ref, rwh_ref, rwl_ref, rb_ref,
                 xo_ref, h2_ref, p_ref, rk_ref, cnt_ref, cnt_scr):
    i = pl.program_id(0)

    @pl.when(i == 0)
    def _():
        cnt_scr[...] = jnp.zeros_like(cnt_scr)

    yf = jnp.where(i < CTX_TILES, yfc_ref[...], yfl_ref[...])
    og = of_ref[0] + ob_ref[0]
    hm = hf_ref[0] + hb_ref[0]
    og_parts, hm_parts = [], []
    for h in range(H):
        sl = slice(h * DH, (h + 1) * DH)
        og_parts.append(_rms(og[:, sl]) * gn_ref[...] * _silu(gz_ref[:, sl].astype(F32)))
        hm_parts.append(_rms(hm[:, sl]) * mn_ref[:, sl] * _sigmoid(mo_ref[:, sl].astype(F32)))
    ogb = jnp.concatenate(og_parts, axis=1).astype(BF16)
    hmb = jnp.concatenate(hm_parts, axis=1).astype(BF16)
    y = (_sigmoid(gates_ref[:, 0:D].astype(F32)) * _dot(yf, wbf_ref[...])
         + _sigmoid(gates_ref[:, D:2 * D].astype(F32)) * _dot(ogb, wbg_ref[...])
         + _sigmoid(gates_ref[:, 2 * D:3 * D].astype(F32)) * _dot(hmb, wbm_ref[...]))
    x = x_ref[...] + mod_ref[0, 2:3, :] * _dot(y.astype(BF16), wo_ref[...])
    xo_ref[...] = x
    h2 = _rms(x) * n2_ref[...]
    h2 = h2 * (1.0 + mod_ref[0, 4:5, :]) + mod_ref[0, 3:4, :]
    hb2, hl2 = _split2(h2)
    h2_ref[...] = hb2
    logits = (_dot(hb2, rwh_ref[...]) + (_dot(hb2, rwl_ref[...]) + _dot(hl2, rwh_ref[...]))
              + rb_ref[...])
    lane = lax.broadcasted_iota(jnp.int32, logits.shape, 1).astype(F32)
    work = jnp.where(lane < N_EXP, logits, -jnp.inf)
    sel = jnp.zeros(logits.shape, F32)
    pmat = jnp.zeros(logits.shape, F32)
    top0 = None
    denom = None
    for kk in range(TOP_K):
        mx = jnp.max(work, axis=1, keepdims=True)
        first = jnp.min(jnp.where(work == mx, lane, float(LANES)), axis=1, keepdims=True)
        hit = lane == first
        if kk == 0:
            top0 = mx
        e = jnp.exp(mx - top0)
        denom = e if kk == 0 else denom + e
        pmat = jnp.where(hit, e, pmat)
        sel = jnp.where(hit, 1.0, sel)
        work = jnp.where(hit, -jnp.inf, work)
    p_ref[...] = pmat / denom
    r_i = lax.broadcasted_iota(jnp.int32, (TT, TT), 0)
    c_i = lax.broadcasted_iota(jnp.int32, (TT, TT), 1)
    rank = _dot(_ones_where(c_i < r_i), sel.astype(BF16)) + cnt_scr[0:1, :]
    rk_ref[...] = jnp.where(sel > 0.0, rank, -1.0)
    cnt_scr[0:1, :] = cnt_scr[0:1, :] + jnp.sum(sel, axis=0, keepdims=True)
    cnt_ref[...] = cnt_scr[...]


def _post(cond_tbl, x, mod, yfc, yfl, o_gdn, o_ml, gz, mo, gates, gn, mn, wbf, wbg, wbm, wo,
          n2, rwh, rwl, rb):
    tile = lambda w: pl.BlockSpec((TT, w), lambda i, t: (i, 0))
    const = lambda shape: pl.BlockSpec(shape, lambda i, t: (0,) * len(shape))
    dirs = lambda d: pl.BlockSpec((1, TT, HW), lambda i, t: (d, i, 0))
    in_specs = [tile(D), pl.BlockSpec((1, N_MOD, D), lambda i, t: (t[i], 0, 0)),
                pl.BlockSpec((TT, HW), lambda i, t: (jnp.minimum(i, CTX_TILES - 1), 0)),
                pl.BlockSpec((TT, HW), lambda i, t: (jnp.maximum(i - CTX_TILES, 0), 0)),
                dirs(0), dirs(1), dirs(0), dirs(1),
                tile(HW), tile(HW), tile(3 * D),
                const((1, DH)), const((1, HW)), const((HW, D)), const((HW, D)), const((HW, D)),
                const((D, D)), const((1, D)), const((D, LANES)), const((D, LANES)), const((1, LANES))]
    return pl.pallas_call(
        _post_kernel,
        grid_spec=pltpu.PrefetchScalarGridSpec(
            num_scalar_prefetch=1, grid=(N_TILES,), in_specs=in_specs,
            out_specs=[tile(D), tile(D), tile(LANES), tile(LANES), const((8, LANES))],
            scratch_shapes=[pltpu.VMEM((8, LANES), F32)]),
        out_shape=[jax.ShapeDtypeStruct((N_TOK, D), F32), jax.ShapeDtypeStruct((N_TOK, D), BF16),
                   jax.ShapeDtypeStruct((N_TOK, LANES), F32), jax.ShapeDtypeStruct((N_TOK, LANES), F32),
                   jax.ShapeDtypeStruct((8, LANES), F32)],
        compiler_params=_cparams("arbitrary"),
    )(cond_tbl, x, mod, yfc, yfl, o_gdn, o_gdn, o_ml, o_ml, gz, mo, gates, gn, mn, wbf, wbg, wbm, wo,
      n2, rwh, rwl, rb)


def _gmm_kernel(te_ref, nu_ref, x_ref, wg_ref, bg_ref, wu_ref, bu_ref, wd_ref, bd_ref, o_ref,
                wg_scr, wu_scr, wd_scr):
    i = pl.program_id(0)
    used = i < nu_ref[0]

    @pl.when(used)
    def _():
        e = te_ref[i]
        e_prev = te_ref[jnp.maximum(i - 1, 0)]

        @pl.when(jnp.logical_or(i == 0, e != e_prev))
        def _():
            wg_scr[...] = wg_ref[0].astype(BF16)
            wu_scr[...] = wu_ref[0].astype(BF16)
            wd_scr[...] = wd_ref[0].astype(BF16)

        x = x_ref[...]
        gate = jnp.minimum(_dot(x, wg_scr[...]) + bg_ref[0], SWIGLU_LIMIT)
        up = jnp.clip(_dot(x, wu_scr[...]) + bu_ref[0], -SWIGLU_LIMIT, SWIGLU_LIMIT)
        act = (up + 1.0) * gate * _sigmoid(SWIGLU_ALPHA * gate)
        o_ref[...] = _dot(act.astype(BF16), wd_scr[...]) + bd_ref[0]

    @pl.when(jnp.logical_not(used))
    def _():
        o_ref[...] = jnp.zeros_like(o_ref)


def _gmm(tile_exp, n_used, xs, wg, bg, wu, bu, wd, bd):
    wspec = pl.BlockSpec((1, D, D), lambda i, te, nu: (te[i], 0, 0))
    bspec = pl.BlockSpec((1, 1, D), lambda i, te, nu: (te[i], 0, 0))
    return pl.pallas_call(
        _gmm_kernel,
        grid_spec=pltpu.PrefetchScalarGridSpec(
            num_scalar_prefetch=2, grid=(NT_MAX,),
            in_specs=[pl.BlockSpec((TM, D), lambda i, te, nu: (jnp.minimum(i, nu[0] - 1), 0)),
                      wspec, bspec, wspec, bspec, wspec, bspec],
            out_specs=pl.BlockSpec((TM, D), lambda i, te, nu: (i, 0)),
            scratch_shapes=[pltpu.VMEM((D, D), BF16)] * 3),
        out_shape=jax.ShapeDtypeStruct((S_MAX, D), F32),
        compiler_params=_cparams("arbitrary"),
    )(tile_exp, n_used, xs, wg, bg.reshape(N_EXP, 1, D), wu, bu.reshape(N_EXP, 1, D),
      wd, bd.reshape(N_EXP, 1, D))


def _resid_kernel(tbl_ref, x_ref, moe_ref, mod_ref, o_ref):
    o_ref[...] = x_ref[...] + mod_ref[0, 5:6, :] * moe_ref[...]


def _resid(cond_tbl, x, moe, mod):
    tile = pl.BlockSpec((TT, D), lambda i, t: (i, 0))
    return pl.pallas_call(
        _resid_kernel,
        grid_spec=pltpu.PrefetchScalarGridSpec(
            num_scalar_prefetch=1, grid=(N_TILES,),
            in_specs=[tile, tile, pl.BlockSpec((1, N_MOD, D), lambda i, t: (t[i], 0, 0))],
            out_specs=tile),
        out_shape=jax.ShapeDtypeStruct((N_TOK, D), F32),
        compiler_params=_cparams("parallel"),
    )(cond_tbl, x, moe, mod)


def _final_kernel(tbl_ref, x_ref, moe_ref, mod_ref, g_ref, o_ref):
    x = x_ref[...] + mod_ref[0, 5:6, :] * moe_ref[...]
    o_ref[...] = _rms(x) * g_ref[...]


def _final(cond_tbl, x, moe, mod, g, tile0, n_tiles):
    tile = pl.BlockSpec((TT, D), lambda i, t: (tile0 + i, 0))
    return pl.pallas_call(
        _final_kernel,
        grid_spec=pltpu.PrefetchScalarGridSpec(
            num_scalar_prefetch=1, grid=(n_tiles,),
            in_specs=[tile, tile, pl.BlockSpec((1, N_MOD, D), lambda i, t: (t[tile0 + i], 0, 0)),
                      pl.BlockSpec((1, D), lambda i, t: (0, 0))],
            out_specs=pl.BlockSpec((TT, D), lambda i, t: (i, 0))),
        out_shape=jax.ShapeDtypeStruct((n_tiles * TT, D), F32),
        compiler_params=_cparams("parallel"),
    )(cond_tbl, x, moe, mod, g)


def _dft_consts(t_len):
    k = np.arange(t_len, dtype=np.int64)
    ang = 2.0 * np.pi * ((k[:, None] * k[None, :]) % t_len).astype(np.float64) / t_len
    return np.concatenate([np.cos(ang), -np.sin(ang)], axis=1).astype(np.float32)


def _pos_table():
    quarter = D // 4
    omega = 1.0 / (10000.0 ** (np.arange(quarter, dtype=np.float32) / np.float32(quarter)))
    omega = omega.astype(np.float32).astype(np.float64)
    t = np.arange(T_LAT)
    ang_r = (t // GRID_W).astype(np.float64)[:, None] * omega
    ang_c = (t % GRID_W).astype(np.float64)[:, None] * omega
    return np.concatenate([np.sin(ang_r), np.cos(ang_r), np.sin(ang_c), np.cos(ang_c)],
                          axis=-1).astype(np.float32)


def _repack_w_in(w):
    f, gq, gk, gv, gz, ga, gb, mq, mk, mv, mo, mi, mf, g1, g2, g3 = jnp.split(
        w, np.cumsum([512, 512, 512, 512, 512, 8, 8, 512, 512, 512, 512, 8, 8, 1024, 1024])[:].tolist(),
        axis=1)
    main = jnp.concatenate([f, gq, gk, gv, gz, mq, mk, mv, mo, g1, g2, g3], axis=1).astype(BF16)
    small = jnp.concatenate([ga, gb, mi, mf, jnp.zeros((D, LANES - 32), F32)], axis=1)
    hi = small.astype(BF16)
    lo = (small - hi.astype(F32)).astype(BF16)
    return main, hi, lo


def _route_glue(rk, pmat, cnt):
    counts = cnt[0, :N_EXP].astype(jnp.int32)
    region = ((counts + TM - 1) // TM) * TM
    off = jnp.cumsum(region) - region
    n_used = (jnp.sum(region) // TM).astype(jnp.int32)
    tile_start = jnp.arange(NT_MAX, dtype=jnp.int32) * TM
    ends = jnp.cumsum(region)
    tile_exp = jnp.minimum(jnp.searchsorted(ends, tile_start, side='right'), N_EXP - 1).astype(jnp.int32)
    last_exp = tile_exp[jnp.maximum(n_used - 1, 0)]
    tile_exp = jnp.where(tile_start < ends[-1], tile_exp, last_exp)
    selected = rk[:, :N_EXP] >= 0
    dest = jnp.where(selected, off[None, :] + rk[:, :N_EXP].astype(jnp.int32), S_MAX)
    _, ids = lax.top_k(selected.astype(F32), TOP_K)
    d4 = jnp.take_along_axis(dest, ids, axis=1)
    p4 = jnp.take_along_axis(pmat[:, :N_EXP], ids, axis=1)
    tok = jnp.broadcast_to(jnp.arange(N_TOK, dtype=jnp.int32)[:, None], d4.shape)
    src = jnp.zeros((S_MAX,), jnp.int32).at[d4.reshape(-1)].set(tok.reshape(-1), mode='drop')
    return tile_exp, n_used.reshape(1), src, d4, p4


def kernel(x_prompt, x_sample, state_gdn, state_mlstm_c, state_mlstm_n, state_mlstm_m, c, c_ctx,
           w_ada, b_ada, norm1_g, norm2_g, w_in, gdn_conv_w, gdn_a_log, gdn_dt_bias, gdn_norm_g,
           mlstm_i_bias, mlstm_f_bias, mlstm_norm_g, w_branch_f, w_branch_g, w_branch_m, w_out,
           router_w, router_b, exp_w_gate, exp_b_gate, exp_w_up, exp_b_up, exp_w_down, exp_b_down,
           final_norm_g):
    cond_tbl = jnp.asarray(np.concatenate([np.zeros(CTX_TILES, np.int32),
                                           1 + np.arange(N_TILES - CTX_TILES, dtype=np.int32) // LAT_TILES_PER_SEQ]))
    tables = tuple(jnp.asarray(t) for t in _scan_tables())
    cond8 = jnp.concatenate([c_ctx[None, :], c, jnp.zeros((8 - 1 - B_LAT, D), F32)], axis=0)
    mod_all = _ada(cond8, w_ada, b_ada).reshape(DEPTH, 8, N_MOD, D)

    pos = jnp.asarray(_pos_table())
    dft_c = jnp.asarray(_dft_consts(T_CTX)).astype(BF16)
    dft_l = jnp.asarray(_dft_consts(T_LAT)).astype(BF16)
    ang = 2.0 * np.pi * ((np.arange(DH)[:, None] * np.arange(DH)[None, :]) % DH) / DH
    c128 = jnp.asarray(np.cos(ang).astype(np.float32)).astype(BF16)
    s128 = jnp.asarray(np.sin(ang).astype(np.float32)).astype(BF16)

    zeros_s = jnp.zeros((B_CTX, 2, H, DH, DH), F32)
    zeros_v = jnp.zeros((B_CTX, 2, H, LANES), F32)
    x = None
    gdn_states, c_states, n_states, m_states = [], [], [], []
    y_prompt = y_sample = None
    for l in range(DEPTH):
        mod = mod_all[l]
        wm, wsh, wsl = _repack_w_in(w_in[l])
        if l == 0:
            outs = _pre(True, (x_prompt.reshape(N_CTX, D), x_sample.reshape(N_LAT, D), pos), mod,
                        norm1_g[l][None, :], wm, wsh, wsl, cond_tbl)
            x, outs = outs[0], outs[1:]
        else:
            outs = _pre(False, x, mod, norm1_g[l][None, :], wm, wsh, wsl, cond_tbl)
        f_all, gqkv, gz, mqkv, mo, gates, sm = outs

        yfc = _fourier(f_all, dft_c, c128, s128, B_CTX, T_CTX, 0)
        yfl = _fourier(f_all, dft_l, c128, s128, B_LAT, T_LAT, N_CTX)

        prm = jnp.zeros((8, LANES), F32)
        prm = prm.at[0, 0:8].set(gdn_a_log[l].reshape(-1))
        prm = prm.at[1, 0:8].set(gdn_dt_bias[l].reshape(-1))
        prm = prm.at[1, 16:24].set(mlstm_i_bias[l].reshape(-1))
        prm = prm.at[1, 24:32].set(mlstm_f_bias[l].reshape(-1))
        qkv_c, gts = _prep2(gqkv, sm, gdn_conv_w[l], prm)

        s0 = jnp.concatenate([zeros_s, state_gdn[:, l]], axis=0)
        o_gdn, s_fin = _gdn_scan(tables, qkv_c, gts, s0)
        c0 = jnp.concatenate([zeros_s, state_mlstm_c[:, l]], axis=0)
        n0 = jnp.concatenate([zeros_v, state_mlstm_n[:, l]], axis=0)
        m0 = jnp.concatenate([zeros_v, jnp.broadcast_to(state_mlstm_m[:, l][..., None],
                                                         (B_LAT, 2, H, LANES))], axis=0)
        o_ml, c_fin, n_fin, m_fin = _mlstm_scan(tables, mqkv, gts, c0, n0, m0)
        gdn_states.append(s_fin[:B_CTX])
        c_states.append(c_fin[:B_CTX])
        n_states.append(n_fin[:B_CTX])
        m_states.append(m_fin[:B_CTX, :, :, 0])

        rw = jnp.concatenate([router_w[l], jnp.zeros((D, LANES - N_EXP), F32)], axis=1)
        rwh = rw.astype(BF16)
        rwl = (rw - rwh.astype(F32)).astype(BF16)
        rb = jnp.concatenate([router_b[l], jnp.zeros((LANES - N_EXP,), F32)])[None, :]
        x, h2, pmat, rk, cnt = _post(
            cond_tbl, x, mod, yfc, yfl, o_gdn, o_ml, gz, mo, gates,
            gdn_norm_g[l][None, :], mlstm_norm_g[l][None, :],
            w_branch_f[l].astype(BF16), w_branch_g[l].astype(BF16), w_branch_m[l].astype(BF16),
            w_out[l].astype(BF16), norm2_g[l][None, :], rwh, rwl, rb)

        tile_exp, n_used, src, d4, p4 = _route_glue(rk, pmat, cnt)
        xs_sorted = jnp.take(h2, src, axis=0)
        y_sorted = _gmm(tile_exp, n_used, xs_sorted, exp_w_gate[l], exp_b_gate[l], exp_w_up[l],
                        exp_b_up[l], exp_w_down[l], exp_b_down[l])
        moe = jnp.sum(jnp.take(y_sorted, d4, axis=0) * p4[..., None], axis=1)
        if l + 1 < DEPTH:
            x = _resid(cond_tbl, x, moe, mod)
        else:
            y_prompt = _final(cond_tbl, x, moe, mod, final_norm_g[None, :], 0, CTX_TILES)
            y_sample = _final(cond_tbl, x, moe, mod, final_norm_g[None, :], CTX_TILES,
                              N_TILES - CTX_TILES)

    return (y_prompt.reshape(B_CTX, T_CTX, D), y_sample.reshape(B_LAT, T_LAT, D),
            jnp.stack(gdn_states, axis=1), jnp.stack(c_states, axis=1),
            jnp.stack(n_states, axis=1), jnp.stack(m_states, axis=1))
```

```python
import functools
import math

import numpy as np
import jax
import jax.numpy as jnp
from jax import lax
from jax.experimental import pallas as pl
from jax.experimental.pallas import tpu as pltpu

F32 = jnp.float32
BF16 = jnp.bfloat16

D = 1024
DEPTH = 2
B_CTX, T_CTX = 16, 256
B_LAT, T_LAT = 4, 2048
N_CTX = B_CTX * T_CTX
N_LAT = B_LAT * T_LAT
N_TOK = N_CTX + N_LAT
N_SEQ = B_CTX + B_LAT
GRID_W = 64
H = 4
DH = 128
HW = H * DH
CH = 64
TT = 256
N_TILES = N_TOK // TT
CTX_TILES = N_CTX // TT
LAT_TILES_PER_SEQ = T_LAT // TT
N_CHUNKS = N_TOK // CH
N_EXP = 32
TOP_K = 4
N_MOD = 6
EPS = 1e-6
SWIGLU_ALPHA = 1.702
SWIGLU_LIMIT = 7.0
LANES = 128
TM = 256
S_MAX = N_TOK * TOP_K + N_EXP * (TM - 1)
S_MAX = ((S_MAX + TM - 1) // TM) * TM
NT_MAX = S_MAX // TM

W_MAIN = 512 + 1536 + 512 + 1536 + 512 + 3072


def _dot(a, b):
    return jnp.dot(a, b, preferred_element_type=F32)


def _dot_nt(a, b):
    return lax.dot_general(a, b, (((1,), (1,)), ((), ())), preferred_element_type=F32)


def _dot_tn(a, b):
    return lax.dot_general(a, b, (((0,), (0,)), ((), ())), preferred_element_type=F32)


def _split2(a):
    hi = a.astype(BF16)
    lo = (a - hi.astype(F32)).astype(BF16)
    return hi, lo


def _split3(a):
    hi = a.astype(BF16)
    r = a - hi.astype(F32)
    mid = r.astype(BF16)
    lo = (r - mid.astype(F32)).astype(BF16)
    return hi, mid, lo


def _dot3(a, b):
    ah, al = _split2(a)
    bh, bl = _split2(b)
    return _dot(ah, bh) + (_dot(ah, bl) + _dot(al, bh))


def _dot_exact_lhs(a_bf16, b):
    bh, bm, bl = _split3(b)
    return _dot(a_bf16, bh) + (_dot(a_bf16, bm) + _dot(a_bf16, bl))


def _sigmoid(x):
    return 1.0 / (1.0 + jnp.exp(-x))


def _silu(x):
    return x * _sigmoid(x)


def _softplus(x):
    return jnp.maximum(x, 0.0) + jnp.log(1.0 + jnp.exp(-jnp.abs(x)))


def _rms(x):
    return x * lax.rsqrt(jnp.mean(x * x, axis=-1, keepdims=True) + EPS)


def _cparams(*sem):
    return pltpu.CompilerParams(dimension_semantics=tuple(sem))


def _ada_kernel(c_ref, w_ref, b_ref, o_ref):
    o_ref[0] = _dot3(_silu(c_ref[...]), w_ref[0]) + b_ref[0]


def _ada(cond8, w_ada, b_ada):
    nb = 1536
    return pl.pallas_call(
        _ada_kernel,
        grid=(DEPTH, N_MOD * D // nb),
        in_specs=[pl.BlockSpec((8, D), lambda l, j: (0, 0)),
                  pl.BlockSpec((1, D, nb), lambda l, j: (l, 0, j)),
                  pl.BlockSpec((1, 1, nb), lambda l, j: (l, 0, j))],
        out_specs=pl.BlockSpec((1, 8, nb), lambda l, j: (l, 0, j)),
        out_shape=jax.ShapeDtypeStruct((DEPTH, 8, N_MOD * D), F32),
        compiler_params=_cparams("parallel", "parallel"),
    )(cond8, w_ada, b_ada.reshape(DEPTH, 1, N_MOD * D))


def _pre_body(x, mod_ref, n1_ref, wm_ref, wsh_ref, wsl_ref, outs):
    f_ref, gqkv_ref, gz_ref, mqkv_ref, mo_ref, gates_ref, sm_ref = outs
    h = _rms(x) * n1_ref[...]
    h = h * (1.0 + mod_ref[0, 1:2, :]) + mod_ref[0, 0:1, :]
    hb, hl = _split2(h)
    off = 0
    for ref, width in ((f_ref, 512), (gqkv_ref, 1536), (gz_ref, 512), (mqkv_ref, 1536),
                       (mo_ref, 512), (gates_ref, 3072)):
        ref[...] = _dot(hb, wm_ref[:, off:off + width]).astype(BF16)
        off += width
    sm_ref[...] = _dot(hb, wsh_ref[...]) + (_dot(hb, wsl_ref[...]) + _dot(hl, wsh_ref[...]))


def _pre_first_kernel(tbl_ref, xp_ref, xs_ref, pos_ref, mod_ref, n1_ref, wm_ref, wsh_ref, wsl_ref,
                      x_out_ref, *outs):
    i = pl.program_id(0)
    x = jnp.where(i < CTX_TILES, xp_ref[...], xs_ref[...] + pos_ref[...])
    x_out_ref[...] = x
    _pre_body(x, mod_ref, n1_ref, wm_ref, wsh_ref, wsl_ref, outs)


def _pre_next_kernel(tbl_ref, x_ref, mod_ref, n1_ref, wm_ref, wsh_ref, wsl_ref, *outs):
    _pre_body(x_ref[...], mod_ref, n1_ref, wm_ref, wsh_ref, wsl_ref, outs)


_PRE_OUT_WIDTHS = ((512, BF16), (1536, BF16), (512, BF16), (1536, BF16), (512, BF16), (3072, BF16),
                   (LANES, F32))


def _pre(first, xs_in, mod, n1, wm, wsh, wsl, cond_tbl):
    tile = lambda w: pl.BlockSpec((TT, w), lambda i, t: (i, 0))
    const = lambda shape: pl.BlockSpec(shape, lambda i, t: (0,) * len(shape))
    w_specs = [pl.BlockSpec((1, N_MOD, D), lambda i, t: (t[i], 0, 0)), const((1, D)),
               const((D, W_MAIN)), const((D, LANES)), const((D, LANES))]
    out_specs = [tile(w) for w, _ in _PRE_OUT_WIDTHS]
    out_shape = [jax.ShapeDtypeStruct((N_TOK, w), dt) for w, dt in _PRE_OUT_WIDTHS]
    if first:
        x_prompt2, x_sample2, pos = xs_in
        in_specs = [pl.BlockSpec((TT, D), lambda i, t: (jnp.minimum(i, CTX_TILES - 1), 0)),
                    pl.BlockSpec((TT, D), lambda i, t: (jnp.maximum(i - CTX_TILES, 0), 0)),
                    pl.BlockSpec((TT, D), lambda i, t: (jnp.maximum(i - CTX_TILES, 0) % LAT_TILES_PER_SEQ, 0))]
        kern = _pre_first_kernel
        out_specs = [tile(D)] + out_specs
        out_shape = [jax.ShapeDtypeStruct((N_TOK, D), F32)] + out_shape
        args = (x_prompt2, x_sample2, pos)
    else:
        in_specs = [tile(D)]
        kern = _pre_next_kernel
        args = (xs_in,)
    return pl.pallas_call(
        kern,
        grid_spec=pltpu.PrefetchScalarGridSpec(
            num_scalar_prefetch=1, grid=(N_TILES,), in_specs=in_specs + w_specs, out_specs=out_specs),
        out_shape=out_shape,
        compiler_params=_cparams("parallel"),
    )(cond_tbl, *args, mod, n1, wm, wsh, wsl)


def _fourier_kernel(x_ref, dft_ref, c_ref, s_ref, o_ref, z_scr, *, t_len, scale):
    @pl.when(pl.program_id(1) == 0)
    def _():
        for g in range(H):
            xg = x_ref[:, g * DH:(g + 1) * DH]
            z_scr[0:t_len, g * DH:(g + 1) * DH] = _dot(xg, c_ref[...]).astype(BF16)
            z_scr[t_len:2 * t_len, g * DH:(g + 1) * DH] = _dot(xg, s_ref[...]).astype(BF16)

    o_ref[...] = (_dot(dft_ref[...], z_scr[...]) * scale).astype(BF16)


def _fourier(f_all, dft, c128, s128, n_seq, t_len, row0):
    tr = min(t_len, 512)
    blk0 = row0 // t_len
    return pl.pallas_call(
        functools.partial(_fourier_kernel, t_len=t_len, scale=1.0 / math.sqrt(t_len * DH)),
        grid=(n_seq, t_len // tr),
        in_specs=[pl.BlockSpec((t_len, HW), lambda b, r: (blk0 + b, 0)),
                  pl.BlockSpec((tr, 2 * t_len), lambda b, r: (r, 0)),
                  pl.BlockSpec((DH, DH), lambda b, r: (0, 0)),
                  pl.BlockSpec((DH, DH), lambda b, r: (0, 0))],
        out_specs=pl.BlockSpec((tr, HW), lambda b, r: (b * (t_len // tr) + r, 0)),
        out_shape=jax.ShapeDtypeStruct((n_seq * t_len, HW), BF16),
        scratch_shapes=[pltpu.VMEM((2 * t_len, HW), BF16)],
        compiler_params=_cparams("parallel", "arbitrary"),
    )(f_all, dft, c128, s128)


HALO = 16


def _prep2_kernel(cur_ref, prev_ref, next_ref, sm_ref, cw_ref, prm_ref, qkv_ref, g_ref):
    i = pl.program_id(0)
    j = jnp.maximum(i - CTX_TILES, 0) % LAT_TILES_PER_SEQ
    is_lat = i >= CTX_TILES
    has_prev = jnp.logical_and(is_lat, j > 0)
    has_next = jnp.logical_and(is_lat, j < LAT_TILES_PER_SEQ - 1)
    x = cur_ref[...].astype(F32)
    row = lax.broadcasted_iota(jnp.int32, x.shape, 0)
    prev_row = jnp.where(has_prev, prev_ref[HALO - 1:HALO, :].astype(F32), 0.0)
    next_row = jnp.where(has_next, next_ref[0:1, :].astype(F32), 0.0)
    xp = jnp.where(row == 0, prev_row, pltpu.roll(x, 1, 0))
    xn = jnp.where(row == TT - 1, next_row, pltpu.roll(x, TT - 1, 0))
    y = _silu(cw_ref[0:1, :] * xp + cw_ref[1:2, :] * x + cw_ref[2:3, :] * xn)
    for h in range(2 * H):
        seg = y[:, h * DH:(h + 1) * DH]
        seg = seg * lax.rsqrt(jnp.sum(seg * seg, axis=-1, keepdims=True) + EPS)
        if h < H:
            seg = seg * (DH ** -0.5)
        qkv_ref[:, h * DH:(h + 1) * DH] = seg.astype(BF16)
    qkv_ref[:, 2 * HW:3 * HW] = y[:, 2 * HW:3 * HW].astype(BF16)
    z = sm_ref[...] + prm_ref[1:2, :]
    lane = lax.broadcasted_iota(jnp.int32, z.shape, 1)
    g_log = -jnp.exp(prm_ref[0:1, :]) * _softplus(z)
    out = jnp.where(lane < 8, g_log,
                    jnp.where(lane < 16, _sigmoid(z), jnp.where(lane < 24, z, -_softplus(-z))))
    g_ref[...] = out


def _prep2(gqkv, sm, conv_w, prm):
    nb = N_TOK // HALO
    return pl.pallas_call(
        _prep2_kernel,
        grid=(N_TILES,),
        in_specs=[pl.BlockSpec((TT, 3 * HW), lambda i: (i, 0)),
                  pl.BlockSpec((HALO, 3 * HW), lambda i: (jnp.maximum(i * (TT // HALO) - 1, 0), 0)),
                  pl.BlockSpec((HALO, 3 * HW), lambda i: (jnp.minimum((i + 1) * (TT // HALO), nb - 1), 0)),
                  pl.BlockSpec((TT, LANES), lambda i: (i, 0)),
                  pl.BlockSpec((3, 3 * HW), lambda i: (0, 0)),
                  pl.BlockSpec((8, LANES), lambda i: (0, 0))],
        out_specs=[pl.BlockSpec((TT, 3 * HW), lambda i: (i, 0)),
                   pl.BlockSpec((TT, LANES), lambda i: (i, 0))],
        out_shape=[jax.ShapeDtypeStruct((N_TOK, 3 * HW), BF16),
                   jax.ShapeDtypeStruct((N_TOK, LANES), F32)],
        compiler_params=_cparams("parallel"),
    )(gqkv, gqkv, gqkv, sm, conv_w, prm)


def _scan_tables():
    blk_f = np.zeros(N_CHUNKS, np.int32)
    blk_b = np.zeros(N_CHUNKS, np.int32)
    seq = np.zeros(N_CHUNKS, np.int32)
    first = np.zeros(N_CHUNKS, np.int32)
    last = np.zeros(N_CHUNKS, np.int32)
    step = 0
    base = 0
    sid = 0
    for n_seq, t_len in ((B_CTX, T_CTX), (B_LAT, T_LAT)):
        nc = t_len // CH
        for _ in range(n_seq):
            for c in range(nc):
                blk_f[step] = base + c
                blk_b[step] = base + nc - 1 - c
                seq[step] = sid
                first[step] = int(c == 0)
                last[step] = int(c == nc - 1)
                step += 1
            base += nc
            sid += 1
    return blk_f, blk_b, seq, first, last


def _ones_where(mask):
    return jnp.where(mask, 1.0, 0.0).astype(BF16)


def _scan_order_mask(d, n=CH):
    row = lax.broadcasted_iota(jnp.int32, (n, n), 0)
    col = lax.broadcasted_iota(jnp.int32, (n, n), 1)
    return (col <= row) if d == 0 else (col >= row)


def _row_of(col_vec):
    return jnp.broadcast_to(col_vec, (col_vec.shape[0], LANES)).T[0:1, :]


HC = H * CH


def _unit_tri_inverse(lmat, row, col):
    same = lambda sh: jnp.right_shift(row, sh) == jnp.right_shift(col, sh)
    b = lambda a: a.astype(BF16)
    m = jnp.where(same(3), -lmat, 0.0)
    m2 = _dot(b(m), b(m))
    m4 = _dot(b(m2), b(m2))
    x = jnp.where(row == col, 1.0, 0.0) + m
    x = x + _dot(b(x), b(m2))
    x = x + _dot(b(x), b(m4))
    for sh in (3, 4, 5):
        e = jnp.where(jnp.logical_and(same(sh + 1), jnp.logical_not(same(sh))), lmat, 0.0)
        x = x - _dot(b(x), b(_dot(b(e), b(x))))
    return x


def _head_stack(ref, r0, base):
    return jnp.concatenate([ref[r0:r0 + CH, base + h * DH:base + (h + 1) * DH] for h in range(H)], axis=0)


def _gdn_wy_kernel(qkv_ref, g_ref, u_ref, w_ref, qg_ref, kg_ref, attn_ref, gl_ref):
    row = lax.broadcasted_iota(jnp.int32, (HC, HC), 0)
    col = lax.broadcasted_iota(jnp.int32, (HC, HC), 1)
    same_head = jnp.right_shift(row, 6) == jnp.right_shift(col, 6)
    lane_head = jnp.right_shift(lax.broadcasted_iota(jnp.int32, (CH, HC), 1), 6)
    for s in range(TT // CH):
        r0 = s * CH
        g = g_ref[r0:r0 + CH, :]
        q_st = _head_stack(qkv_ref, r0, 0)
        k_st = _head_stack(qkv_ref, r0, HW)
        v_st = _head_stack(qkv_ref, r0, 2 * HW).astype(F32)
        kf = k_st.astype(F32)
        kk = _dot_nt(k_st, k_st)
        qk = _dot_nt(q_st, k_st)
        for d in range(2):
            incl = jnp.logical_and(same_head, (col <= row) if d == 0 else (col >= row))
            strict = jnp.logical_and(same_head, (col < row) if d == 0 else (col > row))
            cols = _dot_exact_lhs(_ones_where(_scan_order_mask(d)), g)
            last = CH - 1 if d == 0 else 0
            lanes = [d * H + h for h in range(H)]
            gc = jnp.concatenate([cols[:, l:l + 1] for l in lanes], axis=0)
            beta = jnp.concatenate([g[:, 8 + l:9 + l] for l in lanes], axis=0)
            g_last = jnp.concatenate([jnp.broadcast_to(cols[last:last + 1, l:l + 1], (CH, 1)) for l in lanes],
                                     axis=0)
            gr = _row_of(gc)
            decay = jnp.where(incl, jnp.exp(jnp.where(incl, gc - gr, 0.0)), 0.0)
            lmat = jnp.where(strict, beta * kk * decay, 0.0)
            t_inv = _unit_tri_inverse(lmat, row, col).astype(BF16)
            egc = jnp.exp(gc)
            rhs = jnp.concatenate([v_st * beta, kf * (beta * egc)], axis=1)
            y = _dot(t_inv, rhs.astype(BF16))
            resid = rhs - (y + _dot3(lmat, y))
            y = y + _dot(t_inv, resid.astype(BF16))
            attn = qk * decay
            a64 = jnp.zeros((CH, HC), F32)
            for h in range(H):
                a64 = jnp.where(lane_head == h, attn[h * CH:(h + 1) * CH, :], a64)
            attn_ref[d, r0:r0 + CH, :] = a64.astype(BF16)
            qg = (q_st.astype(F32) * egc).astype(BF16)
            kg = (kf * jnp.exp(g_last - gc)).astype(BF16)
            for h in range(H):
                rs = slice(h * CH, (h + 1) * CH)
                ls = slice(h * DH, (h + 1) * DH)
                u_ref[d, r0:r0 + CH, ls] = y[rs, 0:DH]
                w_ref[d, r0:r0 + CH, ls] = y[rs, DH:2 * DH].astype(BF16)
                qg_ref[d, r0:r0 + CH, ls] = qg[rs, :]
                kg_ref[d, r0:r0 + CH, ls] = kg[rs, :]
            gl_rows = [jnp.broadcast_to(cols[last:last + 1, l:l + 1], (1, LANES)) for l in lanes]
            gl_ref[d, s * 8:(s + 1) * 8, :] = jnp.concatenate(gl_rows + [jnp.zeros((8 - H, LANES), F32)], axis=0)


def _gdn_wy(qkv, gates):
    spec = lambda w: pl.BlockSpec((2, TT, w), lambda i: (0, i, 0))
    shp = lambda w, dt: jax.ShapeDtypeStruct((2, N_TOK, w), dt)
    return pl.pallas_call(
        _gdn_wy_kernel,
        grid=(N_TILES,),
        in_specs=[pl.BlockSpec((TT, 3 * HW), lambda i: (i, 0)), pl.BlockSpec((TT, LANES), lambda i: (i, 0))],
        out_specs=[spec(HW), spec(HW), spec(HW), spec(HW), spec(HC),
                   pl.BlockSpec((2, 8 * (TT // CH), LANES), lambda i: (0, i, 0))],
        out_shape=[shp(HW, F32), shp(HW, BF16), shp(HW, BF16), shp(HW, BF16), shp(HC, BF16),
                   jax.ShapeDtypeStruct((2, 8 * N_CHUNKS, LANES), F32)],
        compiler_params=_cparams("parallel"),
    )(qkv, gates)


def _gdn_rec_kernel(bf_ref, bb_ref, seq_ref, first_ref, last_ref, *refs):
    ins, (s0_ref, of_ref, ob_ref, sfin_ref, s_scr) = refs[:12], refs[12:]
    t = pl.program_id(0)

    @pl.when(first_ref[t] == 1)
    def _():
        s_scr[...] = s0_ref[0]

    for d in range(2):
        u_ref, w_ref, qg_ref, kg_ref, attn_ref, gl_ref = ins[6 * d:6 * d + 6]
        o_ref = of_ref if d == 0 else ob_ref
        for h in range(H):
            ls = slice(h * DH, (h + 1) * DH)
            s = s_scr[d, h]
            sb = s.astype(BF16)
            v_new = u_ref[0, :, ls] - _dot(w_ref[0, :, ls], sb)
            vb = v_new.astype(BF16)
            o_ref[:, ls] = _dot(qg_ref[0, :, ls], sb) + _dot(attn_ref[0, :, h * CH:(h + 1) * CH], vb)
            s_scr[d, h] = s * jnp.exp(gl_ref[0, h:h + 1, :]) + _dot_tn(kg_ref[0, :, ls], vb)

    @pl.when(last_ref[t] == 1)
    def _():
        sfin_ref[0] = s_scr[...]


def _gdn_rec(tables, u, w, qg, kg, attn, gl, s0):
    def dir_specs(d):
        pick = (lambda t, bf, bb, *_: bf[t]) if d == 0 else (lambda t, bf, bb, *_: bb[t])
        idx = lambda t, *tb: (d, pick(t, *tb), 0)
        return [pl.BlockSpec((1, CH, HW), idx)] * 4 + [pl.BlockSpec((1, CH, HC), idx),
                                                        pl.BlockSpec((1, 8, LANES), idx)]
    st = pl.BlockSpec((1, 2, H, DH, DH), lambda t, bf, bb, sq, *_: (sq[t], 0, 0, 0, 0))
    per_dir = (u, w, qg, kg, attn, gl)
    return pl.pallas_call(
        _gdn_rec_kernel,
        grid_spec=pltpu.PrefetchScalarGridSpec(
            num_scalar_prefetch=5, grid=(N_CHUNKS,),
            in_specs=dir_specs(0) + dir_specs(1) + [st],
            out_specs=[pl.BlockSpec((CH, HW), lambda t, bf, bb, *_: (bf[t], 0)),
                       pl.BlockSpec((CH, HW), lambda t, bf, bb, *_: (bb[t], 0)), st],
            scratch_shapes=[pltpu.VMEM((2, H, DH, DH), F32)]),
        out_shape=[jax.ShapeDtypeStruct((N_TOK, HW), F32), jax.ShapeDtypeStruct((N_TOK, HW), F32),
                   jax.ShapeDtypeStruct((N_SEQ, 2, H, DH, DH), F32)],
        compiler_params=_cparams("arbitrary"),
    )(*tables, *per_dir, *per_dir, s0)


def _mlstm_kernel(bf_ref, bb_ref, seq_ref, first_ref, last_ref,
                  qkvf_ref, gf_ref, qkvb_ref, gb_ref, c0_ref, n0_ref, m0_ref,
                  of_ref, ob_ref, cfin_ref, nfin_ref, mfin_ref, c_scr, n_scr, m_scr):
    t = pl.program_id(0)

    @pl.when(first_ref[t] == 1)
    def _():
        c_scr[...] = c0_ref[0]
        n_scr[...] = n0_ref[0]
        m_scr[...] = m0_ref[0]

    scale = DH ** -0.5
    for d in range(2):
        qkv_ref, g_ref, o_ref = (qkvf_ref, gf_ref, of_ref) if d == 0 else (qkvb_ref, gb_ref, ob_ref)
        incl = _scan_order_mask(d)
        last = CH - 1 if d == 0 else 0
        g = g_ref[...]
        cols = _dot_exact_lhs(_ones_where(incl), g)
        g_t = jnp.concatenate([g, cols], axis=0).T
        for h in range(H):
            lf = 24 + d * H + h
            li = 16 + d * H + h
            bc = cols[:, lf:lf + 1]
            br = g_t[lf:lf + 1, CH:2 * CH]
            ig_c = g[:, li:li + 1]
            ig_r = g_t[li:li + 1, 0:CH]
            q = qkv_ref[:, h * DH:(h + 1) * DH]
            k = qkv_ref[:, HW + h * DH:HW + (h + 1) * DH]
            v = qkv_ref[:, 2 * HW + h * DH:2 * HW + (h + 1) * DH]
            dlog = jnp.where(incl, bc - br + ig_r, -jnp.inf)
            dmax = jnp.max(dlog, axis=1, keepdims=True)
            qk = _dot_nt(q, k) * scale
            b_last = bc[last:last + 1, :]
            w_max = jnp.max(b_last - br + ig_r, axis=1, keepdims=True)
            m = m_scr[d, h:h + 1, 0:1]
            c = c_scr[d, h]
            n = n_scr[d, h:h + 1, :]
            m_inter = bc + m
            m_row = jnp.maximum(dmax, m_inter)
            s = qk * jnp.exp(dlog - m_row)
            e_inter = jnp.exp(m_inter - m_row)
            num = e_inter * _dot(q, c.astype(BF16)) + _dot(s.astype(BF16), v)
            den = (e_inter * jnp.sum(q.astype(F32) * n, axis=1, keepdims=True)
                   + jnp.sum(s, axis=1, keepdims=True))
            o_ref[:, h * DH:(h + 1) * DH] = num / jnp.maximum(jnp.abs(den), jnp.exp(-m_row))
            m_new = jnp.maximum(b_last + m, w_max)
            f_c = jnp.exp(b_last + m - m_new)
            kw = k.astype(F32) * (jnp.exp(b_last - bc + ig_c - m_new) * scale)
            c_scr[d, h] = f_c * c + _dot_tn(kw.astype(BF16), v)
            n_scr[d, h:h + 1, :] = f_c * n + jnp.sum(kw, axis=0, keepdims=True)
            m_scr[d, h:h + 1, :] = jnp.broadcast_to(m_new, (1, LANES))

    @pl.when(last_ref[t] == 1)
    def _():
        cfin_ref[0] = c_scr[...]
        nfin_ref[0] = n_scr[...]
        mfin_ref[0] = m_scr[...]


def _mlstm_scan(tables, qkv, gates, c0, n0, m0):
    fwd = lambda t, bf, bb, *_: (bf[t], 0)
    bwd = lambda t, bf, bb, *_: (bb[t], 0)
    vec = pl.BlockSpec((1, 2, H, LANES), lambda t, bf, bb, sq, *_: (sq[t], 0, 0, 0))
    mat = pl.BlockSpec((1, 2, H, DH, DH), lambda t, bf, bb, sq, *_: (sq[t], 0, 0, 0, 0))
    return pl.pallas_call(
        _mlstm_kernel,
        grid_spec=pltpu.PrefetchScalarGridSpec(
            num_scalar_prefetch=5, grid=(N_CHUNKS,),
            in_specs=[pl.BlockSpec((CH, 3 * HW), fwd), pl.BlockSpec((CH, LANES), fwd),
                      pl.BlockSpec((CH, 3 * HW), bwd), pl.BlockSpec((CH, LANES), bwd), mat, vec, vec],
            out_specs=[pl.BlockSpec((CH, HW), fwd), pl.BlockSpec((CH, HW), bwd), mat, vec, vec],
            scratch_shapes=[pltpu.VMEM((2, H, DH, DH), F32), pltpu.VMEM((2, H, LANES), F32),
                            pltpu.VMEM((2, H, LANES), F32)]),
        out_shape=[jax.ShapeDtypeStruct((N_TOK, HW), F32), jax.ShapeDtypeStruct((N_TOK, HW), F32),
                   jax.ShapeDtypeStruct((N_SEQ, 2, H, DH, DH), F32),
                   jax.ShapeDtypeStruct((N_SEQ, 2, H, LANES), F32),
                   jax.ShapeDtypeStruct((N_SEQ, 2, H, LANES), F32)],
        compiler_params=_cparams("arbitrary"),
    )(*tables, qkv, gates, qkv, gates, c0, n0, m0)


def _post_kernel(tbl_ref, x_ref, mod_ref, yfc_ref, yfl_ref, of_ref, ob_ref, hf_ref, hb_ref,
                 gz_ref, mo_ref, gates_ref, gn_ref, mn_ref, wbf_ref, wbg_ref, wbm_ref, wo_ref,
                 n2_ref, rwh_ref, rwl_ref, rb_ref,
                 xo_ref, h2_ref, info_ref, infot_ref, stats_ref, cnt_scr):
    i = pl.program_id(0)

    @pl.when(i == 0)
    def _():
        cnt_scr[...] = jnp.zeros_like(cnt_scr)

    yf = jnp.where(i < CTX_TILES, yfc_ref[...], yfl_ref[...])
    og = of_ref[...] + ob_ref[...]
    hm = hf_ref[...] + hb_ref[...]
    og_parts, hm_parts = [], []
    for h in range(H):
        sl = slice(h * DH, (h + 1) * DH)
        og_parts.append(_rms(og[:, sl]) * gn_ref[...] * _silu(gz_ref[:, sl].astype(F32)))
        hm_parts.append(_rms(hm[:, sl]) * mn_ref[:, sl] * _sigmoid(mo_ref[:, sl].astype(F32)))
    ogb = jnp.concatenate(og_parts, axis=1).astype(BF16)
    hmb = jnp.concatenate(hm_parts, axis=1).astype(BF16)
    y = (_sigmoid(gates_ref[:, 0:D].astype(F32)) * _dot(yf, wbf_ref[...])
         + _sigmoid(gates_ref[:, D:2 * D].astype(F32)) * _dot(ogb, wbg_ref[...])
         + _sigmoid(gates_ref[:, 2 * D:3 * D].astype(F32)) * _dot(hmb, wbm_ref[...]))
    x = x_ref[...] + mod_ref[0, 2:3, :] * _dot(y.astype(BF16), wo_ref[...])
    xo_ref[...] = x
    h2 = _rms(x) * n2_ref[...]
    h2 = h2 * (1.0 + mod_ref[0, 4:5, :]) + mod_ref[0, 3:4, :]
    hb2, hl2 = _split2(h2)
    h2_ref[...] = hb2
    logits = (_dot(hb2, rwh_ref[...]) + (_dot(hb2, rwl_ref[...]) + _dot(hl2, rwh_ref[...]))
              + rb_ref[...])
    lane = lax.broadcasted_iota(jnp.int32, logits.shape, 1).astype(F32)
    work = jnp.where(lane < N_EXP, logits, -jnp.inf)
    sel = jnp.zeros(logits.shape, F32)
    hits, exps = [], []
    top0 = None
    denom = None
    for kk in range(TOP_K):
        mx = jnp.max(work, axis=1, keepdims=True)
        first = jnp.min(jnp.where(work == mx, lane, float(LANES)), axis=1, keepdims=True)
        hit = lane == first
        if kk == 0:
            top0 = mx
        e = jnp.exp(mx - top0)
        denom = e if kk == 0 else denom + e
        hits.append(hit)
        exps.append(e)
        sel = jnp.where(hit, 1.0, sel)
        work = jnp.where(hit, -jnp.inf, work)
    r_i = lax.broadcasted_iota(jnp.int32, (TT, TT), 0)
    c_i = lax.broadcasted_iota(jnp.int32, (TT, TT), 1)
    within = _dot(_ones_where(c_i < r_i), sel.astype(BF16))
    tile_cnt = jnp.sum(sel, axis=0, keepdims=True)
    a_i = lax.broadcasted_iota(jnp.int32, (LANES, LANES), 0)
    b_i = lax.broadcasted_iota(jnp.int32, (LANES, LANES), 1)
    tile_off = _dot(jnp.broadcast_to(tile_cnt, (8, LANES)).astype(BF16), _ones_where(a_i < b_i))[0:1, :]
    local = tile_off + within
    info = jnp.zeros(logits.shape, F32)
    for kk in range(TOP_K):
        pos = jnp.sum(jnp.where(hits[kk], local, 0.0), axis=1, keepdims=True)
        info = jnp.where(lane == float(kk), pos, info)
        info = jnp.where(lane == float(TOP_K + kk), exps[kk] / denom, info)
    info_ref[...] = info
    infot_ref[...] = info.T[0:8, :]
    stats_ref[0] = jnp.concatenate([cnt_scr[0:1, :], tile_cnt, tile_off, jnp.zeros((5, LANES), F32)], axis=0)
    cnt_scr[0:1, :] = cnt_scr[0:1, :] + tile_cnt


def _post(cond_tbl, x, mod, yfc, yfl, o_f, o_b, h_f, h_b, gz, mo, gates, gn, mn, wbf, wbg, wbm, wo,
          n2, rwh, rwl, rb):
    tile = lambda w: pl.BlockSpec((TT, w), lambda i, t: (i, 0))
    const = lambda shape: pl.BlockSpec(shape, lambda i, t: (0,) * len(shape))
    in_specs = [tile(D), pl.BlockSpec((1, N_MOD, D), lambda i, t: (t[i], 0, 0)),
                pl.BlockSpec((TT, HW), lambda i, t: (jnp.minimum(i, CTX_TILES - 1), 0)),
                pl.BlockSpec((TT, HW), lambda i, t: (jnp.maximum(i - CTX_TILES, 0), 0)),
                tile(HW), tile(HW), tile(HW), tile(HW),
                tile(HW), tile(HW), tile(3 * D),
                const((1, DH)), const((1, HW)), const((HW, D)), const((HW, D)), const((HW, D)),
                const((D, D)), const((1, D)), const((D, LANES)), const((D, LANES)), const((1, LANES))]
    return pl.pallas_call(
        _post_kernel,
        grid_spec=pltpu.PrefetchScalarGridSpec(
            num_scalar_prefetch=1, grid=(N_TILES,), in_specs=in_specs,
            out_specs=[tile(D), tile(D), tile(LANES), pl.BlockSpec((8, TT), lambda i, t: (0, i)),
                       pl.BlockSpec((1, 8, LANES), lambda i, t: (i, 0, 0))],
            scratch_shapes=[pltpu.VMEM((8, LANES), F32)]),
        out_shape=[jax.ShapeDtypeStruct((N_TOK, D), F32), jax.ShapeDtypeStruct((N_TOK, D), BF16),
                   jax.ShapeDtypeStruct((N_TOK, LANES), F32), jax.ShapeDtypeStruct((8, N_TOK), F32),
                   jax.ShapeDtypeStruct((N_TILES, 8, LANES), F32)],
        compiler_params=_cparams("arbitrary"),
    )(cond_tbl, x, mod, yfc, yfl, o_f, o_b, h_f, h_b, gz, mo, gates, gn, mn, wbf, wbg, wbm, wo,
      n2, rwh, rwl, rb)


PAIRS = TT * TOP_K
RUN_BITS = tuple(1 << b for b in range(8, -1, -1))


def _slot_rows(start_slot, n_slots):
    return pl.ds(pl.multiple_of(start_slot * 8, 8), n_slots * 8)


def _for_each_run(base, cnt_ref, fn):
    def body(e, carry):
        cnt = cnt_ref[base + e]
        for bit in RUN_BITS:
            @pl.when((cnt & bit) != 0)
            def _():
                fn(base + e, cnt & (-2 * bit), bit)
        return carry
    lax.fori_loop(0, N_EXP, body, 0)


def _dispatch_kernel(ss_ref, cnt_ref, toff_ref, h2_ref, infot_ref, xs_hbm, stage, sem):
    i = pl.program_id(0)
    buf = i % 2
    whole = lambda b: pltpu.make_async_copy(stage.at[b], xs_hbm.at[pl.ds(0, PAIRS * 8)], sem.at[b])

    @pl.when(i >= 2)
    def _():
        whole(buf).wait()

    r = lax.broadcasted_iota(jnp.int32, (PAIRS, TT), 0).astype(F32)
    pick = r == infot_ref[0:1, :]
    for kk in range(1, TOP_K):
        pick = jnp.logical_or(pick, r == infot_ref[kk:kk + 1, :])
    rows = _dot(_ones_where(pick), h2_ref[...])
    for cc in range(8):
        stage[buf, pl.ds(cc, PAIRS, stride=8), :] = rows[:, cc * LANES:(cc + 1) * LANES]

    def send(j, o, n):
        pltpu.make_async_copy(stage.at[buf, _slot_rows(toff_ref[j] + o, n)],
                              xs_hbm.at[_slot_rows(ss_ref[j] + o, n)], sem.at[buf]).start()

    _for_each_run(i * N_EXP, cnt_ref, send)

    @pl.when(i == N_TILES - 1)
    def _():
        whole(1 - buf).wait()
        whole(buf).wait()


def _dispatch(slot_start, seg_cnt, seg_off, h2, info_t):
    return pl.pallas_call(
        _dispatch_kernel,
        grid_spec=pltpu.PrefetchScalarGridSpec(
            num_scalar_prefetch=3, grid=(N_TILES,),
            in_specs=[pl.BlockSpec((TT, D), lambda i, *_: (i, 0)), pl.BlockSpec((8, TT), lambda i, *_: (0, i))],
            out_specs=pl.BlockSpec(memory_space=pl.ANY),
            scratch_shapes=[pltpu.VMEM((2, PAIRS * 8, LANES), F32), pltpu.SemaphoreType.DMA((2,))]),
        out_shape=jax.ShapeDtypeStruct((S_MAX * 8, LANES), F32),
        compiler_params=_cparams("arbitrary"),
    )(slot_start, seg_cnt, seg_off, h2, info_t)


def _gmm_kernel(te_ref, nu_ref, valid_ref, x_ref, wg_ref, bg_ref, wu_ref, bu_ref, wd_ref, bd_ref, o_ref,
                wg_scr, wu_scr, wd_scr):
    i = pl.program_id(0)
    used = i < nu_ref[0]

    @pl.when(used)
    def _():
        e = te_ref[i]
        e_prev = te_ref[jnp.maximum(i - 1, 0)]

        @pl.when(jnp.logical_or(i == 0, e != e_prev))
        def _():
            wg_scr[...] = wg_ref[0].astype(BF16)
            wu_scr[...] = wu_ref[0].astype(BF16)
            wd_scr[...] = wd_ref[0].astype(BF16)

        x = jnp.concatenate([x_ref[pl.ds(cc, TM, stride=8), :] for cc in range(8)], axis=1)
        row = lax.broadcasted_iota(jnp.int32, (TM, 1), 0)
        x = jnp.where(row < valid_ref[i], x, 0.0).astype(BF16)
        gate = jnp.minimum(_dot(x, wg_scr[...]) + bg_ref[0], SWIGLU_LIMIT)
        up = jnp.clip(_dot(x, wu_scr[...]) + bu_ref[0], -SWIGLU_LIMIT, SWIGLU_LIMIT)
        act = (up + 1.0) * gate * _sigmoid(SWIGLU_ALPHA * gate)
        y = _dot(act.astype(BF16), wd_scr[...]) + bd_ref[0]
        for cc in range(8):
            o_ref[pl.ds(cc, TM, stride=8), :] = y[:, cc * LANES:(cc + 1) * LANES]

    @pl.when(jnp.logical_not(used))
    def _():
        o_ref[...] = jnp.zeros_like(o_ref)


def _gmm(tile_exp, n_used, valid, xs, wg, bg, wu, bu, wd, bd):
    wspec = pl.BlockSpec((1, D, D), lambda i, te, nu, va: (te[i], 0, 0))
    bspec = pl.BlockSpec((1, 1, D), lambda i, te, nu, va: (te[i], 0, 0))
    return pl.pallas_call(
        _gmm_kernel,
        grid_spec=pltpu.PrefetchScalarGridSpec(
            num_scalar_prefetch=3, grid=(NT_MAX,),
            in_specs=[pl.BlockSpec((TM * 8, LANES), lambda i, te, nu, va: (jnp.minimum(i, nu[0] - 1), 0)),
                      wspec, bspec, wspec, bspec, wspec, bspec],
            out_specs=pl.BlockSpec((TM * 8, LANES), lambda i, te, nu, va: (i, 0)),
            scratch_shapes=[pltpu.VMEM((D, D), BF16)] * 3),
        out_shape=jax.ShapeDtypeStruct((S_MAX * 8, LANES), F32),
        compiler_params=_cparams("arbitrary"),
    )(tile_exp, n_used, valid, xs, wg, bg.reshape(N_EXP, 1, D), wu, bu.reshape(N_EXP, 1, D),
      wd, bd.reshape(N_EXP, 1, D))


def _combine_kernel(tbl_ref, ss_ref, cnt_ref, toff_ref, y_hbm, info_ref, x_ref, mod_ref, g_ref, *refs,
                    final):
    outs, (ybuf, sem) = refs[:-2], refs[-2:]
    i = pl.program_id(0)
    buf = i % 2

    def fetch(tile, b):
        def recv(j, o, n):
            pltpu.make_async_copy(y_hbm.at[_slot_rows(ss_ref[j] + o, n)],
                                  ybuf.at[b, _slot_rows(toff_ref[j] + o, n)], sem.at[b]).start()
        _for_each_run(tile * N_EXP, cnt_ref, recv)

    @pl.when(i == 0)
    def _():
        fetch(0, 0)

    @pl.when(i + 1 < N_TILES)
    def _():
        fetch(i + 1, 1 - buf)

    r = lax.broadcasted_iota(jnp.int32, (TT, PAIRS), 1).astype(F32)
    wm = jnp.zeros((TT, PAIRS), F32)
    for kk in range(TOP_K):
        wm = jnp.where(r == info_ref[:, kk:kk + 1], info_ref[:, TOP_K + kk:TOP_K + kk + 1], wm)
    pltpu.make_async_copy(y_hbm.at[pl.ds(0, PAIRS * 8)], ybuf.at[buf], sem.at[buf]).wait()
    y = jnp.concatenate([ybuf[buf, pl.ds(cc, PAIRS, stride=8), :] for cc in range(8)], axis=1)
    x = x_ref[...] + mod_ref[0, 5:6, :] * _dot(wm.astype(BF16), y.astype(BF16))
    if not final:
        outs[0][...] = x
    else:
        res = _rms(x) * g_ref[...]

        @pl.when(i < CTX_TILES)
        def _():
            outs[0][...] = res

        @pl.when(i >= CTX_TILES)
        def _():
            outs[1][...] = res


def _combine(final, cond_tbl, slot_start, seg_cnt, seg_off, y_sorted, info, x, mod, g):
    tile = lambda w: pl.BlockSpec((TT, w), lambda i, *_: (i, 0))
    if final:
        out_specs = [pl.BlockSpec((TT, D), lambda i, *_: (jnp.minimum(i, CTX_TILES - 1), 0)),
                     pl.BlockSpec((TT, D), lambda i, *_: (jnp.maximum(i - CTX_TILES, 0), 0))]
        out_shape = [jax.ShapeDtypeStruct((N_CTX, D), F32), jax.ShapeDtypeStruct((N_LAT, D), F32)]
    else:
        out_specs = [tile(D)]
        out_shape = [jax.ShapeDtypeStruct((N_TOK, D), F32)]
    return pl.pallas_call(
        functools.partial(_combine_kernel, final=final),
        grid_spec=pltpu.PrefetchScalarGridSpec(
            num_scalar_prefetch=4, grid=(N_TILES,),
            in_specs=[pl.BlockSpec(memory_space=pl.ANY), tile(LANES), tile(D),
                      pl.BlockSpec((1, N_MOD, D), lambda i, t, *_: (t[i], 0, 0)),
                      pl.BlockSpec((1, D), lambda i, *_: (0, 0))],
            out_specs=out_specs,
            scratch_shapes=[pltpu.VMEM((2, PAIRS * 8, LANES), F32), pltpu.SemaphoreType.DMA((2,))]),
        out_shape=out_shape,
        compiler_params=_cparams("arbitrary"),
    )(cond_tbl, slot_start, seg_cnt, seg_off, y_sorted, info, x, mod, g)


def _dft_consts(t_len):
    k = np.arange(t_len, dtype=np.int64)
    ang = 2.0 * np.pi * ((k[:, None] * k[None, :]) % t_len).astype(np.float64) / t_len
    return np.concatenate([np.cos(ang), -np.sin(ang)], axis=1).astype(np.float32)


def _pos_table():
    quarter = D // 4
    omega = 1.0 / (10000.0 ** (np.arange(quarter, dtype=np.float32) / np.float32(quarter)))
    omega = omega.astype(np.float32).astype(np.float64)
    t = np.arange(T_LAT)
    ang_r = (t // GRID_W).astype(np.float64)[:, None] * omega
    ang_c = (t % GRID_W).astype(np.float64)[:, None] * omega
    return np.concatenate([np.sin(ang_r), np.cos(ang_r), np.sin(ang_c), np.cos(ang_c)],
                          axis=-1).astype(np.float32)


def _repack_w_in(w):
    f, gq, gk, gv, gz, ga, gb, mq, mk, mv, mo, mi, mf, g1, g2, g3 = jnp.split(
        w, np.cumsum([512, 512, 512, 512, 512, 8, 8, 512, 512, 512, 512, 8, 8, 1024, 1024])[:].tolist(),
        axis=1)
    main = jnp.concatenate([f, gq, gk, gv, gz, mq, mk, mv, mo, g1, g2, g3], axis=1).astype(BF16)
    small = jnp.concatenate([ga, gb, mi, mf, jnp.zeros((D, LANES - 32), F32)], axis=1)
    hi = small.astype(BF16)
    lo = (small - hi.astype(F32)).astype(BF16)
    return main, hi, lo


def _route_glue(stats):
    seg_base = stats[:, 0, :N_EXP].astype(jnp.int32)
    seg_cnt = stats[:, 1, :N_EXP].astype(jnp.int32)
    seg_off = stats[:, 2, :N_EXP].astype(jnp.int32)
    counts = seg_base[-1] + seg_cnt[-1]
    region = ((counts + TM - 1) // TM) * TM
    ends = jnp.cumsum(region)
    off = ends - region
    n_used = ends[-1] // TM
    tile_start = jnp.arange(NT_MAX, dtype=jnp.int32) * TM
    tile_exp = jnp.minimum(jnp.sum((tile_start[:, None] >= ends[None, :]).astype(jnp.int32), axis=1), N_EXP - 1)
    last_exp = jnp.take(tile_exp, jnp.maximum(n_used - 1, 0))
    tile_exp = jnp.where(tile_start < ends[-1], tile_exp, last_exp)
    valid = jnp.clip(jnp.take(off + counts, tile_exp) - tile_start, 0, TM)
    slot_start = off[None, :] + seg_base
    return (tile_exp.astype(jnp.int32), n_used.reshape(1).astype(jnp.int32), valid.astype(jnp.int32),
            slot_start.reshape(-1), seg_cnt.reshape(-1), seg_off.reshape(-1))


def kernel(x_prompt, x_sample, state_gdn, state_mlstm_c, state_mlstm_n, state_mlstm_m, c, c_ctx,
           w_ada, b_ada, norm1_g, norm2_g, w_in, gdn_conv_w, gdn_a_log, gdn_dt_bias, gdn_norm_g,
           mlstm_i_bias, mlstm_f_bias, mlstm_norm_g, w_branch_f, w_branch_g, w_branch_m, w_out,
           router_w, router_b, exp_w_gate, exp_b_gate, exp_w_up, exp_b_up, exp_w_down, exp_b_down,
           final_norm_g):
    cond_tbl = jnp.asarray(np.concatenate([np.zeros(CTX_TILES, np.int32),
                                           1 + np.arange(N_TILES - CTX_TILES, dtype=np.int32) // LAT_TILES_PER_SEQ]))
    tables = tuple(jnp.asarray(t) for t in _scan_tables())
    cond8 = jnp.concatenate([c_ctx[None, :], c, jnp.zeros((8 - 1 - B_LAT, D), F32)], axis=0)
    mod_all = _ada(cond8, w_ada, b_ada).reshape(DEPTH, 8, N_MOD, D)

    pos = jnp.asarray(_pos_table())
    dft_c = jnp.asarray(_dft_consts(T_CTX)).astype(BF16)
    dft_l = jnp.asarray(_dft_consts(T_LAT)).astype(BF16)
    ang = 2.0 * np.pi * ((np.arange(DH)[:, None] * np.arange(DH)[None, :]) % DH) / DH
    c128 = jnp.asarray(np.cos(ang).astype(np.float32)).astype(BF16)
    s128 = jnp.asarray(np.sin(ang).astype(np.float32)).astype(BF16)

    zeros_s = jnp.zeros((B_CTX, 2, H, DH, DH), F32)
    zeros_v = jnp.zeros((B_CTX, 2, H, LANES), F32)
    x = None
    gdn_states, c_states, n_states, m_states = [], [], [], []
    y_prompt = y_sample = None
    for l in range(DEPTH):
        mod = mod_all[l]
        wm, wsh, wsl = _repack_w_in(w_in[l])
        if l == 0:
            outs = _pre(True, (x_prompt.reshape(N_CTX, D), x_sample.reshape(N_LAT, D), pos), mod,
                        norm1_g[l][None, :], wm, wsh, wsl, cond_tbl)
            x, outs = outs[0], outs[1:]
        else:
            outs = _pre(False, x, mod, norm1_g[l][None, :], wm, wsh, wsl, cond_tbl)
        f_all, gqkv, gz, mqkv, mo, gates, sm = outs

        yfc = _fourier(f_all, dft_c, c128, s128, B_CTX, T_CTX, 0)
        yfl = _fourier(f_all, dft_l, c128, s128, B_LAT, T_LAT, N_CTX)

        prm = jnp.zeros((8, LANES), F32)
        prm = prm.at[0, 0:8].set(gdn_a_log[l].reshape(-1))
        prm = prm.at[1, 0:8].set(gdn_dt_bias[l].reshape(-1))
        prm = prm.at[1, 16:24].set(mlstm_i_bias[l].reshape(-1))
        prm = prm.at[1, 24:32].set(mlstm_f_bias[l].reshape(-1))
        qkv_c, gts = _prep2(gqkv, sm, gdn_conv_w[l], prm)

        s0 = jnp.concatenate([zeros_s, state_gdn[:, l]], axis=0)
        o_f, o_b, s_fin = _gdn_rec(tables, *_gdn_wy(qkv_c, gts), s0)
        c0 = jnp.concatenate([zeros_s, state_mlstm_c[:, l]], axis=0)
        n0 = jnp.concatenate([zeros_v, state_mlstm_n[:, l]], axis=0)
        m0 = jnp.concatenate([zeros_v, jnp.broadcast_to(state_mlstm_m[:, l][..., None],
                                                         (B_LAT, 2, H, LANES))], axis=0)
        h_f, h_b, c_fin, n_fin, m_fin = _mlstm_scan(tables, mqkv, gts, c0, n0, m0)
        gdn_states.append(s_fin[:B_CTX])
        c_states.append(c_fin[:B_CTX])
        n_states.append(n_fin[:B_CTX])
        m_states.append(m_fin[:B_CTX, :, :, 0])

        rw = jnp.concatenate([router_w[l], jnp.zeros((D, LANES - N_EXP), F32)], axis=1)
        rwh = rw.astype(BF16)
        rwl = (rw - rwh.astype(F32)).astype(BF16)
        rb = jnp.concatenate([router_b[l], jnp.zeros((LANES - N_EXP,), F32)])[None, :]
        x, h2, info, info_t, stats = _post(
            cond_tbl, x, mod, yfc, yfl, o_f, o_b, h_f, h_b, gz, mo, gates,
            gdn_norm_g[l][None, :], mlstm_norm_g[l][None, :],
            w_branch_f[l].astype(BF16), w_branch_g[l].astype(BF16), w_branch_m[l].astype(BF16),
            w_out[l].astype(BF16), norm2_g[l][None, :], rwh, rwl, rb)

        tile_exp, n_used, valid, slot_start, seg_cnt, seg_off = _route_glue(stats)
        xs_sorted = _dispatch(slot_start, seg_cnt, seg_off, h2, info_t)
        y_sorted = _gmm(tile_exp, n_used, valid, xs_sorted, exp_w_gate[l], exp_b_gate[l], exp_w_up[l],
                        exp_b_up[l], exp_w_down[l], exp_b_down[l])
        res = _combine(l + 1 == DEPTH, cond_tbl, slot_start, seg_cnt, seg_off, y_sorted, info, x, mod,
                       final_norm_g[None, :])
        if l + 1 < DEPTH:
            x = res[0]
        else:
            y_prompt, y_sample = res

    return (y_prompt.reshape(B_CTX, T_CTX, D), y_sample.reshape(B_LAT, T_LAT, D),
            jnp.stack(gdn_states, axis=1), jnp.stack(c_states, axis=1),
            jnp.stack(n_states, axis=1), jnp.stack(m_states, axis=1))
```

```python
import functools
import math

import numpy as np
import jax
import jax.numpy as jnp
from jax import lax
from jax.experimental import pallas as pl
from jax.experimental.pallas import tpu as pltpu

F32 = jnp.float32
BF16 = jnp.bfloat16

D = 1024
DEPTH = 2
B_CTX, T_CTX = 16, 256
B_LAT, T_LAT = 4, 2048
N_CTX = B_CTX * T_CTX
N_LAT = B_LAT * T_LAT
N_TOK = N_CTX + N_LAT
N_SEQ = B_CTX + B_LAT
GRID_W = 64
H = 4
DH = 128
HW = H * DH
CH = 64
TT = 256
N_TILES = N_TOK // TT
CTX_TILES = N_CTX // TT
LAT_TILES_PER_SEQ = T_LAT // TT
N_CHUNKS = N_TOK // CH
N_EXP = 32
TOP_K = 4
N_MOD = 6
EPS = 1e-6
SWIGLU_ALPHA = 1.702
SWIGLU_LIMIT = 7.0
LANES = 128
TM = 256
FF_CHUNK = 256
S_MAX = N_TOK * TOP_K + N_EXP * (TM - 1)
S_MAX = ((S_MAX + TM - 1) // TM) * TM
NT_MAX = S_MAX // TM

W_MAIN = 512 + 1536 + 512 + 1536 + 512 + 3072


def _dot(a, b):
    return jnp.dot(a, b, preferred_element_type=F32)


def _dot_nt(a, b):
    return lax.dot_general(a, b, (((1,), (1,)), ((), ())), preferred_element_type=F32)


def _dot_tn(a, b):
    return lax.dot_general(a, b, (((0,), (0,)), ((), ())), preferred_element_type=F32)


def _split2(a):
    hi = a.astype(BF16)
    lo = (a - hi.astype(F32)).astype(BF16)
    return hi, lo


def _split3(a):
    hi = a.astype(BF16)
    r = a - hi.astype(F32)
    mid = r.astype(BF16)
    lo = (r - mid.astype(F32)).astype(BF16)
    return hi, mid, lo


def _dot3(a, b):
    ah, al = _split2(a)
    bh, bl = _split2(b)
    return _dot(ah, bh) + (_dot(ah, bl) + _dot(al, bh))


def _dot_exact_lhs(a_bf16, b):
    bh, bm, bl = _split3(b)
    return _dot(a_bf16, bh) + (_dot(a_bf16, bm) + _dot(a_bf16, bl))


def _sigmoid(x):
    return 1.0 / (1.0 + jnp.exp(-x))


def _silu(x):
    return x * _sigmoid(x)


def _softplus(x):
    return jnp.maximum(x, 0.0) + jnp.log(1.0 + jnp.exp(-jnp.abs(x)))


def _rms(x):
    return x * lax.rsqrt(jnp.mean(x * x, axis=-1, keepdims=True) + EPS)


def _cparams(*sem):
    return pltpu.CompilerParams(dimension_semantics=tuple(sem))


def _ada_kernel(c_ref, w_ref, b_ref, o_ref):
    o_ref[0] = _dot3(_silu(c_ref[...]), w_ref[0]) + b_ref[0]


def _ada(cond8, w_ada, b_ada):
    nb = 1536
    return pl.pallas_call(
        _ada_kernel,
        grid=(DEPTH, N_MOD * D // nb),
        in_specs=[pl.BlockSpec((8, D), lambda l, j: (0, 0)),
                  pl.BlockSpec((1, D, nb), lambda l, j: (l, 0, j)),
                  pl.BlockSpec((1, 1, nb), lambda l, j: (l, 0, j))],
        out_specs=pl.BlockSpec((1, 8, nb), lambda l, j: (l, 0, j)),
        out_shape=jax.ShapeDtypeStruct((DEPTH, 8, N_MOD * D), F32),
        compiler_params=_cparams("parallel", "parallel"),
    )(cond8, w_ada, b_ada.reshape(DEPTH, 1, N_MOD * D))


def _pre_body(x, mod_ref, n1_ref, wm_ref, wsh_ref, wsl_ref, outs):
    f_ref, gqkv_ref, gz_ref, mqkv_ref, mo_ref, gates_ref, sm_ref = outs
    h = _rms(x) * n1_ref[...]
    h = h * (1.0 + mod_ref[0, 1:2, :]) + mod_ref[0, 0:1, :]
    hb, hl = _split2(h)
    off = 0
    for ref, width in ((f_ref, 512), (gqkv_ref, 1536), (gz_ref, 512), (mqkv_ref, 1536),
                       (mo_ref, 512), (gates_ref, 3072)):
        ref[...] = _dot(hb, wm_ref[:, off:off + width]).astype(BF16)
        off += width
    sm_ref[...] = _dot(hb, wsh_ref[...]) + (_dot(hb, wsl_ref[...]) + _dot(hl, wsh_ref[...]))


def _pre_first_kernel(tbl_ref, xp_ref, xs_ref, pos_ref, mod_ref, n1_ref, wm_ref, wsh_ref, wsl_ref,
                      x_out_ref, *outs):
    i = pl.program_id(0)
    x = jnp.where(i < CTX_TILES, xp_ref[...], xs_ref[...] + pos_ref[...])
    x_out_ref[...] = x
    _pre_body(x, mod_ref, n1_ref, wm_ref, wsh_ref, wsl_ref, outs)


def _pre_next_kernel(tbl_ref, x_ref, mod_ref, n1_ref, wm_ref, wsh_ref, wsl_ref, *outs):
    _pre_body(x_ref[...], mod_ref, n1_ref, wm_ref, wsh_ref, wsl_ref, outs)


_PRE_OUT_WIDTHS = ((512, BF16), (1536, BF16), (512, BF16), (1536, BF16), (512, BF16), (3072, BF16),
                   (LANES, F32))


def _pre(first, xs_in, mod, n1, wm, wsh, wsl, cond_tbl):
    tile = lambda w: pl.BlockSpec((TT, w), lambda i, t: (i, 0))
    const = lambda shape: pl.BlockSpec(shape, lambda i, t: (0,) * len(shape))
    w_specs = [pl.BlockSpec((1, N_MOD, D), lambda i, t: (t[i], 0, 0)), const((1, D)),
               const((D, W_MAIN)), const((D, LANES)), const((D, LANES))]
    out_specs = [tile(w) for w, _ in _PRE_OUT_WIDTHS]
    out_shape = [jax.ShapeDtypeStruct((N_TOK, w), dt) for w, dt in _PRE_OUT_WIDTHS]
    if first:
        x_prompt2, x_sample2, pos = xs_in
        in_specs = [pl.BlockSpec((TT, D), lambda i, t: (jnp.minimum(i, CTX_TILES - 1), 0)),
                    pl.BlockSpec((TT, D), lambda i, t: (jnp.maximum(i - CTX_TILES, 0), 0)),
                    pl.BlockSpec((TT, D), lambda i, t: (jnp.maximum(i - CTX_TILES, 0) % LAT_TILES_PER_SEQ, 0))]
        kern = _pre_first_kernel
        out_specs = [tile(D)] + out_specs
        out_shape = [jax.ShapeDtypeStruct((N_TOK, D), F32)] + out_shape
        args = (x_prompt2, x_sample2, pos)
    else:
        in_specs = [tile(D)]
        kern = _pre_next_kernel
        args = (xs_in,)
    return pl.pallas_call(
        kern,
        grid_spec=pltpu.PrefetchScalarGridSpec(
            num_scalar_prefetch=1, grid=(N_TILES,), in_specs=in_specs + w_specs, out_specs=out_specs),
        out_shape=out_shape,
        compiler_params=_cparams("parallel"),
    )(cond_tbl, *args, mod, n1, wm, wsh, wsl)


def _fourier_kernel(x_ref, dft_ref, c_ref, s_ref, o_ref, z_scr, *, t_len, scale):
    @pl.when(pl.program_id(1) == 0)
    def _():
        for g in range(H):
            xg = x_ref[:, g * DH:(g + 1) * DH]
            z_scr[0:t_len, g * DH:(g + 1) * DH] = _dot(xg, c_ref[...]).astype(BF16)
            z_scr[t_len:2 * t_len, g * DH:(g + 1) * DH] = _dot(xg, s_ref[...]).astype(BF16)

    o_ref[...] = (_dot(dft_ref[...], z_scr[...]) * scale).astype(BF16)


def _fourier(f_all, dft, c128, s128, n_seq, t_len, row0):
    tr = min(t_len, 512)
    blk0 = row0 // t_len
    return pl.pallas_call(
        functools.partial(_fourier_kernel, t_len=t_len, scale=1.0 / math.sqrt(t_len * DH)),
        grid=(n_seq, t_len // tr),
        in_specs=[pl.BlockSpec((t_len, HW), lambda b, r: (blk0 + b, 0)),
                  pl.BlockSpec((tr, 2 * t_len), lambda b, r: (r, 0)),
                  pl.BlockSpec((DH, DH), lambda b, r: (0, 0)),
                  pl.BlockSpec((DH, DH), lambda b, r: (0, 0))],
        out_specs=pl.BlockSpec((tr, HW), lambda b, r: (b * (t_len // tr) + r, 0)),
        out_shape=jax.ShapeDtypeStruct((n_seq * t_len, HW), BF16),
        scratch_shapes=[pltpu.VMEM((2 * t_len, HW), BF16)],
        compiler_params=_cparams("parallel", "arbitrary"),
    )(f_all, dft, c128, s128)


HALO = 16


def _prep2_kernel(cur_ref, prev_ref, next_ref, sm_ref, cw_ref, prm_ref, qkv_ref, g_ref):
    i = pl.program_id(0)
    j = jnp.maximum(i - CTX_TILES, 0) % LAT_TILES_PER_SEQ
    is_lat = i >= CTX_TILES
    has_prev = jnp.logical_and(is_lat, j > 0)
    has_next = jnp.logical_and(is_lat, j < LAT_TILES_PER_SEQ - 1)
    x = cur_ref[...].astype(F32)
    row = lax.broadcasted_iota(jnp.int32, x.shape, 0)
    prev_row = jnp.where(has_prev, prev_ref[HALO - 1:HALO, :].astype(F32), 0.0)
    next_row = jnp.where(has_next, next_ref[0:1, :].astype(F32), 0.0)
    xp = jnp.where(row == 0, prev_row, pltpu.roll(x, 1, 0))
    xn = jnp.where(row == TT - 1, next_row, pltpu.roll(x, TT - 1, 0))
    y = _silu(cw_ref[0:1, :] * xp + cw_ref[1:2, :] * x + cw_ref[2:3, :] * xn)
    for h in range(2 * H):
        seg = y[:, h * DH:(h + 1) * DH]
        seg = seg * lax.rsqrt(jnp.sum(seg * seg, axis=-1, keepdims=True) + EPS)
        if h < H:
            seg = seg * (DH ** -0.5)
        qkv_ref[:, h * DH:(h + 1) * DH] = seg.astype(BF16)
    qkv_ref[:, 2 * HW:3 * HW] = y[:, 2 * HW:3 * HW].astype(BF16)
    z = sm_ref[...] + prm_ref[1:2, :]
    lane = lax.broadcasted_iota(jnp.int32, z.shape, 1)
    g_log = -jnp.exp(prm_ref[0:1, :]) * _softplus(z)
    out = jnp.where(lane < 8, g_log,
                    jnp.where(lane < 16, _sigmoid(z), jnp.where(lane < 24, z, -_softplus(-z))))
    g_ref[...] = out


def _prep2(gqkv, sm, conv_w, prm):
    nb = N_TOK // HALO
    return pl.pallas_call(
        _prep2_kernel,
        grid=(N_TILES,),
        in_specs=[pl.BlockSpec((TT, 3 * HW), lambda i: (i, 0)),
                  pl.BlockSpec((HALO, 3 * HW), lambda i: (jnp.maximum(i * (TT // HALO) - 1, 0), 0)),
                  pl.BlockSpec((HALO, 3 * HW), lambda i: (jnp.minimum((i + 1) * (TT // HALO), nb - 1), 0)),
                  pl.BlockSpec((TT, LANES), lambda i: (i, 0)),
                  pl.BlockSpec((3, 3 * HW), lambda i: (0, 0)),
                  pl.BlockSpec((8, LANES), lambda i: (0, 0))],
        out_specs=[pl.BlockSpec((TT, 3 * HW), lambda i: (i, 0)),
                   pl.BlockSpec((TT, LANES), lambda i: (i, 0))],
        out_shape=[jax.ShapeDtypeStruct((N_TOK, 3 * HW), BF16),
                   jax.ShapeDtypeStruct((N_TOK, LANES), F32)],
        compiler_params=_cparams("parallel"),
    )(gqkv, gqkv, gqkv, sm, conv_w, prm)


def _scan_tables():
    blk_f = np.zeros(N_CHUNKS, np.int32)
    blk_b = np.zeros(N_CHUNKS, np.int32)
    seq = np.zeros(N_CHUNKS, np.int32)
    first = np.zeros(N_CHUNKS, np.int32)
    last = np.zeros(N_CHUNKS, np.int32)
    step = 0
    base = 0
    sid = 0
    for n_seq, t_len in ((B_CTX, T_CTX), (B_LAT, T_LAT)):
        nc = t_len // CH
        for _ in range(n_seq):
            for c in range(nc):
                blk_f[step] = base + c
                blk_b[step] = base + nc - 1 - c
                seq[step] = sid
                first[step] = int(c == 0)
                last[step] = int(c == nc - 1)
                step += 1
            base += nc
            sid += 1
    return blk_f, blk_b, seq, first, last


def _ones_where(mask):
    return jnp.where(mask, 1.0, 0.0).astype(BF16)


def _scan_order_mask(d, n=CH):
    row = lax.broadcasted_iota(jnp.int32, (n, n), 0)
    col = lax.broadcasted_iota(jnp.int32, (n, n), 1)
    return (col <= row) if d == 0 else (col >= row)


def _row_of(col_vec):
    return jnp.broadcast_to(col_vec, (col_vec.shape[0], LANES)).T[0:1, :]


HC = H * CH


def _unit_tri_inverses(lbs, row, col):
    same = lambda sh: jnp.right_shift(row, sh) == jnp.right_shift(col, sh)
    zero = jnp.zeros_like(lbs[0])
    eye = jnp.where(row == col, 1.0, 0.0).astype(BF16)
    ms = [jnp.where(same(3), -lb, zero) for lb in lbs]
    m2s = [_dot(m, m).astype(BF16) for m in ms]
    m4s = [_dot(m2, m2).astype(BF16) for m2 in m2s]
    xs = [eye + m for m in ms]
    xs = [(x.astype(F32) + _dot(x, m2)).astype(BF16) for x, m2 in zip(xs, m2s)]
    xs = [(x.astype(F32) + _dot(x, m4)).astype(BF16) for x, m4 in zip(xs, m4s)]
    for sh in (3, 4, 5):
        pair = jnp.logical_and(same(sh + 1), jnp.logical_not(same(sh)))
        exs = [_dot(jnp.where(pair, lb, zero), x).astype(BF16) for lb, x in zip(lbs, xs)]
        fills = [_dot(x, ex) for x, ex in zip(xs, exs)]
        xs = [jnp.where(pair, (-fill).astype(BF16), x) for fill, x in zip(fills, xs)]
    return xs


def _head_stack(ref, r0, base):
    return jnp.concatenate([ref[r0:r0 + CH, base + h * DH:base + (h + 1) * DH] for h in range(H)], axis=0)


def _gdn_wy_kernel(qkv_ref, g_ref, u_ref, w_ref, qg_ref, kg_ref, attn_ref, gl_ref):
    row = lax.broadcasted_iota(jnp.int32, (HC, HC), 0)
    col = lax.broadcasted_iota(jnp.int32, (HC, HC), 1)
    same_head = jnp.right_shift(row, 6) == jnp.right_shift(col, 6)
    lane_head = jnp.right_shift(lax.broadcasted_iota(jnp.int32, (CH, HC), 1), 6)
    subs = range(TT // CH)
    chains = [(s, d) for s in subs for d in range(2)]
    g = [g_ref[s * CH:(s + 1) * CH, :] for s in subs]
    q_st = [_head_stack(qkv_ref, s * CH, 0) for s in subs]
    k_st = [_head_stack(qkv_ref, s * CH, HW) for s in subs]
    v_st = [_head_stack(qkv_ref, s * CH, 2 * HW).astype(F32) for s in subs]
    kf = [k.astype(F32) for k in k_st]
    kk = [_dot_nt(k, k) for k in k_st]
    qk = [_dot_nt(q, k) for q, k in zip(q_st, k_st)]
    order = [_ones_where(_scan_order_mask(d)) for d in range(2)]
    g3 = [_split3(a) for a in g]
    cols = [_dot(order[d], g3[s][0]) + (_dot(order[d], g3[s][1]) + _dot(order[d], g3[s][2]))
            for s, d in chains]
    incl = [jnp.logical_and(same_head, (col <= row) if d == 0 else (col >= row)) for d in range(2)]
    strict = [jnp.logical_and(same_head, (col < row) if d == 0 else (col > row)) for d in range(2)]
    gc, beta, g_last, decay, l_hi, l_lo, egc, rhs = [], [], [], [], [], [], [], []
    for c, (s, d) in enumerate(chains):
        last = CH - 1 if d == 0 else 0
        lanes = [d * H + h for h in range(H)]
        gc.append(jnp.concatenate([cols[c][:, l:l + 1] for l in lanes], axis=0))
        beta.append(jnp.concatenate([g[s][:, 8 + l:9 + l] for l in lanes], axis=0))
        g_last.append(jnp.concatenate(
            [jnp.broadcast_to(cols[c][last:last + 1, l:l + 1], (CH, 1)) for l in lanes], axis=0))
        gr = _row_of(gc[c])
        decay.append(jnp.where(incl[d], jnp.exp(jnp.where(incl[d], gc[c] - gr, 0.0)), 0.0))
        lmat = jnp.where(strict[d], beta[c] * kk[s] * decay[c], 0.0)
        l_hi.append(lmat.astype(BF16))
        l_lo.append((lmat - l_hi[c].astype(F32)).astype(BF16))
        egc.append(jnp.exp(gc[c]))
        rhs.append(jnp.concatenate([v_st[s] * beta[c], kf[s] * (beta[c] * egc[c])], axis=1))
        gl_rows = [jnp.broadcast_to(cols[c][last:last + 1, l:l + 1], (1, LANES)) for l in lanes]
        gl_ref[d, s * 8:(s + 1) * 8, :] = jnp.concatenate(gl_rows + [jnp.zeros((8 - H, LANES), F32)], axis=0)
    t_inv = _unit_tri_inverses(l_hi, row, col)
    y = [_dot(t, r.astype(BF16)) for t, r in zip(t_inv, rhs)]
    ys = [_split2(a) for a in y]
    ly = [_dot(l_hi[c], ys[c][0]) + (_dot(l_hi[c], ys[c][1]) + _dot(l_lo[c], ys[c][0])) for c in range(len(chains))]
    resid = [(rhs[c] - (y[c] + ly[c])).astype(BF16) for c in range(len(chains))]
    y = [y[c] + _dot(t_inv[c], resid[c]) for c in range(len(chains))]
    for c, (s, d) in enumerate(chains):
        r0 = s * CH
        attn = qk[s] * decay[c]
        a64 = jnp.zeros((CH, HC), F32)
        for h in range(H):
            a64 = jnp.where(lane_head == h, attn[h * CH:(h + 1) * CH, :], a64)
        attn_ref[d, r0:r0 + CH, :] = a64.astype(BF16)
        qg = (q_st[s].astype(F32) * egc[c]).astype(BF16)
        kg = (kf[s] * jnp.exp(g_last[c] - gc[c])).astype(BF16)
        for h in range(H):
            rs = slice(h * CH, (h + 1) * CH)
            ls = slice(h * DH, (h + 1) * DH)
            u_ref[d, r0:r0 + CH, ls] = y[c][rs, 0:DH]
            w_ref[d, r0:r0 + CH, ls] = y[c][rs, DH:2 * DH].astype(BF16)
            qg_ref[d, r0:r0 + CH, ls] = qg[rs, :]
            kg_ref[d, r0:r0 + CH, ls] = kg[rs, :]


def _gdn_wy(qkv, gates):
    spec = lambda w: pl.BlockSpec((2, TT, w), lambda i: (0, i, 0))
    shp = lambda w, dt: jax.ShapeDtypeStruct((2, N_TOK, w), dt)
    return pl.pallas_call(
        _gdn_wy_kernel,
        grid=(N_TILES,),
        in_specs=[pl.BlockSpec((TT, 3 * HW), lambda i: (i, 0)), pl.BlockSpec((TT, LANES), lambda i: (i, 0))],
        out_specs=[spec(HW), spec(HW), spec(HW), spec(HW), spec(HC),
                   pl.BlockSpec((2, 8 * (TT // CH), LANES), lambda i: (0, i, 0))],
        out_shape=[shp(HW, F32), shp(HW, BF16), shp(HW, BF16), shp(HW, BF16), shp(HC, BF16),
                   jax.ShapeDtypeStruct((2, 8 * N_CHUNKS, LANES), F32)],
        compiler_params=_cparams("parallel"),
    )(qkv, gates)


def _gdn_rec_kernel(bf_ref, bb_ref, seq_ref, first_ref, last_ref, *refs):
    ins, (s0_ref, of_ref, ob_ref, sfin_ref, s_scr) = refs[:12], refs[12:]
    t = pl.program_id(0)

    @pl.when(first_ref[t] == 1)
    def _():
        s_scr[...] = s0_ref[0]

    chains = [(d, h) for d in range(2) for h in range(H)]
    ls = [slice(h * DH, (h + 1) * DH) for h in range(H)]
    s = [s_scr[d, h] for d, h in chains]
    sb = [a.astype(BF16) for a in s]
    ws = [_dot(ins[6 * d + 1][0, :, ls[h]], sb[c]) for c, (d, h) in enumerate(chains)]
    inter = [_dot(ins[6 * d + 2][0, :, ls[h]], sb[c]) for c, (d, h) in enumerate(chains)]
    vb = [(ins[6 * d][0, :, ls[h]] - ws[c]).astype(BF16) for c, (d, h) in enumerate(chains)]
    intra = [_dot(ins[6 * d + 4][0, :, h * CH:(h + 1) * CH], vb[c]) for c, (d, h) in enumerate(chains)]
    upd = [_dot_tn(ins[6 * d + 3][0, :, ls[h]], vb[c]) for c, (d, h) in enumerate(chains)]
    for c, (d, h) in enumerate(chains):
        o_ref = of_ref if d == 0 else ob_ref
        o_ref[:, ls[h]] = inter[c] + intra[c]
        s_scr[d, h] = s[c] * jnp.exp(ins[6 * d + 5][0, h:h + 1, :]) + upd[c]

    @pl.when(last_ref[t] == 1)
    def _():
        sfin_ref[0] = s_scr[...]


def _gdn_rec(tables, u, w, qg, kg, attn, gl, s0):
    def dir_specs(d):
        pick = (lambda t, bf, bb, *_: bf[t]) if d == 0 else (lambda t, bf, bb, *_: bb[t])
        idx = lambda t, *tb: (d, pick(t, *tb), 0)
        return [pl.BlockSpec((1, CH, HW), idx)] * 4 + [pl.BlockSpec((1, CH, HC), idx),
                                                        pl.BlockSpec((1, 8, LANES), idx)]
    st = pl.BlockSpec((1, 2, H, DH, DH), lambda t, bf, bb, sq, *_: (sq[t], 0, 0, 0, 0))
    per_dir = (u, w, qg, kg, attn, gl)
    return pl.pallas_call(
        _gdn_rec_kernel,
        grid_spec=pltpu.PrefetchScalarGridSpec(
            num_scalar_prefetch=5, grid=(N_CHUNKS,),
            in_specs=dir_specs(0) + dir_specs(1) + [st],
            out_specs=[pl.BlockSpec((CH, HW), lambda t, bf, bb, *_: (bf[t], 0)),
                       pl.BlockSpec((CH, HW), lambda t, bf, bb, *_: (bb[t], 0)), st],
            scratch_shapes=[pltpu.VMEM((2, H, DH, DH), F32)]),
        out_shape=[jax.ShapeDtypeStruct((N_TOK, HW), F32), jax.ShapeDtypeStruct((N_TOK, HW), F32),
                   jax.ShapeDtypeStruct((N_SEQ, 2, H, DH, DH), F32)],
        compiler_params=_cparams("arbitrary"),
    )(*tables, *per_dir, *per_dir, s0)


def _mlstm_kernel(bf_ref, bb_ref, seq_ref, first_ref, last_ref,
                  qkvf_ref, gf_ref, qkvb_ref, gb_ref, c0_ref, n0_ref, m0_ref,
                  of_ref, ob_ref, cfin_ref, nfin_ref, mfin_ref, c_scr, n_scr, m_scr):
    t = pl.program_id(0)

    @pl.when(first_ref[t] == 1)
    def _():
        c_scr[...] = c0_ref[0]
        n_scr[...] = n0_ref[0]
        m_scr[...] = m0_ref[0]

    scale = DH ** -0.5
    qkv_refs, g_refs, o_refs = (qkvf_ref, qkvb_ref), (gf_ref, gb_ref), (of_ref, ob_ref)
    chains = [(d, h) for d in range(2) for h in range(H)]
    nch = range(len(chains))
    incl = [_scan_order_mask(d) for d in range(2)]
    g = [g_refs[d][...] for d in range(2)]
    cols = [_dot_exact_lhs(_ones_where(incl[d]), g[d]) for d in range(2)]
    g_t = [jnp.concatenate([g[d], cols[d]], axis=0).T for d in range(2)]
    q = [qkv_refs[d][:, h * DH:(h + 1) * DH] for d, h in chains]
    k = [qkv_refs[d][:, HW + h * DH:HW + (h + 1) * DH] for d, h in chains]
    v = [qkv_refs[d][:, 2 * HW + h * DH:2 * HW + (h + 1) * DH] for d, h in chains]
    qk = [_dot_nt(q[c], k[c]) * scale for c in nch]
    cst = [c_scr[d, h] for d, h in chains]
    qc = [_dot(q[c], cst[c].astype(BF16)) for c in nch]
    s, e_inter, m_row, m, n, bc, b_last, w_max, ig_c, sv = [], [], [], [], [], [], [], [], [], []
    for c, (d, h) in enumerate(chains):
        lf = 24 + d * H + h
        li = 16 + d * H + h
        last = CH - 1 if d == 0 else 0
        bc.append(cols[d][:, lf:lf + 1])
        br = g_t[d][lf:lf + 1, CH:2 * CH]
        ig_c.append(g[d][:, li:li + 1])
        ig_r = g_t[d][li:li + 1, 0:CH]
        dlog = jnp.where(incl[d], bc[c] - br + ig_r, -jnp.inf)
        dmax = jnp.max(dlog, axis=1, keepdims=True)
        b_last.append(bc[c][last:last + 1, :])
        w_max.append(jnp.max(b_last[c] - br + ig_r, axis=1, keepdims=True))
        m.append(m_scr[d, h:h + 1, 0:1])
        n.append(n_scr[d, h:h + 1, :])
        m_inter = bc[c] + m[c]
        m_row.append(jnp.maximum(dmax, m_inter))
        s.append(qk[c] * jnp.exp(dlog - m_row[c]))
        e_inter.append(jnp.exp(m_inter - m_row[c]))
        sv.append(_dot(s[c].astype(BF16), v[c]))
    kw, kv = [], []
    for c, (d, h) in enumerate(chains):
        num = e_inter[c] * qc[c] + sv[c]
        den = (e_inter[c] * jnp.sum(q[c].astype(F32) * n[c], axis=1, keepdims=True)
               + jnp.sum(s[c], axis=1, keepdims=True))
        o_refs[d][:, h * DH:(h + 1) * DH] = num / jnp.maximum(jnp.abs(den), jnp.exp(-m_row[c]))
        m_new = jnp.maximum(b_last[c] + m[c], w_max[c])
        f_c = jnp.exp(b_last[c] + m[c] - m_new)
        kw.append(k[c].astype(F32) * (jnp.exp(b_last[c] - bc[c] + ig_c[c] - m_new) * scale))
        n_scr[d, h:h + 1, :] = f_c * n[c] + jnp.sum(kw[c], axis=0, keepdims=True)
        m_scr[d, h:h + 1, :] = jnp.broadcast_to(m_new, (1, LANES))
        cst[c] = f_c * cst[c]
        kv.append(_dot_tn(kw[c].astype(BF16), v[c]))
    for c, (d, h) in enumerate(chains):
        c_scr[d, h] = cst[c] + kv[c]

    @pl.when(last_ref[t] == 1)
    def _():
        cfin_ref[0] = c_scr[...]
        nfin_ref[0] = n_scr[...]
        mfin_ref[0] = m_scr[...]


def _mlstm_scan(tables, qkv, gates, c0, n0, m0):
    fwd = lambda t, bf, bb, *_: (bf[t], 0)
    bwd = lambda t, bf, bb, *_: (bb[t], 0)
    vec = pl.BlockSpec((1, 2, H, LANES), lambda t, bf, bb, sq, *_: (sq[t], 0, 0, 0))
    mat = pl.BlockSpec((1, 2, H, DH, DH), lambda t, bf, bb, sq, *_: (sq[t], 0, 0, 0, 0))
    return pl.pallas_call(
        _mlstm_kernel,
        grid_spec=pltpu.PrefetchScalarGridSpec(
            num_scalar_prefetch=5, grid=(N_CHUNKS,),
            in_specs=[pl.BlockSpec((CH, 3 * HW), fwd), pl.BlockSpec((CH, LANES), fwd),
                      pl.BlockSpec((CH, 3 * HW), bwd), pl.BlockSpec((CH, LANES), bwd), mat, vec, vec],
            out_specs=[pl.BlockSpec((CH, HW), fwd), pl.BlockSpec((CH, HW), bwd), mat, vec, vec],
            scratch_shapes=[pltpu.VMEM((2, H, DH, DH), F32), pltpu.VMEM((2, H, LANES), F32),
                            pltpu.VMEM((2, H, LANES), F32)]),
        out_shape=[jax.ShapeDtypeStruct((N_TOK, HW), F32), jax.ShapeDtypeStruct((N_TOK, HW), F32),
                   jax.ShapeDtypeStruct((N_SEQ, 2, H, DH, DH), F32),
                   jax.ShapeDtypeStruct((N_SEQ, 2, H, LANES), F32),
                   jax.ShapeDtypeStruct((N_SEQ, 2, H, LANES), F32)],
        compiler_params=_cparams("arbitrary"),
    )(*tables, qkv, gates, qkv, gates, c0, n0, m0)


def _post_kernel(tbl_ref, x_ref, mod_ref, yfc_ref, yfl_ref, of_ref, ob_ref, hf_ref, hb_ref,
                 gz_ref, mo_ref, gates_ref, gn_ref, mn_ref, wbf_ref, wbg_ref, wbm_ref, wo_ref,
                 n2_ref, rwh_ref, rwl_ref, rb_ref,
                 xo_ref, h2_ref, info_ref, infot_ref, stats_ref, cnt_scr):
    i = pl.program_id(0)

    @pl.when(i == 0)
    def _():
        cnt_scr[...] = jnp.zeros_like(cnt_scr)

    yf = jnp.where(i < CTX_TILES, yfc_ref[...], yfl_ref[...])
    og = of_ref[...] + ob_ref[...]
    hm = hf_ref[...] + hb_ref[...]
    og_parts, hm_parts = [], []
    for h in range(H):
        sl = slice(h * DH, (h + 1) * DH)
        og_parts.append(_rms(og[:, sl]) * gn_ref[...] * _silu(gz_ref[:, sl].astype(F32)))
        hm_parts.append(_rms(hm[:, sl]) * mn_ref[:, sl] * _sigmoid(mo_ref[:, sl].astype(F32)))
    ogb = jnp.concatenate(og_parts, axis=1).astype(BF16)
    hmb = jnp.concatenate(hm_parts, axis=1).astype(BF16)
    y = (_sigmoid(gates_ref[:, 0:D].astype(F32)) * _dot(yf, wbf_ref[...])
         + _sigmoid(gates_ref[:, D:2 * D].astype(F32)) * _dot(ogb, wbg_ref[...])
         + _sigmoid(gates_ref[:, 2 * D:3 * D].astype(F32)) * _dot(hmb, wbm_ref[...]))
    x = x_ref[...] + mod_ref[0, 2:3, :] * _dot(y.astype(BF16), wo_ref[...])
    xo_ref[...] = x
    h2 = _rms(x) * n2_ref[...]
    h2 = h2 * (1.0 + mod_ref[0, 4:5, :]) + mod_ref[0, 3:4, :]
    hb2, hl2 = _split2(h2)
    h2_ref[...] = hb2
    logits = (_dot(hb2, rwh_ref[...]) + (_dot(hb2, rwl_ref[...]) + _dot(hl2, rwh_ref[...]))
              + rb_ref[...])
    lane = lax.broadcasted_iota(jnp.int32, logits.shape, 1).astype(F32)
    work = jnp.where(lane < N_EXP, logits, -jnp.inf)
    sel = jnp.zeros(logits.shape, F32)
    hits, exps = [], []
    top0 = None
    denom = None
    for kk in range(TOP_K):
        mx = jnp.max(work, axis=1, keepdims=True)
        first = jnp.min(jnp.where(work == mx, lane, float(LANES)), axis=1, keepdims=True)
        hit = lane == first
        if kk == 0:
            top0 = mx
        e = jnp.exp(mx - top0)
        denom = e if kk == 0 else denom + e
        hits.append(hit)
        exps.append(e)
        sel = jnp.where(hit, 1.0, sel)
        work = jnp.where(hit, -jnp.inf, work)
    r_i = lax.broadcasted_iota(jnp.int32, (TT, TT), 0)
    c_i = lax.broadcasted_iota(jnp.int32, (TT, TT), 1)
    within = _dot(_ones_where(c_i < r_i), sel.astype(BF16))
    tile_cnt = jnp.sum(sel, axis=0, keepdims=True)
    a_i = lax.broadcasted_iota(jnp.int32, (LANES, LANES), 0)
    b_i = lax.broadcasted_iota(jnp.int32, (LANES, LANES), 1)
    tile_off = _dot(jnp.broadcast_to(tile_cnt, (8, LANES)).astype(BF16), _ones_where(a_i < b_i))[0:1, :]
    local = tile_off + within
    info = jnp.zeros(logits.shape, F32)
    for kk in range(TOP_K):
        pos = jnp.sum(jnp.where(hits[kk], local, 0.0), axis=1, keepdims=True)
        info = jnp.where(lane == float(kk), pos, info)
        info = jnp.where(lane == float(TOP_K + kk), exps[kk] / denom, info)
    info_ref[...] = info
    infot_ref[...] = info.T[0:8, :]
    stats_ref[0] = jnp.concatenate([cnt_scr[0:1, :], tile_cnt, tile_off, jnp.zeros((5, LANES), F32)], axis=0)
    cnt_scr[0:1, :] = cnt_scr[0:1, :] + tile_cnt


def _post(cond_tbl, x, mod, yfc, yfl, o_f, o_b, h_f, h_b, gz, mo, gates, gn, mn, wbf, wbg, wbm, wo,
          n2, rwh, rwl, rb):
    tile = lambda w: pl.BlockSpec((TT, w), lambda i, t: (i, 0))
    const = lambda shape: pl.BlockSpec(shape, lambda i, t: (0,) * len(shape))
    in_specs = [tile(D), pl.BlockSpec((1, N_MOD, D), lambda i, t: (t[i], 0, 0)),
                pl.BlockSpec((TT, HW), lambda i, t: (jnp.minimum(i, CTX_TILES - 1), 0)),
                pl.BlockSpec((TT, HW), lambda i, t: (jnp.maximum(i - CTX_TILES, 0), 0)),
                tile(HW), tile(HW), tile(HW), tile(HW),
                tile(HW), tile(HW), tile(3 * D),
                const((1, DH)), const((1, HW)), const((HW, D)), const((HW, D)), const((HW, D)),
                const((D, D)), const((1, D)), const((D, LANES)), const((D, LANES)), const((1, LANES))]
    return pl.pallas_call(
        _post_kernel,
        grid_spec=pltpu.PrefetchScalarGridSpec(
            num_scalar_prefetch=1, grid=(N_TILES,), in_specs=in_specs,
            out_specs=[tile(D), tile(D), tile(LANES), pl.BlockSpec((8, TT), lambda i, t: (0, i)),
                       pl.BlockSpec((1, 8, LANES), lambda i, t: (i, 0, 0))],
            scratch_shapes=[pltpu.VMEM((8, LANES), F32)]),
        out_shape=[jax.ShapeDtypeStruct((N_TOK, D), F32), jax.ShapeDtypeStruct((N_TOK, D), BF16),
                   jax.ShapeDtypeStruct((N_TOK, LANES), F32), jax.ShapeDtypeStruct((8, N_TOK), F32),
                   jax.ShapeDtypeStruct((N_TILES, 8, LANES), F32)],
        compiler_params=_cparams("arbitrary"),
    )(cond_tbl, x, mod, yfc, yfl, o_f, o_b, h_f, h_b, gz, mo, gates, gn, mn, wbf, wbg, wbm, wo,
      n2, rwh, rwl, rb)


PAIRS = TT * TOP_K
RUN_BITS = tuple(1 << b for b in range(8, -1, -1))


def _slot_rows(start_slot, n_slots):
    return pl.ds(pl.multiple_of(start_slot * 8, 8), n_slots * 8)


def _for_each_run(base, cnt_ref, fn):
    def body(e, carry):
        cnt = cnt_ref[base + e]
        for bit in RUN_BITS:
            @pl.when((cnt & bit) != 0)
            def _():
                fn(base + e, cnt & (-2 * bit), bit)
        return carry
    lax.fori_loop(0, N_EXP, body, 0)


def _zero_fill(ps_ref, pc_ref, tail_ref, xs_hbm, zbuf, zsem, start):
    def piece(dst_slot, n):
        cp = pltpu.make_async_copy(zbuf.at[pl.ds(0, n * 8)], xs_hbm.at[_slot_rows(dst_slot, n)], zsem)
        if start:
            cp.start()
        else:
            cp.wait()

    def pad(e, carry):
        cnt = pc_ref[e]
        for bit in RUN_BITS[1:]:
            @pl.when((cnt & bit) != 0)
            def _():
                piece(ps_ref[e] + (cnt & (-2 * bit)), bit)
        return carry

    def tail(t, carry):
        piece(tail_ref[0] + t * TM, TM)
        return carry

    lax.fori_loop(0, N_EXP, pad, 0)
    lax.fori_loop(0, tail_ref[1], tail, 0)


def _dispatch_kernel(ss_ref, cnt_ref, toff_ref, ps_ref, pc_ref, tail_ref, h2_ref, infot_ref, xs_hbm,
                     stage, zbuf, sem, zsem):
    i = pl.program_id(0)
    buf = i % 2
    whole = lambda b: pltpu.make_async_copy(stage.at[b], xs_hbm.at[pl.ds(0, PAIRS * 8)], sem.at[b])

    @pl.when(i >= 2)
    def _():
        whole(buf).wait()

    r = lax.broadcasted_iota(jnp.int32, (PAIRS, TT), 0).astype(F32)
    pick = r == infot_ref[0:1, :]
    for kk in range(1, TOP_K):
        pick = jnp.logical_or(pick, r == infot_ref[kk:kk + 1, :])
    rows = _dot(_ones_where(pick), h2_ref[...])
    for cc in range(8):
        stage[buf, pl.ds(cc, PAIRS, stride=8), :] = rows[:, cc * LANES:(cc + 1) * LANES]

    def send(j, o, n):
        pltpu.make_async_copy(stage.at[buf, _slot_rows(toff_ref[j] + o, n)],
                              xs_hbm.at[_slot_rows(ss_ref[j] + o, n)], sem.at[buf]).start()

    _for_each_run(i * N_EXP, cnt_ref, send)

    @pl.when(i == 0)
    def _():
        zbuf[...] = jnp.zeros_like(zbuf)
        _zero_fill(ps_ref, pc_ref, tail_ref, xs_hbm, zbuf, zsem, True)

    @pl.when(i == N_TILES - 1)
    def _():
        whole(1 - buf).wait()
        whole(buf).wait()
        _zero_fill(ps_ref, pc_ref, tail_ref, xs_hbm, zbuf, zsem, False)


def _dispatch(slot_start, seg_cnt, seg_off, pad_start, pad_cnt, tail, h2, info_t):
    return pl.pallas_call(
        _dispatch_kernel,
        grid_spec=pltpu.PrefetchScalarGridSpec(
            num_scalar_prefetch=6, grid=(N_TILES,),
            in_specs=[pl.BlockSpec((TT, D), lambda i, *_: (i, 0)), pl.BlockSpec((8, TT), lambda i, *_: (0, i))],
            out_specs=pl.BlockSpec(memory_space=pl.ANY),
            scratch_shapes=[pltpu.VMEM((2, PAIRS * 8, LANES), F32), pltpu.VMEM((TM * 8, LANES), F32),
                            pltpu.SemaphoreType.DMA((2,)), pltpu.SemaphoreType.DMA(())]),
        out_shape=jax.ShapeDtypeStruct((S_MAX * 8, LANES), F32),
        compiler_params=_cparams("arbitrary"),
    )(slot_start, seg_cnt, seg_off, pad_start, pad_cnt, tail, h2, info_t)


def _gmm_kernel(te_ref, nu_ref, valid_ref, x_ref, wg_ref, bg_ref, wu_ref, bu_ref, wd_ref, bd_ref, o_ref,
                wg_scr, wu_scr, wd_scr):
    i = pl.program_id(0)
    used = i < nu_ref[0]

    @pl.when(used)
    def _():
        e = te_ref[i]
        e_prev = te_ref[jnp.maximum(i - 1, 0)]

        @pl.when(jnp.logical_or(i == 0, e != e_prev))
        def _():
            wg_scr[...] = wg_ref[0, 0].astype(BF16)
            wu_scr[...] = wu_ref[0, 0].astype(BF16)
            wd_scr[...] = wd_ref[0, 0].astype(BF16)

        x = jnp.concatenate([x_ref[pl.ds(cc, TM, stride=8), :] for cc in range(8)], axis=1)
        row = lax.broadcasted_iota(jnp.int32, (TM, 1), 0)
        x = jnp.where(row < valid_ref[i], x, 0.0).astype(BF16)
        n_ch = D // FF_CHUNK
        cs = [slice(j * FF_CHUNK, (j + 1) * FF_CHUNK) for j in range(n_ch)]
        gate_up = lambda j: (_dot(x, wg_scr[:, cs[j]]), _dot(x, wu_scr[:, cs[j]]))
        nxt = gate_up(0)
        y = None
        for j in range(n_ch):
            g_raw, u_raw = nxt
            if j + 1 < n_ch:
                nxt = gate_up(j + 1)
            gate = jnp.minimum(g_raw + bg_ref[0, 0, :, cs[j]], SWIGLU_LIMIT)
            up = jnp.clip(u_raw + bu_ref[0, 0, :, cs[j]], -SWIGLU_LIMIT, SWIGLU_LIMIT)
            act = ((up + 1.0) * gate * _sigmoid(SWIGLU_ALPHA * gate)).astype(BF16)
            part = _dot(act, wd_scr[cs[j], :])
            y = part if y is None else y + part
        y = y + bd_ref[0, 0]
        for cc in range(8):
            o_ref[pl.ds(cc, TM, stride=8), :] = y[:, cc * LANES:(cc + 1) * LANES]

    @pl.when(jnp.logical_not(used))
    def _():
        o_ref[...] = jnp.zeros_like(o_ref)


def _gmm(layer, tile_exp, n_used, valid, xs, wg, bg, wu, bu, wd, bd):
    wspec = pl.BlockSpec((1, 1, D, D), lambda i, te, nu, va: (layer, te[i], 0, 0))
    bspec = pl.BlockSpec((1, 1, 1, D), lambda i, te, nu, va: (layer, te[i], 0, 0))
    return pl.pallas_call(
        _gmm_kernel,
        grid_spec=pltpu.PrefetchScalarGridSpec(
            num_scalar_prefetch=3, grid=(NT_MAX,),
            in_specs=[pl.BlockSpec((TM * 8, LANES), lambda i, te, nu, va: (jnp.minimum(i, nu[0] - 1), 0)),
                      wspec, bspec, wspec, bspec, wspec, bspec],
            out_specs=pl.BlockSpec((TM * 8, LANES), lambda i, te, nu, va: (i, 0)),
            scratch_shapes=[pltpu.VMEM((D, D), BF16)] * 3),
        out_shape=jax.ShapeDtypeStruct((S_MAX * 8, LANES), F32),
        compiler_params=_cparams("arbitrary"),
    )(tile_exp, n_used, valid, xs, wg, bg.reshape(DEPTH, N_EXP, 1, D), wu, bu.reshape(DEPTH, N_EXP, 1, D),
      wd, bd.reshape(DEPTH, N_EXP, 1, D))


def _combine_kernel(tbl_ref, ss_ref, cnt_ref, toff_ref, y_hbm, info_ref, x_ref, mod_ref, g_ref, *refs,
                    final):
    outs, (ybuf, sem) = refs[:-2], refs[-2:]
    i = pl.program_id(0)
    buf = i % 2

    def fetch(tile, b):
        def recv(j, o, n):
            pltpu.make_async_copy(y_hbm.at[_slot_rows(ss_ref[j] + o, n)],
                                  ybuf.at[b, _slot_rows(toff_ref[j] + o, n)], sem.at[b]).start()
        _for_each_run(tile * N_EXP, cnt_ref, recv)

    @pl.when(i == 0)
    def _():
        fetch(0, 0)

    @pl.when(i + 1 < N_TILES)
    def _():
        fetch(i + 1, 1 - buf)

    r = lax.broadcasted_iota(jnp.int32, (TT, PAIRS), 1).astype(F32)
    wm = jnp.zeros((TT, PAIRS), F32)
    for kk in range(TOP_K):
        wm = jnp.where(r == info_ref[:, kk:kk + 1], info_ref[:, TOP_K + kk:TOP_K + kk + 1], wm)
    pltpu.make_async_copy(y_hbm.at[pl.ds(0, PAIRS * 8)], ybuf.at[buf], sem.at[buf]).wait()
    y = jnp.concatenate([ybuf[buf, pl.ds(cc, PAIRS, stride=8), :] for cc in range(8)], axis=1)
    x = x_ref[...] + mod_ref[0, 5:6, :] * _dot(wm.astype(BF16), y.astype(BF16))
    if not final:
        outs[0][...] = x
    else:
        res = _rms(x) * g_ref[...]

        @pl.when(i < CTX_TILES)
        def _():
            outs[0][...] = res

        @pl.when(i >= CTX_TILES)
        def _():
            outs[1][...] = res


def _combine(final, cond_tbl, slot_start, seg_cnt, seg_off, y_sorted, info, x, mod, g):
    tile = lambda w: pl.BlockSpec((TT, w), lambda i, *_: (i, 0))
    if final:
        out_specs = [pl.BlockSpec((TT, D), lambda i, *_: (jnp.minimum(i, CTX_TILES - 1), 0)),
                     pl.BlockSpec((TT, D), lambda i, *_: (jnp.maximum(i - CTX_TILES, 0), 0))]
        out_shape = [jax.ShapeDtypeStruct((N_CTX, D), F32), jax.ShapeDtypeStruct((N_LAT, D), F32)]
    else:
        out_specs = [tile(D)]
        out_shape = [jax.ShapeDtypeStruct((N_TOK, D), F32)]
    return pl.pallas_call(
        functools.partial(_combine_kernel, final=final),
        grid_spec=pltpu.PrefetchScalarGridSpec(
            num_scalar_prefetch=4, grid=(N_TILES,),
            in_specs=[pl.BlockSpec(memory_space=pl.ANY), tile(LANES), tile(D),
                      pl.BlockSpec((1, N_MOD, D), lambda i, t, *_: (t[i], 0, 0)),
                      pl.BlockSpec((1, D), lambda i, *_: (0, 0))],
            out_specs=out_specs,
            scratch_shapes=[pltpu.VMEM((2, PAIRS * 8, LANES), F32), pltpu.SemaphoreType.DMA((2,))]),
        out_shape=out_shape,
        compiler_params=_cparams("arbitrary"),
    )(cond_tbl, slot_start, seg_cnt, seg_off, y_sorted, info, x, mod, g)


def _dft_consts(t_len):
    k = np.arange(t_len, dtype=np.int64)
    ang = 2.0 * np.pi * ((k[:, None] * k[None, :]) % t_len).astype(np.float64) / t_len
    return np.concatenate([np.cos(ang), -np.sin(ang)], axis=1).astype(np.float32)


def _pos_table():
    quarter = D // 4
    omega = 1.0 / (10000.0 ** (np.arange(quarter, dtype=np.float32) / np.float32(quarter)))
    omega = omega.astype(np.float32).astype(np.float64)
    t = np.arange(T_LAT)
    ang_r = (t // GRID_W).astype(np.float64)[:, None] * omega
    ang_c = (t % GRID_W).astype(np.float64)[:, None] * omega
    return np.concatenate([np.sin(ang_r), np.cos(ang_r), np.sin(ang_c), np.cos(ang_c)],
                          axis=-1).astype(np.float32)


def _repack_w_in(w):
    f, gq, gk, gv, gz, ga, gb, mq, mk, mv, mo, mi, mf, g1, g2, g3 = jnp.split(
        w, np.cumsum([512, 512, 512, 512, 512, 8, 8, 512, 512, 512, 512, 8, 8, 1024, 1024])[:].tolist(),
        axis=1)
    main = jnp.concatenate([f, gq, gk, gv, gz, mq, mk, mv, mo, g1, g2, g3], axis=1).astype(BF16)
    small = jnp.concatenate([ga, gb, mi, mf, jnp.zeros((D, LANES - 32), F32)], axis=1)
    hi = small.astype(BF16)
    lo = (small - hi.astype(F32)).astype(BF16)
    return main, hi, lo


def _route_glue(stats):
    seg_base = stats[:, 0, :N_EXP].astype(jnp.int32)
    seg_cnt = stats[:, 1, :N_EXP].astype(jnp.int32)
    seg_off = stats[:, 2, :N_EXP].astype(jnp.int32)
    counts = seg_base[-1] + seg_cnt[-1]
    region = ((counts + TM - 1) // TM) * TM
    ends = jnp.cumsum(region)
    off = ends - region
    n_used = ends[-1] // TM
    tile_start = jnp.arange(NT_MAX, dtype=jnp.int32) * TM
    tile_exp = jnp.minimum(jnp.sum((tile_start[:, None] >= ends[None, :]).astype(jnp.int32), axis=1), N_EXP - 1)
    last_exp = jnp.take(tile_exp, jnp.maximum(n_used - 1, 0))
    tile_exp = jnp.where(tile_start < ends[-1], tile_exp, last_exp)
    valid = jnp.clip(jnp.take(off + counts, tile_exp) - tile_start, 0, TM)
    slot_start = off[None, :] + seg_base
    tail = jnp.stack([ends[-1], (S_MAX - ends[-1]) // TM]).astype(jnp.int32)
    return (tile_exp.astype(jnp.int32), n_used.reshape(1).astype(jnp.int32), valid.astype(jnp.int32),
            slot_start.reshape(-1), seg_cnt.reshape(-1), seg_off.reshape(-1),
            (off + counts).astype(jnp.int32), (region - counts).astype(jnp.int32), tail)


def kernel(x_prompt, x_sample, state_gdn, state_mlstm_c, state_mlstm_n, state_mlstm_m, c, c_ctx,
           w_ada, b_ada, norm1_g, norm2_g, w_in, gdn_conv_w, gdn_a_log, gdn_dt_bias, gdn_norm_g,
           mlstm_i_bias, mlstm_f_bias, mlstm_norm_g, w_branch_f, w_branch_g, w_branch_m, w_out,
           router_w, router_b, exp_w_gate, exp_b_gate, exp_w_up, exp_b_up, exp_w_down, exp_b_down,
           final_norm_g):
    cond_tbl = jnp.asarray(np.concatenate([np.zeros(CTX_TILES, np.int32),
                                           1 + np.arange(N_TILES - CTX_TILES, dtype=np.int32) // LAT_TILES_PER_SEQ]))
    tables = tuple(jnp.asarray(t) for t in _scan_tables())
    cond8 = jnp.concatenate([c_ctx[None, :], c, jnp.zeros((8 - 1 - B_LAT, D), F32)], axis=0)
    mod_all = _ada(cond8, w_ada, b_ada).reshape(DEPTH, 8, N_MOD, D)

    pos = jnp.asarray(_pos_table())
    dft_c = jnp.asarray(_dft_consts(T_CTX)).astype(BF16)
    dft_l = jnp.asarray(_dft_consts(T_LAT)).astype(BF16)
    ang = 2.0 * np.pi * ((np.arange(DH)[:, None] * np.arange(DH)[None, :]) % DH) / DH
    c128 = jnp.asarray(np.cos(ang).astype(np.float32)).astype(BF16)
    s128 = jnp.asarray(np.sin(ang).astype(np.float32)).astype(BF16)

    zeros_s = jnp.zeros((B_CTX, 2, H, DH, DH), F32)
    zeros_v = jnp.zeros((B_CTX, 2, H, LANES), F32)
    x = None
    gdn_states, c_states, n_states, m_states = [], [], [], []
    y_prompt = y_sample = None
    for l in range(DEPTH):
        mod = mod_all[l]
        wm, wsh, wsl = _repack_w_in(w_in[l])
        if l == 0:
            outs = _pre(True, (x_prompt.reshape(N_CTX, D), x_sample.reshape(N_LAT, D), pos), mod,
                        norm1_g[l][None, :], wm, wsh, wsl, cond_tbl)
            x, outs = outs[0], outs[1:]
        else:
            outs = _pre(False, x, mod, norm1_g[l][None, :], wm, wsh, wsl, cond_tbl)
        f_all, gqkv, gz, mqkv, mo, gates, sm = outs

        yfc = _fourier(f_all, dft_c, c128, s128, B_CTX, T_CTX, 0)
        yfl = _fourier(f_all, dft_l, c128, s128, B_LAT, T_LAT, N_CTX)

        prm = jnp.zeros((8, LANES), F32)
        prm = prm.at[0, 0:8].set(gdn_a_log[l].reshape(-1))
        prm = prm.at[1, 0:8].set(gdn_dt_bias[l].reshape(-1))
        prm = prm.at[1, 16:24].set(mlstm_i_bias[l].reshape(-1))
        prm = prm.at[1, 24:32].set(mlstm_f_bias[l].reshape(-1))
        qkv_c, gts = _prep2(gqkv, sm, gdn_conv_w[l], prm)

        s0 = jnp.concatenate([zeros_s, state_gdn[:, l]], axis=0)
        o_f, o_b, s_fin = _gdn_rec(tables, *_gdn_wy(qkv_c, gts), s0)
        c0 = jnp.concatenate([zeros_s, state_mlstm_c[:, l]], axis=0)
        n0 = jnp.concatenate([zeros_v, state_mlstm_n[:, l]], axis=0)
        m0 = jnp.concatenate([zeros_v, jnp.broadcast_to(state_mlstm_m[:, l][..., None],
                                                         (B_LAT, 2, H, LANES))], axis=0)
        h_f, h_b, c_fin, n_fin, m_fin = _mlstm_scan(tables, mqkv, gts, c0, n0, m0)
        gdn_states.append(s_fin[:B_CTX])
        c_states.append(c_fin[:B_CTX])
        n_states.append(n_fin[:B_CTX])
        m_states.append(m_fin[:B_CTX, :, :, 0])

        rw = jnp.concatenate([router_w[l], jnp.zeros((D, LANES - N_EXP), F32)], axis=1)
        rwh = rw.astype(BF16)
        rwl = (rw - rwh.astype(F32)).astype(BF16)
        rb = jnp.concatenate([router_b[l], jnp.zeros((LANES - N_EXP,), F32)])[None, :]
        x, h2, info, info_t, stats = _post(
            cond_tbl, x, mod, yfc, yfl, o_f, o_b, h_f, h_b, gz, mo, gates,
            gdn_norm_g[l][None, :], mlstm_norm_g[l][None, :],
            w_branch_f[l].astype(BF16), w_branch_g[l].astype(BF16), w_branch_m[l].astype(BF16),
            w_out[l].astype(BF16), norm2_g[l][None, :], rwh, rwl, rb)

        tile_exp, n_used, valid, slot_start, seg_cnt, seg_off, pad_start, pad_cnt, tail = _route_glue(stats)
        xs_sorted = _dispatch(slot_start, seg_cnt, seg_off, pad_start, pad_cnt, tail, h2, info_t)
        y_sorted = _gmm(l, tile_exp, n_used, valid, xs_sorted, exp_w_gate, exp_b_gate, exp_w_up,
                        exp_b_up, exp_w_down, exp_b_down)
        res = _combine(l + 1 == DEPTH, cond_tbl, slot_start, seg_cnt, seg_off, y_sorted, info, x, mod,
                       final_norm_g[None, :])
        if l + 1 < DEPTH:
            x = res[0]
        else:
            y_prompt, y_sample = res

    return (y_prompt.reshape(B_CTX, T_CTX, D), y_sample.reshape(B_LAT, T_LAT, D),
            jnp.stack(gdn_states, axis=1), jnp.stack(c_states, axis=1),
            jnp.stack(n_states, axis=1), jnp.stack(m_states, axis=1))
```

```python
import functools
import math

import numpy as np
import jax
import jax.numpy as jnp
from jax import lax
from jax.experimental import pallas as pl
from jax.experimental.pallas import tpu as pltpu

F32 = jnp.float32
BF16 = jnp.bfloat16

D = 1024
DEPTH = 2
B_CTX, T_CTX = 16, 256
B_LAT, T_LAT = 4, 2048
N_CTX = B_CTX * T_CTX
N_LAT = B_LAT * T_LAT
N_TOK = N_CTX + N_LAT
N_SEQ = B_CTX + B_LAT
GRID_W = 64
H = 4
DH = 128
HW = H * DH
CH = 64
TT = 256
N_TILES = N_TOK // TT
CTX_TILES = N_CTX // TT
LAT_TILES_PER_SEQ = T_LAT // TT
N_CHUNKS = N_TOK // CH
N_EXP = 32
TOP_K = 4
N_MOD = 6
EPS = 1e-6
SWIGLU_ALPHA = 1.702
SWIGLU_LIMIT = 7.0
LANES = 128
TM = 256
FF_CHUNK = 256
S_MAX = N_TOK * TOP_K + N_EXP * (TM - 1)
S_MAX = ((S_MAX + TM - 1) // TM) * TM
NT_MAX = S_MAX // TM

W_MAIN = 512 + 1536 + 512 + 1536 + 512 + 3072


def _dot(a, b):
    return jnp.dot(a, b, preferred_element_type=F32)


def _dot_nt(a, b):
    return lax.dot_general(a, b, (((1,), (1,)), ((), ())), preferred_element_type=F32)


def _dot_tn(a, b):
    return lax.dot_general(a, b, (((0,), (0,)), ((), ())), preferred_element_type=F32)


def _split2(a):
    hi = a.astype(BF16)
    lo = (a - hi.astype(F32)).astype(BF16)
    return hi, lo


def _split3(a):
    hi = a.astype(BF16)
    r = a - hi.astype(F32)
    mid = r.astype(BF16)
    lo = (r - mid.astype(F32)).astype(BF16)
    return hi, mid, lo


def _dot3(a, b):
    ah, al = _split2(a)
    bh, bl = _split2(b)
    return _dot(ah, bh) + (_dot(ah, bl) + _dot(al, bh))


def _dot_exact_lhs(a_bf16, b):
    bh, bm, bl = _split3(b)
    return _dot(a_bf16, bh) + (_dot(a_bf16, bm) + _dot(a_bf16, bl))


def _sigmoid(x):
    return 0.5 * jnp.tanh(0.5 * x) + 0.5


def _silu(x):
    return x * _sigmoid(x)


def _softplus(x):
    return jnp.maximum(x, 0.0) + jnp.log(1.0 + jnp.exp(-jnp.abs(x)))


def _rms(x):
    return x * lax.rsqrt(jnp.mean(x * x, axis=-1, keepdims=True) + EPS)


def _cparams(*sem):
    return pltpu.CompilerParams(dimension_semantics=tuple(sem))


def _ada_kernel(c_ref, w_ref, b_ref, o_ref):
    o_ref[0] = _dot3(_silu(c_ref[...]), w_ref[0]) + b_ref[0]


def _ada(cond8, w_ada, b_ada):
    nb = 1536
    return pl.pallas_call(
        _ada_kernel,
        grid=(DEPTH, N_MOD * D // nb),
        in_specs=[pl.BlockSpec((8, D), lambda l, j: (0, 0)),
                  pl.BlockSpec((1, D, nb), lambda l, j: (l, 0, j)),
                  pl.BlockSpec((1, 1, nb), lambda l, j: (l, 0, j))],
        out_specs=pl.BlockSpec((1, 8, nb), lambda l, j: (l, 0, j)),
        out_shape=jax.ShapeDtypeStruct((DEPTH, 8, N_MOD * D), F32),
        compiler_params=_cparams("parallel", "parallel"),
    )(cond8, w_ada, b_ada.reshape(DEPTH, 1, N_MOD * D))


def _pre_body(x, mod_ref, n1_ref, wm_ref, wsh_ref, wsl_ref, outs):
    f_ref, gqkv_ref, gz_ref, mqkv_ref, mo_ref, gates_ref, sm_ref = outs
    h = _rms(x) * n1_ref[...]
    h = h * (1.0 + mod_ref[0, 1:2, :]) + mod_ref[0, 0:1, :]
    hb, hl = _split2(h)
    off = 0
    for ref, width in ((f_ref, 512), (gqkv_ref, 1536), (gz_ref, 512), (mqkv_ref, 1536),
                       (mo_ref, 512), (gates_ref, 3072)):
        ref[...] = _dot(hb, wm_ref[:, off:off + width]).astype(BF16)
        off += width
    sm_ref[...] = _dot(hb, wsh_ref[...]) + (_dot(hb, wsl_ref[...]) + _dot(hl, wsh_ref[...]))


def _pre_first_kernel(tbl_ref, xp_ref, xs_ref, pos_ref, mod_ref, n1_ref, wm_ref, wsh_ref, wsl_ref,
                      x_out_ref, *outs):
    i = pl.program_id(0)
    x = jnp.where(i < CTX_TILES, xp_ref[...], xs_ref[...] + pos_ref[...])
    x_out_ref[...] = x
    _pre_body(x, mod_ref, n1_ref, wm_ref, wsh_ref, wsl_ref, outs)


def _pre_next_kernel(tbl_ref, x_ref, mod_ref, n1_ref, wm_ref, wsh_ref, wsl_ref, *outs):
    _pre_body(x_ref[...], mod_ref, n1_ref, wm_ref, wsh_ref, wsl_ref, outs)


_PRE_OUT_WIDTHS = ((512, BF16), (1536, BF16), (512, BF16), (1536, BF16), (512, BF16), (3072, BF16),
                   (LANES, F32))


def _pre(first, xs_in, mod, n1, wm, wsh, wsl, cond_tbl):
    tile = lambda w: pl.BlockSpec((TT, w), lambda i, t: (i, 0))
    const = lambda shape: pl.BlockSpec(shape, lambda i, t: (0,) * len(shape))
    w_specs = [pl.BlockSpec((1, N_MOD, D), lambda i, t: (t[i], 0, 0)), const((1, D)),
               const((D, W_MAIN)), const((D, LANES)), const((D, LANES))]
    out_specs = [tile(w) for w, _ in _PRE_OUT_WIDTHS]
    out_shape = [jax.ShapeDtypeStruct((N_TOK, w), dt) for w, dt in _PRE_OUT_WIDTHS]
    if first:
        x_prompt2, x_sample2, pos = xs_in
        in_specs = [pl.BlockSpec((TT, D), lambda i, t: (jnp.minimum(i, CTX_TILES - 1), 0)),
                    pl.BlockSpec((TT, D), lambda i, t: (jnp.maximum(i - CTX_TILES, 0), 0)),
                    pl.BlockSpec((TT, D), lambda i, t: (jnp.maximum(i - CTX_TILES, 0) % LAT_TILES_PER_SEQ, 0))]
        kern = _pre_first_kernel
        out_specs = [tile(D)] + out_specs
        out_shape = [jax.ShapeDtypeStruct((N_TOK, D), F32)] + out_shape
        args = (x_prompt2, x_sample2, pos)
    else:
        in_specs = [tile(D)]
        kern = _pre_next_kernel
        args = (xs_in,)
    return pl.pallas_call(
        kern,
        grid_spec=pltpu.PrefetchScalarGridSpec(
            num_scalar_prefetch=1, grid=(N_TILES,), in_specs=in_specs + w_specs, out_specs=out_specs),
        out_shape=out_shape,
        compiler_params=_cparams("parallel"),
    )(cond_tbl, *args, mod, n1, wm, wsh, wsl)


def _fourier_kernel(x_ref, dft_ref, c_ref, s_ref, o_ref, z_scr, *, t_len, scale):
    @pl.when(pl.program_id(1) == 0)
    def _():
        for g in range(H):
            xg = x_ref[:, g * DH:(g + 1) * DH]
            z_scr[0:t_len, g * DH:(g + 1) * DH] = _dot(xg, c_ref[...]).astype(BF16)
            z_scr[t_len:2 * t_len, g * DH:(g + 1) * DH] = _dot(xg, s_ref[...]).astype(BF16)

    o_ref[...] = (_dot(dft_ref[...], z_scr[...]) * scale).astype(BF16)


def _fourier(f_all, dft, c128, s128, n_seq, t_len, row0):
    tr = min(t_len, 512)
    blk0 = row0 // t_len
    return pl.pallas_call(
        functools.partial(_fourier_kernel, t_len=t_len, scale=1.0 / math.sqrt(t_len * DH)),
        grid=(n_seq, t_len // tr),
        in_specs=[pl.BlockSpec((t_len, HW), lambda b, r: (blk0 + b, 0)),
                  pl.BlockSpec((tr, 2 * t_len), lambda b, r: (r, 0)),
                  pl.BlockSpec((DH, DH), lambda b, r: (0, 0)),
                  pl.BlockSpec((DH, DH), lambda b, r: (0, 0))],
        out_specs=pl.BlockSpec((tr, HW), lambda b, r: (b * (t_len // tr) + r, 0)),
        out_shape=jax.ShapeDtypeStruct((n_seq * t_len, HW), BF16),
        scratch_shapes=[pltpu.VMEM((2 * t_len, HW), BF16)],
        compiler_params=_cparams("parallel", "arbitrary"),
    )(f_all, dft, c128, s128)


HALO = 16


def _prep2_kernel(cur_ref, prev_ref, next_ref, sm_ref, cw_ref, prm_ref, qkv_ref, g_ref):
    i = pl.program_id(0)
    j = jnp.maximum(i - CTX_TILES, 0) % LAT_TILES_PER_SEQ
    is_lat = i >= CTX_TILES
    has_prev = jnp.logical_and(is_lat, j > 0)
    has_next = jnp.logical_and(is_lat, j < LAT_TILES_PER_SEQ - 1)
    x = cur_ref[...].astype(F32)
    row = lax.broadcasted_iota(jnp.int32, x.shape, 0)
    prev_row = jnp.where(has_prev, prev_ref[HALO - 1:HALO, :].astype(F32), 0.0)
    next_row = jnp.where(has_next, next_ref[0:1, :].astype(F32), 0.0)
    xp = jnp.where(row == 0, prev_row, pltpu.roll(x, 1, 0))
    xn = jnp.where(row == TT - 1, next_row, pltpu.roll(x, TT - 1, 0))
    y = _silu(cw_ref[0:1, :] * xp + cw_ref[1:2, :] * x + cw_ref[2:3, :] * xn)
    for h in range(2 * H):
        seg = y[:, h * DH:(h + 1) * DH]
        seg = seg * lax.rsqrt(jnp.sum(seg * seg, axis=-1, keepdims=True) + EPS)
        if h < H:
            seg = seg * (DH ** -0.5)
        qkv_ref[:, h * DH:(h + 1) * DH] = seg.astype(BF16)
    qkv_ref[:, 2 * HW:3 * HW] = y[:, 2 * HW:3 * HW].astype(BF16)
    z = sm_ref[...] + prm_ref[1:2, :]
    lane = lax.broadcasted_iota(jnp.int32, z.shape, 1)
    g_log = -jnp.exp(prm_ref[0:1, :]) * _softplus(z)
    out = jnp.where(lane < 8, g_log,
                    jnp.where(lane < 16, _sigmoid(z), jnp.where(lane < 24, z, -_softplus(-z))))
    g_ref[...] = out


def _prep2(gqkv, sm, conv_w, prm):
    nb = N_TOK // HALO
    return pl.pallas_call(
        _prep2_kernel,
        grid=(N_TILES,),
        in_specs=[pl.BlockSpec((TT, 3 * HW), lambda i: (i, 0)),
                  pl.BlockSpec((HALO, 3 * HW), lambda i: (jnp.maximum(i * (TT // HALO) - 1, 0), 0)),
                  pl.BlockSpec((HALO, 3 * HW), lambda i: (jnp.minimum((i + 1) * (TT // HALO), nb - 1), 0)),
                  pl.BlockSpec((TT, LANES), lambda i: (i, 0)),
                  pl.BlockSpec((3, 3 * HW), lambda i: (0, 0)),
                  pl.BlockSpec((8, LANES), lambda i: (0, 0))],
        out_specs=[pl.BlockSpec((TT, 3 * HW), lambda i: (i, 0)),
                   pl.BlockSpec((TT, LANES), lambda i: (i, 0))],
        out_shape=[jax.ShapeDtypeStruct((N_TOK, 3 * HW), BF16),
                   jax.ShapeDtypeStruct((N_TOK, LANES), F32)],
        compiler_params=_cparams("parallel"),
    )(gqkv, gqkv, gqkv, sm, conv_w, prm)


def _scan_tables():
    blk_f = np.zeros(N_CHUNKS, np.int32)
    blk_b = np.zeros(N_CHUNKS, np.int32)
    seq = np.zeros(N_CHUNKS, np.int32)
    first = np.zeros(N_CHUNKS, np.int32)
    last = np.zeros(N_CHUNKS, np.int32)
    step = 0
    base = 0
    sid = 0
    for n_seq, t_len in ((B_CTX, T_CTX), (B_LAT, T_LAT)):
        nc = t_len // CH
        for _ in range(n_seq):
            for c in range(nc):
                blk_f[step] = base + c
                blk_b[step] = base + nc - 1 - c
                seq[step] = sid
                first[step] = int(c == 0)
                last[step] = int(c == nc - 1)
                step += 1
            base += nc
            sid += 1
    return blk_f, blk_b, seq, first, last


def _ones_where(mask):
    return jnp.where(mask, 1.0, 0.0).astype(BF16)


def _scan_order_mask(d, n=CH):
    row = lax.broadcasted_iota(jnp.int32, (n, n), 0)
    col = lax.broadcasted_iota(jnp.int32, (n, n), 1)
    return (col <= row) if d == 0 else (col >= row)


def _row_of(col_vec):
    return jnp.broadcast_to(col_vec, (col_vec.shape[0], LANES)).T[0:1, :]


HC = H * CH


def _unit_tri_inverses(lbs, row, col):
    same = lambda sh: jnp.right_shift(row, sh) == jnp.right_shift(col, sh)
    zero = jnp.zeros_like(lbs[0])
    eye = jnp.where(row == col, 1.0, 0.0).astype(BF16)
    ms = [jnp.where(same(3), -lb, zero) for lb in lbs]
    m2s = [_dot(m, m).astype(BF16) for m in ms]
    m4s = [_dot(m2, m2).astype(BF16) for m2 in m2s]
    xs = [eye + m for m in ms]
    xs = [(x.astype(F32) + _dot(x, m2)).astype(BF16) for x, m2 in zip(xs, m2s)]
    xs = [(x.astype(F32) + _dot(x, m4)).astype(BF16) for x, m4 in zip(xs, m4s)]
    for sh in (3, 4, 5):
        pair = jnp.logical_and(same(sh + 1), jnp.logical_not(same(sh)))
        exs = [_dot(jnp.where(pair, lb, zero), x).astype(BF16) for lb, x in zip(lbs, xs)]
        fills = [_dot(x, ex) for x, ex in zip(xs, exs)]
        xs = [jnp.where(pair, (-fill).astype(BF16), x) for fill, x in zip(fills, xs)]
    return xs


def _head_stack(ref, r0, base):
    return jnp.concatenate([ref[r0:r0 + CH, base + h * DH:base + (h + 1) * DH] for h in range(H)], axis=0)


def _gdn_wy_kernel(qkv_ref, g_ref, u_ref, w_ref, qg_ref, kg_ref, attn_ref, gl_ref):
    row = lax.broadcasted_iota(jnp.int32, (HC, HC), 0)
    col = lax.broadcasted_iota(jnp.int32, (HC, HC), 1)
    same_head = jnp.right_shift(row, 6) == jnp.right_shift(col, 6)
    lane_head = jnp.right_shift(lax.broadcasted_iota(jnp.int32, (CH, HC), 1), 6)
    subs = range(TT // CH)
    chains = [(s, d) for s in subs for d in range(2)]
    g = [g_ref[s * CH:(s + 1) * CH, :] for s in subs]
    q_st = [_head_stack(qkv_ref, s * CH, 0) for s in subs]
    k_st = [_head_stack(qkv_ref, s * CH, HW) for s in subs]
    v_st = [_head_stack(qkv_ref, s * CH, 2 * HW).astype(F32) for s in subs]
    kf = [k.astype(F32) for k in k_st]
    kk = [_dot_nt(k, k) for k in k_st]
    qk = [_dot_nt(q, k) for q, k in zip(q_st, k_st)]
    order = [_ones_where(_scan_order_mask(d)) for d in range(2)]
    g3 = [_split3(a) for a in g]
    cols = [_dot(order[d], g3[s][0]) + (_dot(order[d], g3[s][1]) + _dot(order[d], g3[s][2]))
            for s, d in chains]
    incl = [jnp.logical_and(same_head, (col <= row) if d == 0 else (col >= row)) for d in range(2)]
    strict = [jnp.logical_and(same_head, (col < row) if d == 0 else (col > row)) for d in range(2)]
    gc, beta, g_last, decay, l_hi, l_lo, egc, rhs = [], [], [], [], [], [], [], []
    for c, (s, d) in enumerate(chains):
        last = CH - 1 if d == 0 else 0
        lanes = [d * H + h for h in range(H)]
        gc.append(jnp.concatenate([cols[c][:, l:l + 1] for l in lanes], axis=0))
        beta.append(jnp.concatenate([g[s][:, 8 + l:9 + l] for l in lanes], axis=0))
        g_last.append(jnp.concatenate(
            [jnp.broadcast_to(cols[c][last:last + 1, l:l + 1], (CH, 1)) for l in lanes], axis=0))
        gr = _row_of(gc[c])
        decay.append(jnp.where(incl[d], jnp.exp(jnp.where(incl[d], gc[c] - gr, 0.0)), 0.0))
        lmat = jnp.where(strict[d], beta[c] * kk[s] * decay[c], 0.0)
        l_hi.append(lmat.astype(BF16))
        l_lo.append((lmat - l_hi[c].astype(F32)).astype(BF16))
        egc.append(jnp.exp(gc[c]))
        rhs.append(jnp.concatenate([v_st[s] * beta[c], kf[s] * (beta[c] * egc[c])], axis=1))
        gl_rows = [jnp.broadcast_to(cols[c][last:last + 1, l:l + 1], (1, LANES)) for l in lanes]
        gl_ref[d, s * 8:(s + 1) * 8, :] = jnp.concatenate(gl_rows + [jnp.zeros((8 - H, LANES), F32)], axis=0)
    t_inv = _unit_tri_inverses(l_hi, row, col)
    y = [_dot(t, r.astype(BF16)) for t, r in zip(t_inv, rhs)]
    ys = [_split2(a) for a in y]
    ly = [_dot(l_hi[c], ys[c][0]) + (_dot(l_hi[c], ys[c][1]) + _dot(l_lo[c], ys[c][0])) for c in range(len(chains))]
    resid = [(rhs[c] - (y[c] + ly[c])).astype(BF16) for c in range(len(chains))]
    y = [y[c] + _dot(t_inv[c], resid[c]) for c in range(len(chains))]
    for c, (s, d) in enumerate(chains):
        r0 = s * CH
        attn = qk[s] * decay[c]
        a64 = jnp.zeros((CH, HC), F32)
        for h in range(H):
            a64 = jnp.where(lane_head == h, attn[h * CH:(h + 1) * CH, :], a64)
        attn_ref[d, r0:r0 + CH, :] = a64.astype(BF16)
        qg = (q_st[s].astype(F32) * egc[c]).astype(BF16)
        kg = (kf[s] * jnp.exp(g_last[c] - gc[c])).astype(BF16)
        for h in range(H):
            rs = slice(h * CH, (h + 1) * CH)
            ls = slice(h * DH, (h + 1) * DH)
            u_ref[d, r0:r0 + CH, ls] = y[c][rs, 0:DH]
            w_ref[d, r0:r0 + CH, ls] = y[c][rs, DH:2 * DH].astype(BF16)
            qg_ref[d, r0:r0 + CH, ls] = qg[rs, :]
            kg_ref[d, r0:r0 + CH, ls] = kg[rs, :]


def _gdn_wy(qkv, gates):
    spec = lambda w: pl.BlockSpec((2, TT, w), lambda i: (0, i, 0))
    shp = lambda w, dt: jax.ShapeDtypeStruct((2, N_TOK, w), dt)
    return pl.pallas_call(
        _gdn_wy_kernel,
        grid=(N_TILES,),
        in_specs=[pl.BlockSpec((TT, 3 * HW), lambda i: (i, 0)), pl.BlockSpec((TT, LANES), lambda i: (i, 0))],
        out_specs=[spec(HW), spec(HW), spec(HW), spec(HW), spec(HC),
                   pl.BlockSpec((2, 8 * (TT // CH), LANES), lambda i: (0, i, 0))],
        out_shape=[shp(HW, F32), shp(HW, BF16), shp(HW, BF16), shp(HW, BF16), shp(HC, BF16),
                   jax.ShapeDtypeStruct((2, 8 * N_CHUNKS, LANES), F32)],
        compiler_params=_cparams("parallel"),
    )(qkv, gates)


def _gdn_rec_body(ins, of_ref, ob_ref, s_scr):
    chains = [(d, h) for d in range(2) for h in range(H)]
    ls = [slice(h * DH, (h + 1) * DH) for h in range(H)]
    s = [s_scr[d, h] for d, h in chains]
    sb = [a.astype(BF16) for a in s]
    ws = [_dot(ins[6 * d + 1][0, :, ls[h]], sb[c]) for c, (d, h) in enumerate(chains)]
    inter = [_dot(ins[6 * d + 2][0, :, ls[h]], sb[c]) for c, (d, h) in enumerate(chains)]
    vb = [(ins[6 * d][0, :, ls[h]] - ws[c]).astype(BF16) for c, (d, h) in enumerate(chains)]
    intra = [_dot(ins[6 * d + 4][0, :, h * CH:(h + 1) * CH], vb[c]) for c, (d, h) in enumerate(chains)]
    upd = [_dot_tn(ins[6 * d + 3][0, :, ls[h]], vb[c]) for c, (d, h) in enumerate(chains)]
    for c, (d, h) in enumerate(chains):
        o_ref = of_ref if d == 0 else ob_ref
        o_ref[:, ls[h]] = inter[c] + intra[c]
        s_scr[d, h] = s[c] * jnp.exp(ins[6 * d + 5][0, h:h + 1, :]) + upd[c]


def _mlstm_body(qkv_refs, g_refs, o_refs, c_scr, n_scr, m_scr):
    scale = DH ** -0.5
    chains = [(d, h) for d in range(2) for h in range(H)]
    nch = range(len(chains))
    incl = [_scan_order_mask(d) for d in range(2)]
    g = [g_refs[d][...] for d in range(2)]
    cols = [_dot_exact_lhs(_ones_where(incl[d]), g[d]) for d in range(2)]
    g_t = [jnp.concatenate([g[d], cols[d]], axis=0).T for d in range(2)]
    q = [qkv_refs[d][:, h * DH:(h + 1) * DH] for d, h in chains]
    k = [qkv_refs[d][:, HW + h * DH:HW + (h + 1) * DH] for d, h in chains]
    v = [qkv_refs[d][:, 2 * HW + h * DH:2 * HW + (h + 1) * DH] for d, h in chains]
    qk = [_dot_nt(q[c], k[c]) * scale for c in nch]
    cst = [c_scr[d, h] for d, h in chains]
    qc = [_dot(q[c], cst[c].astype(BF16)) for c in nch]
    lf = [24 + d * H + h for d, h in chains]
    li = [16 + d * H + h for d, h in chains]
    dd = [d for d, _ in chains]
    last = [CH - 1 if d == 0 else 0 for d in dd]
    bc = [cols[dd[c]][:, lf[c]:lf[c] + 1] for c in nch]
    br = [g_t[dd[c]][lf[c]:lf[c] + 1, CH:2 * CH] for c in nch]
    ig_c = [g[dd[c]][:, li[c]:li[c] + 1] for c in nch]
    ig_r = [g_t[dd[c]][li[c]:li[c] + 1, 0:CH] for c in nch]
    dlog = [jnp.where(incl[dd[c]], bc[c] - br[c] + ig_r[c], -jnp.inf) for c in nch]
    dmax = [jnp.max(a, axis=1, keepdims=True) for a in dlog]
    b_last = [bc[c][last[c]:last[c] + 1, :] for c in nch]
    w_max = [jnp.max(b_last[c] - br[c] + ig_r[c], axis=1, keepdims=True) for c in nch]
    m = [m_scr[d, h:h + 1, 0:1] for d, h in chains]
    n = [n_scr[d, h:h + 1, :] for d, h in chains]
    m_inter = [bc[c] + m[c] for c in nch]
    m_row = [jnp.maximum(dmax[c], m_inter[c]) for c in nch]
    s = [qk[c] * jnp.exp(dlog[c] - m_row[c]) for c in nch]
    sv = [_dot(s[c].astype(BF16), v[c]) for c in nch]
    e_inter = [jnp.exp(m_inter[c] - m_row[c]) for c in nch]
    qn = [jnp.sum(q[c].astype(F32) * n[c], axis=1, keepdims=True) for c in nch]
    s_sum = [jnp.sum(a, axis=1, keepdims=True) for a in s]
    den = [e_inter[c] * qn[c] + s_sum[c] for c in nch]
    inv = [1.0 / jnp.maximum(jnp.abs(den[c]), jnp.exp(-m_row[c])) for c in nch]
    m_new = [jnp.maximum(b_last[c] + m[c], w_max[c]) for c in nch]
    f_c = [jnp.exp(b_last[c] + m[c] - m_new[c]) for c in nch]
    kw = [k[c].astype(F32) * (jnp.exp(b_last[c] - bc[c] + ig_c[c] - m_new[c]) * scale) for c in nch]
    kv = [_dot_tn(kw[c].astype(BF16), v[c]) for c in nch]
    for c, (d, h) in enumerate(chains):
        o_refs[d][:, h * DH:(h + 1) * DH] = (e_inter[c] * qc[c] + sv[c]) * inv[c]
        n_scr[d, h:h + 1, :] = f_c[c] * n[c] + jnp.sum(kw[c], axis=0, keepdims=True)
        m_scr[d, h:h + 1, :] = jnp.broadcast_to(m_new[c], (1, LANES))
        c_scr[d, h] = f_c[c] * cst[c] + kv[c]


def _rec_kernel(bf_ref, bb_ref, seq_ref, first_ref, last_ref, *refs):
    gdn_ins, (qkvf_ref, gf_ref, qkvb_ref, gb_ref, s0_ref, c0_ref, n0_ref, m0_ref) = refs[:12], refs[12:20]
    of_ref, ob_ref, hf_ref, hb_ref, sfin_ref, cfin_ref, nfin_ref, mfin_ref = refs[20:28]
    s_scr, c_scr, n_scr, m_scr = refs[28:]
    t = pl.program_id(0)

    @pl.when(first_ref[t] == 1)
    def _():
        s_scr[...] = s0_ref[0]
        c_scr[...] = c0_ref[0]
        n_scr[...] = n0_ref[0]
        m_scr[...] = m0_ref[0]

    _gdn_rec_body(gdn_ins, of_ref, ob_ref, s_scr)
    _mlstm_body((qkvf_ref, qkvb_ref), (gf_ref, gb_ref), (hf_ref, hb_ref), c_scr, n_scr, m_scr)

    @pl.when(last_ref[t] == 1)
    def _():
        sfin_ref[0] = s_scr[...]
        cfin_ref[0] = c_scr[...]
        nfin_ref[0] = n_scr[...]
        mfin_ref[0] = m_scr[...]


def _rec_scan(tables, wy, qkv, gates, s0, c0, n0, m0):
    fwd = lambda t, bf, bb, *_: (bf[t], 0)
    bwd = lambda t, bf, bb, *_: (bb[t], 0)

    def dir_specs(d):
        idx = (lambda t, bf, bb, *_: (0, bf[t], 0)) if d == 0 else (lambda t, bf, bb, *_: (1, bb[t], 0))
        return [pl.BlockSpec((1, CH, HW), idx)] * 4 + [pl.BlockSpec((1, CH, HC), idx),
                                                        pl.BlockSpec((1, 8, LANES), idx)]
    vec = pl.BlockSpec((1, 2, H, LANES), lambda t, bf, bb, sq, *_: (sq[t], 0, 0, 0))
    mat = pl.BlockSpec((1, 2, H, DH, DH), lambda t, bf, bb, sq, *_: (sq[t], 0, 0, 0, 0))
    tok = jax.ShapeDtypeStruct((N_TOK, HW), F32)
    mats = jax.ShapeDtypeStruct((N_SEQ, 2, H, DH, DH), F32)
    vecs = jax.ShapeDtypeStruct((N_SEQ, 2, H, LANES), F32)
    return pl.pallas_call(
        _rec_kernel,
        grid_spec=pltpu.PrefetchScalarGridSpec(
            num_scalar_prefetch=5, grid=(N_CHUNKS,),
            in_specs=dir_specs(0) + dir_specs(1)
            + [pl.BlockSpec((CH, 3 * HW), fwd), pl.BlockSpec((CH, LANES), fwd),
               pl.BlockSpec((CH, 3 * HW), bwd), pl.BlockSpec((CH, LANES), bwd), mat, mat, vec, vec],
            out_specs=[pl.BlockSpec((CH, HW), fwd), pl.BlockSpec((CH, HW), bwd),
                       pl.BlockSpec((CH, HW), fwd), pl.BlockSpec((CH, HW), bwd), mat, mat, vec, vec],
            scratch_shapes=[pltpu.VMEM((2, H, DH, DH), F32), pltpu.VMEM((2, H, DH, DH), F32),
                            pltpu.VMEM((2, H, LANES), F32), pltpu.VMEM((2, H, LANES), F32)]),
        out_shape=[tok, tok, tok, tok, mats, mats, vecs, vecs],
        compiler_params=_cparams("arbitrary"),
    )(*tables, *wy, *wy, qkv, gates, qkv, gates, s0, c0, n0, m0)


def _post_kernel(tbl_ref, x_ref, mod_ref, yfc_ref, yfl_ref, of_ref, ob_ref, hf_ref, hb_ref,
                 gz_ref, mo_ref, gates_ref, gn_ref, mn_ref, wbf_ref, wbg_ref, wbm_ref, wo_ref,
                 n2_ref, rwh_ref, rwl_ref, rb_ref,
                 xo_ref, h2_ref, info_ref, infot_ref, stats_ref, cnt_scr):
    i = pl.program_id(0)

    @pl.when(i == 0)
    def _():
        cnt_scr[...] = jnp.zeros_like(cnt_scr)

    yf = jnp.where(i < CTX_TILES, yfc_ref[...], yfl_ref[...])
    og = of_ref[...] + ob_ref[...]
    hm = hf_ref[...] + hb_ref[...]
    og_parts, hm_parts = [], []
    for h in range(H):
        sl = slice(h * DH, (h + 1) * DH)
        og_parts.append(_rms(og[:, sl]) * gn_ref[...] * _silu(gz_ref[:, sl].astype(F32)))
        hm_parts.append(_rms(hm[:, sl]) * mn_ref[:, sl] * _sigmoid(mo_ref[:, sl].astype(F32)))
    ogb = jnp.concatenate(og_parts, axis=1).astype(BF16)
    hmb = jnp.concatenate(hm_parts, axis=1).astype(BF16)
    y = (_sigmoid(gates_ref[:, 0:D].astype(F32)) * _dot(yf, wbf_ref[...])
         + _sigmoid(gates_ref[:, D:2 * D].astype(F32)) * _dot(ogb, wbg_ref[...])
         + _sigmoid(gates_ref[:, 2 * D:3 * D].astype(F32)) * _dot(hmb, wbm_ref[...]))
    x = x_ref[...] + mod_ref[0, 2:3, :] * _dot(y.astype(BF16), wo_ref[...])
    xo_ref[...] = x
    h2 = _rms(x) * n2_ref[...]
    h2 = h2 * (1.0 + mod_ref[0, 4:5, :]) + mod_ref[0, 3:4, :]
    hb2, hl2 = _split2(h2)
    h2_ref[...] = hb2
    logits = (_dot(hb2, rwh_ref[...]) + (_dot(hb2, rwl_ref[...]) + _dot(hl2, rwh_ref[...]))
              + rb_ref[...])
    lane = lax.broadcasted_iota(jnp.int32, logits.shape, 1).astype(F32)
    work = jnp.where(lane < N_EXP, logits, -jnp.inf)
    sel = jnp.zeros(logits.shape, F32)
    hits, exps = [], []
    top0 = None
    denom = None
    for kk in range(TOP_K):
        mx = jnp.max(work, axis=1, keepdims=True)
        first = jnp.min(jnp.where(work == mx, lane, float(LANES)), axis=1, keepdims=True)
        hit = lane == first
        if kk == 0:
            top0 = mx
        e = jnp.exp(mx - top0)
        denom = e if kk == 0 else denom + e
        hits.append(hit)
        exps.append(e)
        sel = jnp.where(hit, 1.0, sel)
        work = jnp.where(hit, -jnp.inf, work)
    r_i = lax.broadcasted_iota(jnp.int32, (TT, TT), 0)
    c_i = lax.broadcasted_iota(jnp.int32, (TT, TT), 1)
    within = _dot(_ones_where(c_i < r_i), sel.astype(BF16))
    tile_cnt = jnp.sum(sel, axis=0, keepdims=True)
    a_i = lax.broadcasted_iota(jnp.int32, (LANES, LANES), 0)
    b_i = lax.broadcasted_iota(jnp.int32, (LANES, LANES), 1)
    tile_off = _dot(jnp.broadcast_to(tile_cnt, (8, LANES)).astype(BF16), _ones_where(a_i < b_i))[0:1, :]
    local = tile_off + within
    info = jnp.zeros(logits.shape, F32)
    for kk in range(TOP_K):
        pos = jnp.sum(jnp.where(hits[kk], local, 0.0), axis=1, keepdims=True)
        info = jnp.where(lane == float(kk), pos, info)
        info = jnp.where(lane == float(TOP_K + kk), exps[kk] / denom, info)
    info_ref[...] = info
    infot_ref[...] = info.T[0:8, :]
    stats_ref[0] = jnp.concatenate([cnt_scr[0:1, :], tile_cnt, tile_off, jnp.zeros((5, LANES), F32)], axis=0)
    cnt_scr[0:1, :] = cnt_scr[0:1, :] + tile_cnt


def _post(cond_tbl, x, mod, yfc, yfl, o_f, o_b, h_f, h_b, gz, mo, gates, gn, mn, wbf, wbg, wbm, wo,
          n2, rwh, rwl, rb):
    tile = lambda w: pl.BlockSpec((TT, w), lambda i, t: (i, 0))
    const = lambda shape: pl.BlockSpec(shape, lambda i, t: (0,) * len(shape))
    in_specs = [tile(D), pl.BlockSpec((1, N_MOD, D), lambda i, t: (t[i], 0, 0)),
                pl.BlockSpec((TT, HW), lambda i, t: (jnp.minimum(i, CTX_TILES - 1), 0)),
                pl.BlockSpec((TT, HW), lambda i, t: (jnp.maximum(i - CTX_TILES, 0), 0)),
                tile(HW), tile(HW), tile(HW), tile(HW),
                tile(HW), tile(HW), tile(3 * D),
                const((1, DH)), const((1, HW)), const((HW, D)), const((HW, D)), const((HW, D)),
                const((D, D)), const((1, D)), const((D, LANES)), const((D, LANES)), const((1, LANES))]
    return pl.pallas_call(
        _post_kernel,
        grid_spec=pltpu.PrefetchScalarGridSpec(
            num_scalar_prefetch=1, grid=(N_TILES,), in_specs=in_specs,
            out_specs=[tile(D), tile(D), tile(LANES), pl.BlockSpec((8, TT), lambda i, t: (0, i)),
                       pl.BlockSpec((1, 8, LANES), lambda i, t: (i, 0, 0))],
            scratch_shapes=[pltpu.VMEM((8, LANES), F32)]),
        out_shape=[jax.ShapeDtypeStruct((N_TOK, D), F32), jax.ShapeDtypeStruct((N_TOK, D), BF16),
                   jax.ShapeDtypeStruct((N_TOK, LANES), F32), jax.ShapeDtypeStruct((8, N_TOK), F32),
                   jax.ShapeDtypeStruct((N_TILES, 8, LANES), F32)],
        compiler_params=_cparams("arbitrary"),
    )(cond_tbl, x, mod, yfc, yfl, o_f, o_b, h_f, h_b, gz, mo, gates, gn, mn, wbf, wbg, wbm, wo,
      n2, rwh, rwl, rb)


PAIRS = TT * TOP_K
RUN_BITS = tuple(1 << b for b in range(8, -1, -1))


def _slot_rows(start_slot, n_slots):
    return pl.ds(pl.multiple_of(start_slot * 8, 8), n_slots * 8)


def _for_each_run(base, cnt_ref, fn):
    def body(e, carry):
        cnt = cnt_ref[base + e]
        for bit in RUN_BITS:
            @pl.when((cnt & bit) != 0)
            def _():
                fn(base + e, cnt & (-2 * bit), bit)
        return carry
    lax.fori_loop(0, N_EXP, body, 0)


def _zero_fill(ps_ref, pc_ref, tail_ref, xs_hbm, zbuf, zsem, start):
    def piece(dst_slot, n):
        cp = pltpu.make_async_copy(zbuf.at[pl.ds(0, n * 8)], xs_hbm.at[_slot_rows(dst_slot, n)], zsem)
        if start:
            cp.start()
        else:
            cp.wait()

    def pad(e, carry):
        cnt = pc_ref[e]
        for bit in RUN_BITS[1:]:
            @pl.when((cnt & bit) != 0)
            def _():
                piece(ps_ref[e] + (cnt & (-2 * bit)), bit)
        return carry

    def tail(t, carry):
        piece(tail_ref[0] + t * TM, TM)
        return carry

    lax.fori_loop(0, N_EXP, pad, 0)
    lax.fori_loop(0, tail_ref[1], tail, 0)


def _dispatch_kernel(ss_ref, cnt_ref, toff_ref, ps_ref, pc_ref, tail_ref, h2_ref, infot_ref, xs_hbm,
                     stage, zbuf, sem, zsem):
    i = pl.program_id(0)
    buf = i % 2
    whole = lambda b: pltpu.make_async_copy(stage.at[b], xs_hbm.at[pl.ds(0, PAIRS * 8)], sem.at[b])

    @pl.when(i >= 2)
    def _():
        whole(buf).wait()

    r = lax.broadcasted_iota(jnp.int32, (PAIRS, TT), 0).astype(F32)
    pick = r == infot_ref[0:1, :]
    for kk in range(1, TOP_K):
        pick = jnp.logical_or(pick, r == infot_ref[kk:kk + 1, :])
    rows = _dot(_ones_where(pick), h2_ref[...])
    for cc in range(8):
        stage[buf, pl.ds(cc, PAIRS, stride=8), :] = rows[:, cc * LANES:(cc + 1) * LANES]

    def send(j, o, n):
        pltpu.make_async_copy(stage.at[buf, _slot_rows(toff_ref[j] + o, n)],
                              xs_hbm.at[_slot_rows(ss_ref[j] + o, n)], sem.at[buf]).start()

    _for_each_run(i * N_EXP, cnt_ref, send)

    @pl.when(i == 0)
    def _():
        zbuf[...] = jnp.zeros_like(zbuf)
        _zero_fill(ps_ref, pc_ref, tail_ref, xs_hbm, zbuf, zsem, True)

    @pl.when(i == N_TILES - 1)
    def _():
        whole(1 - buf).wait()
        whole(buf).wait()
        _zero_fill(ps_ref, pc_ref, tail_ref, xs_hbm, zbuf, zsem, False)


def _dispatch(slot_start, seg_cnt, seg_off, pad_start, pad_cnt, tail, h2, info_t):
    return pl.pallas_call(
        _dispatch_kernel,
        grid_spec=pltpu.PrefetchScalarGridSpec(
            num_scalar_prefetch=6, grid=(N_TILES,),
            in_specs=[pl.BlockSpec((TT, D), lambda i, *_: (i, 0)), pl.BlockSpec((8, TT), lambda i, *_: (0, i))],
            out_specs=pl.BlockSpec(memory_space=pl.ANY),
            scratch_shapes=[pltpu.VMEM((2, PAIRS * 8, LANES), F32), pltpu.VMEM((TM * 8, LANES), F32),
                            pltpu.SemaphoreType.DMA((2,)), pltpu.SemaphoreType.DMA(())]),
        out_shape=jax.ShapeDtypeStruct((S_MAX * 8, LANES), F32),
        compiler_params=_cparams("arbitrary"),
    )(slot_start, seg_cnt, seg_off, pad_start, pad_cnt, tail, h2, info_t)


def _gmm_kernel(te_ref, nu_ref, valid_ref, x_ref, wg_ref, bg_ref, wu_ref, bu_ref, wd_ref, bd_ref, o_ref,
                wg_scr, wu_scr, wd_scr):
    i = pl.program_id(0)
    used = i < nu_ref[0]

    @pl.when(used)
    def _():
        e = te_ref[i]
        e_prev = te_ref[jnp.maximum(i - 1, 0)]

        @pl.when(jnp.logical_or(i == 0, e != e_prev))
        def _():
            wg_scr[...] = wg_ref[0, 0].astype(BF16)
            wu_scr[...] = wu_ref[0, 0].astype(BF16)
            wd_scr[...] = wd_ref[0, 0].astype(BF16)

        x = jnp.concatenate([x_ref[pl.ds(cc, TM, stride=8), :] for cc in range(8)], axis=1)
        row = lax.broadcasted_iota(jnp.int32, (TM, 1), 0)
        x = jnp.where(row < valid_ref[i], x, 0.0).astype(BF16)
        acts = []
        for j in range(D // FF_CHUNK):
            cs = slice(j * FF_CHUNK, (j + 1) * FF_CHUNK)
            gate = jnp.minimum(_dot(x, wg_scr[:, cs]) + bg_ref[0, 0, :, cs], SWIGLU_LIMIT)
            up = jnp.clip(_dot(x, wu_scr[:, cs]) + bu_ref[0, 0, :, cs], -SWIGLU_LIMIT, SWIGLU_LIMIT)
            acts.append(((up + 1.0) * gate * _sigmoid(SWIGLU_ALPHA * gate)).astype(BF16))
        y = _dot(jnp.concatenate(acts, axis=1), wd_scr[...]) + bd_ref[0, 0]
        for cc in range(8):
            o_ref[pl.ds(cc, TM, stride=8), :] = y[:, cc * LANES:(cc + 1) * LANES]

    @pl.when(jnp.logical_not(used))
    def _():
        o_ref[...] = jnp.zeros_like(o_ref)


def _gmm(layer, tile_exp, n_used, valid, xs, wg, bg, wu, bu, wd, bd):
    wspec = pl.BlockSpec((1, 1, D, D), lambda i, te, nu, va: (layer, te[i], 0, 0))
    bspec = pl.BlockSpec((1, 1, 1, D), lambda i, te, nu, va: (layer, te[i], 0, 0))
    return pl.pallas_call(
        _gmm_kernel,
        grid_spec=pltpu.PrefetchScalarGridSpec(
            num_scalar_prefetch=3, grid=(NT_MAX,),
            in_specs=[pl.BlockSpec((TM * 8, LANES), lambda i, te, nu, va: (jnp.minimum(i, nu[0] - 1), 0)),
                      wspec, bspec, wspec, bspec, wspec, bspec],
            out_specs=pl.BlockSpec((TM * 8, LANES), lambda i, te, nu, va: (i, 0)),
            scratch_shapes=[pltpu.VMEM((D, D), BF16)] * 3),
        out_shape=jax.ShapeDtypeStruct((S_MAX * 8, LANES), F32),
        compiler_params=_cparams("arbitrary"),
    )(tile_exp, n_used, valid, xs, wg, bg.reshape(DEPTH, N_EXP, 1, D), wu, bu.reshape(DEPTH, N_EXP, 1, D),
      wd, bd.reshape(DEPTH, N_EXP, 1, D))


def _combine_kernel(tbl_ref, ss_ref, cnt_ref, toff_ref, y_hbm, info_ref, x_ref, mod_ref, g_ref, *refs,
                    final):
    outs, (ybuf, sem) = refs[:-2], refs[-2:]
    i = pl.program_id(0)
    buf = i % 2

    def fetch(tile, b):
        def recv(j, o, n):
            pltpu.make_async_copy(y_hbm.at[_slot_rows(ss_ref[j] + o, n)],
                                  ybuf.at[b, _slot_rows(toff_ref[j] + o, n)], sem.at[b]).start()
        _for_each_run(tile * N_EXP, cnt_ref, recv)

    @pl.when(i == 0)
    def _():
        fetch(0, 0)

    @pl.when(i + 1 < N_TILES)
    def _():
        fetch(i + 1, 1 - buf)

    r = lax.broadcasted_iota(jnp.int32, (TT, PAIRS), 1).astype(F32)
    wm = jnp.zeros((TT, PAIRS), F32)
    for kk in range(TOP_K):
        wm = jnp.where(r == info_ref[:, kk:kk + 1], info_ref[:, TOP_K + kk:TOP_K + kk + 1], wm)
    pltpu.make_async_copy(y_hbm.at[pl.ds(0, PAIRS * 8)], ybuf.at[buf], sem.at[buf]).wait()
    y = jnp.concatenate([ybuf[buf, pl.ds(cc, PAIRS, stride=8), :] for cc in range(8)], axis=1)
    x = x_ref[...] + mod_ref[0, 5:6, :] * _dot(wm.astype(BF16), y.astype(BF16))
    if not final:
        outs[0][...] = x
    else:
        res = _rms(x) * g_ref[...]

        @pl.when(i < CTX_TILES)
        def _():
            outs[0][...] = res

        @pl.when(i >= CTX_TILES)
        def _():
            outs[1][...] = res


def _combine(final, cond_tbl, slot_start, seg_cnt, seg_off, y_sorted, info, x, mod, g):
    tile = lambda w: pl.BlockSpec((TT, w), lambda i, *_: (i, 0))
    if final:
        out_specs = [pl.BlockSpec((TT, D), lambda i, *_: (jnp.minimum(i, CTX_TILES - 1), 0)),
                     pl.BlockSpec((TT, D), lambda i, *_: (jnp.maximum(i - CTX_TILES, 0), 0))]
        out_shape = [jax.ShapeDtypeStruct((N_CTX, D), F32), jax.ShapeDtypeStruct((N_LAT, D), F32)]
    else:
        out_specs = [tile(D)]
        out_shape = [jax.ShapeDtypeStruct((N_TOK, D), F32)]
    return pl.pallas_call(
        functools.partial(_combine_kernel, final=final),
        grid_spec=pltpu.PrefetchScalarGridSpec(
            num_scalar_prefetch=4, grid=(N_TILES,),
            in_specs=[pl.BlockSpec(memory_space=pl.ANY), tile(LANES), tile(D),
                      pl.BlockSpec((1, N_MOD, D), lambda i, t, *_: (t[i], 0, 0)),
                      pl.BlockSpec((1, D), lambda i, *_: (0, 0))],
            out_specs=out_specs,
            scratch_shapes=[pltpu.VMEM((2, PAIRS * 8, LANES), F32), pltpu.SemaphoreType.DMA((2,))]),
        out_shape=out_shape,
        compiler_params=_cparams("arbitrary"),
    )(cond_tbl, slot_start, seg_cnt, seg_off, y_sorted, info, x, mod, g)


def _dft_consts(t_len):
    k = np.arange(t_len, dtype=np.int64)
    ang = 2.0 * np.pi * ((k[:, None] * k[None, :]) % t_len).astype(np.float64) / t_len
    return np.concatenate([np.cos(ang), -np.sin(ang)], axis=1).astype(np.float32)


def _pos_table():
    quarter = D // 4
    omega = 1.0 / (10000.0 ** (np.arange(quarter, dtype=np.float32) / np.float32(quarter)))
    omega = omega.astype(np.float32).astype(np.float64)
    t = np.arange(T_LAT)
    ang_r = (t // GRID_W).astype(np.float64)[:, None] * omega
    ang_c = (t % GRID_W).astype(np.float64)[:, None] * omega
    return np.concatenate([np.sin(ang_r), np.cos(ang_r), np.sin(ang_c), np.cos(ang_c)],
                          axis=-1).astype(np.float32)


def _repack_w_in(w):
    f, gq, gk, gv, gz, ga, gb, mq, mk, mv, mo, mi, mf, g1, g2, g3 = jnp.split(
        w, np.cumsum([512, 512, 512, 512, 512, 8, 8, 512, 512, 512, 512, 8, 8, 1024, 1024])[:].tolist(),
        axis=1)
    main = jnp.concatenate([f, gq, gk, gv, gz, mq, mk, mv, mo, g1, g2, g3], axis=1).astype(BF16)
    small = jnp.concatenate([ga, gb, mi, mf, jnp.zeros((D, LANES - 32), F32)], axis=1)
    hi = small.astype(BF16)
    lo = (small - hi.astype(F32)).astype(BF16)
    return main, hi, lo


def _route_glue(stats):
    seg_base = stats[:, 0, :N_EXP].astype(jnp.int32)
    seg_cnt = stats[:, 1, :N_EXP].astype(jnp.int32)
    seg_off = stats[:, 2, :N_EXP].astype(jnp.int32)
    counts = seg_base[-1] + seg_cnt[-1]
    region = ((counts + TM - 1) // TM) * TM
    ends = jnp.cumsum(region)
    off = ends - region
    n_used = ends[-1] // TM
    tile_start = jnp.arange(NT_MAX, dtype=jnp.int32) * TM
    tile_exp = jnp.minimum(jnp.sum((tile_start[:, None] >= ends[None, :]).astype(jnp.int32), axis=1), N_EXP - 1)
    last_exp = jnp.take(tile_exp, jnp.maximum(n_used - 1, 0))
    tile_exp = jnp.where(tile_start < ends[-1], tile_exp, last_exp)
    valid = jnp.clip(jnp.take(off + counts, tile_exp) - tile_start, 0, TM)
    slot_start = off[None, :] + seg_base
    tail = jnp.stack([ends[-1], (S_MAX - ends[-1]) // TM]).astype(jnp.int32)
    return (tile_exp.astype(jnp.int32), n_used.reshape(1).astype(jnp.int32), valid.astype(jnp.int32),
            slot_start.reshape(-1), seg_cnt.reshape(-1), seg_off.reshape(-1),
            (off + counts).astype(jnp.int32), (region - counts).astype(jnp.int32), tail)


def kernel(x_prompt, x_sample, state_gdn, state_mlstm_c, state_mlstm_n, state_mlstm_m, c, c_ctx,
           w_ada, b_ada, norm1_g, norm2_g, w_in, gdn_conv_w, gdn_a_log, gdn_dt_bias, gdn_norm_g,
           mlstm_i_bias, mlstm_f_bias, mlstm_norm_g, w_branch_f, w_branch_g, w_branch_m, w_out,
           router_w, router_b, exp_w_gate, exp_b_gate, exp_w_up, exp_b_up, exp_w_down, exp_b_down,
           final_norm_g):
    cond_tbl = jnp.asarray(np.concatenate([np.zeros(CTX_TILES, np.int32),
                                           1 + np.arange(N_TILES - CTX_TILES, dtype=np.int32) // LAT_TILES_PER_SEQ]))
    tables = tuple(jnp.asarray(t) for t in _scan_tables())
    cond8 = jnp.concatenate([c_ctx[None, :], c, jnp.zeros((8 - 1 - B_LAT, D), F32)], axis=0)
    mod_all = _ada(cond8, w_ada, b_ada).reshape(DEPTH, 8, N_MOD, D)

    pos = jnp.asarray(_pos_table())
    dft_c = jnp.asarray(_dft_consts(T_CTX)).astype(BF16)
    dft_l = jnp.asarray(_dft_consts(T_LAT)).astype(BF16)
    ang = 2.0 * np.pi * ((np.arange(DH)[:, None] * np.arange(DH)[None, :]) % DH) / DH
    c128 = jnp.asarray(np.cos(ang).astype(np.float32)).astype(BF16)
    s128 = jnp.asarray(np.sin(ang).astype(np.float32)).astype(BF16)

    zeros_s = jnp.zeros((B_CTX, 2, H, DH, DH), F32)
    zeros_v = jnp.zeros((B_CTX, 2, H, LANES), F32)
    x = None
    gdn_states, c_states, n_states, m_states = [], [], [], []
    y_prompt = y_sample = None
    for l in range(DEPTH):
        mod = mod_all[l]
        wm, wsh, wsl = _repack_w_in(w_in[l])
        if l == 0:
            outs = _pre(True, (x_prompt.reshape(N_CTX, D), x_sample.reshape(N_LAT, D), pos), mod,
                        norm1_g[l][None, :], wm, wsh, wsl, cond_tbl)
            x, outs = outs[0], outs[1:]
        else:
            outs = _pre(False, x, mod, norm1_g[l][None, :], wm, wsh, wsl, cond_tbl)
        f_all, gqkv, gz, mqkv, mo, gates, sm = outs

        yfc = _fourier(f_all, dft_c, c128, s128, B_CTX, T_CTX, 0)
        yfl = _fourier(f_all, dft_l, c128, s128, B_LAT, T_LAT, N_CTX)

        prm = jnp.zeros((8, LANES), F32)
        prm = prm.at[0, 0:8].set(gdn_a_log[l].reshape(-1))
        prm = prm.at[1, 0:8].set(gdn_dt_bias[l].reshape(-1))
        prm = prm.at[1, 16:24].set(mlstm_i_bias[l].reshape(-1))
        prm = prm.at[1, 24:32].set(mlstm_f_bias[l].reshape(-1))
        qkv_c, gts = _prep2(gqkv, sm, gdn_conv_w[l], prm)

        s0 = jnp.concatenate([zeros_s, state_gdn[:, l]], axis=0)
        c0 = jnp.concatenate([zeros_s, state_mlstm_c[:, l]], axis=0)
        n0 = jnp.concatenate([zeros_v, state_mlstm_n[:, l]], axis=0)
        m0 = jnp.concatenate([zeros_v, jnp.broadcast_to(state_mlstm_m[:, l][..., None],
                                                         (B_LAT, 2, H, LANES))], axis=0)
        o_f, o_b, h_f, h_b, s_fin, c_fin, n_fin, m_fin = _rec_scan(
            tables, _gdn_wy(qkv_c, gts), mqkv, gts, s0, c0, n0, m0)
        gdn_states.append(s_fin[:B_CTX])
        c_states.append(c_fin[:B_CTX])
        n_states.append(n_fin[:B_CTX])
        m_states.append(m_fin[:B_CTX, :, :, 0])

        rw = jnp.concatenate([router_w[l], jnp.zeros((D, LANES - N_EXP), F32)], axis=1)
        rwh = rw.astype(BF16)
        rwl = (rw - rwh.astype(F32)).astype(BF16)
        rb = jnp.concatenate([router_b[l], jnp.zeros((LANES - N_EXP,), F32)])[None, :]
        x, h2, info, info_t, stats = _post(
            cond_tbl, x, mod, yfc, yfl, o_f, o_b, h_f, h_b, gz, mo, gates,
            gdn_norm_g[l][None, :], mlstm_norm_g[l][None, :],
            w_branch_f[l].astype(BF16), w_branch_g[l].astype(BF16), w_branch_m[l].astype(BF16),
            w_out[l].astype(BF16), norm2_g[l][None, :], rwh, rwl, rb)

        tile_exp, n_used, valid, slot_start, seg_cnt, seg_off, pad_start, pad_cnt, tail = _route_glue(stats)
        xs_sorted = _dispatch(slot_start, seg_cnt, seg_off, pad_start, pad_cnt, tail, h2, info_t)
        y_sorted = _gmm(l, tile_exp, n_used, valid, xs_sorted, exp_w_gate, exp_b_gate, exp_w_up,
                        exp_b_up, exp_w_down, exp_b_down)
        res = _combine(l + 1 == DEPTH, cond_tbl, slot_start, seg_cnt, seg_off, y_sorted, info, x, mod,
                       final_norm_g[None, :])
        if l + 1 < DEPTH:
            x = res[0]
        else:
            y_prompt, y_sample = res

    return (y_prompt.reshape(B_CTX, T_CTX, D), y_sample.reshape(B_LAT, T_LAT, D),
            jnp.stack(gdn_states, axis=1), jnp.stack(c_states, axis=1),
            jnp.stack(n_states, axis=1), jnp.stack(m_states, axis=1))
```

```python
import functools
import math

import numpy as np
import jax
import jax.numpy as jnp
from jax import lax
from jax.experimental import pallas as pl
from jax.experimental.pallas import tpu as pltpu

F32 = jnp.float32
BF16 = jnp.bfloat16

D = 1024
DEPTH = 2
B_CTX, T_CTX = 16, 256
B_LAT, T_LAT = 4, 2048
N_CTX = B_CTX * T_CTX
N_LAT = B_LAT * T_LAT
N_TOK = N_CTX + N_LAT
N_SEQ = B_CTX + B_LAT
GRID_W = 64
H = 4
DH = 128
HW = H * DH
CH = 64
TT = 256
N_TILES = N_TOK // TT
CTX_TILES = N_CTX // TT
LAT_TILES_PER_SEQ = T_LAT // TT
N_CHUNKS = N_TOK // CH
N_EXP = 32
TOP_K = 4
N_MOD = 6
EPS = 1e-6
SWIGLU_ALPHA = 1.702
SWIGLU_LIMIT = 7.0
LANES = 128
TM = 256
FF_CHUNK = 256
S_MAX = N_TOK * TOP_K + N_EXP * (TM - 1)
S_MAX = ((S_MAX + TM - 1) // TM) * TM
NT_MAX = S_MAX // TM

W_MAIN = 512 + 1536 + 512 + 1536 + 512 + 3072


def _dot(a, b):
    return jnp.dot(a, b, preferred_element_type=F32)


def _dot_nt(a, b):
    return lax.dot_general(a, b, (((1,), (1,)), ((), ())), preferred_element_type=F32)


def _dot_tn(a, b):
    return lax.dot_general(a, b, (((0,), (0,)), ((), ())), preferred_element_type=F32)


def _split2(a):
    hi = a.astype(BF16)
    lo = (a - hi.astype(F32)).astype(BF16)
    return hi, lo


def _split3(a):
    hi = a.astype(BF16)
    r = a - hi.astype(F32)
    mid = r.astype(BF16)
    lo = (r - mid.astype(F32)).astype(BF16)
    return hi, mid, lo


def _dot3(a, b):
    ah, al = _split2(a)
    bh, bl = _split2(b)
    return _dot(ah, bh) + (_dot(ah, bl) + _dot(al, bh))


def _dot_exact_lhs(a_bf16, b):
    bh, bm, bl = _split3(b)
    return _dot(a_bf16, bh) + (_dot(a_bf16, bm) + _dot(a_bf16, bl))


def _sigmoid(x):
    return 0.5 * jnp.tanh(0.5 * x) + 0.5


def _silu(x):
    return x * _sigmoid(x)


def _softplus(x):
    return jnp.maximum(x, 0.0) + jnp.log(1.0 + jnp.exp(-jnp.abs(x)))


def _rms(x):
    return x * lax.rsqrt(jnp.mean(x * x, axis=-1, keepdims=True) + EPS)


def _cparams(*sem):
    return pltpu.CompilerParams(dimension_semantics=tuple(sem))


def _ada_kernel(c_ref, w_ref, b_ref, o_ref):
    o_ref[0] = _dot3(_silu(c_ref[...]), w_ref[0]) + b_ref[0]


def _ada(cond8, w_ada, b_ada):
    nb = 1536
    return pl.pallas_call(
        _ada_kernel,
        grid=(DEPTH, N_MOD * D // nb),
        in_specs=[pl.BlockSpec((8, D), lambda l, j: (0, 0)),
                  pl.BlockSpec((1, D, nb), lambda l, j: (l, 0, j)),
                  pl.BlockSpec((1, 1, nb), lambda l, j: (l, 0, j))],
        out_specs=pl.BlockSpec((1, 8, nb), lambda l, j: (l, 0, j)),
        out_shape=jax.ShapeDtypeStruct((DEPTH, 8, N_MOD * D), F32),
        compiler_params=_cparams("parallel", "parallel"),
    )(cond8, w_ada, b_ada.reshape(DEPTH, 1, N_MOD * D))


def _pre_body(x, mod_ref, n1_ref, wm_ref, wsh_ref, wsl_ref, outs):
    f_ref, gqkv_ref, gz_ref, mqkv_ref, mo_ref, gates_ref, sm_ref = outs
    h = _rms(x) * n1_ref[...]
    h = h * (1.0 + mod_ref[0, 1:2, :]) + mod_ref[0, 0:1, :]
    hb, hl = _split2(h)
    off = 0
    for ref, width in ((f_ref, 512), (gqkv_ref, 1536), (gz_ref, 512), (mqkv_ref, 1536),
                       (mo_ref, 512), (gates_ref, 3072)):
        ref[...] = _dot(hb, wm_ref[:, off:off + width]).astype(BF16)
        off += width
    sm_ref[...] = _dot(hb, wsh_ref[...]) + (_dot(hb, wsl_ref[...]) + _dot(hl, wsh_ref[...]))


def _pre_first_kernel(tbl_ref, xp_ref, xs_ref, pos_ref, mod_ref, n1_ref, wm_ref, wsh_ref, wsl_ref,
                      x_out_ref, *outs):
    i = pl.program_id(0)
    x = jnp.where(i < CTX_TILES, xp_ref[...], xs_ref[...] + pos_ref[...])
    x_out_ref[...] = x
    _pre_body(x, mod_ref, n1_ref, wm_ref, wsh_ref, wsl_ref, outs)


def _pre_next_kernel(tbl_ref, x_ref, mod_ref, n1_ref, wm_ref, wsh_ref, wsl_ref, *outs):
    _pre_body(x_ref[...], mod_ref, n1_ref, wm_ref, wsh_ref, wsl_ref, outs)


_PRE_OUT_WIDTHS = ((512, BF16), (1536, BF16), (512, BF16), (1536, BF16), (512, BF16), (3072, BF16),
                   (LANES, F32))


def _pre(first, xs_in, mod, n1, wm, wsh, wsl, cond_tbl):
    tile = lambda w: pl.BlockSpec((TT, w), lambda i, t: (i, 0))
    const = lambda shape: pl.BlockSpec(shape, lambda i, t: (0,) * len(shape))
    w_specs = [pl.BlockSpec((1, N_MOD, D), lambda i, t: (t[i], 0, 0)), const((1, D)),
               const((D, W_MAIN)), const((D, LANES)), const((D, LANES))]
    out_specs = [tile(w) for w, _ in _PRE_OUT_WIDTHS]
    out_shape = [jax.ShapeDtypeStruct((N_TOK, w), dt) for w, dt in _PRE_OUT_WIDTHS]
    if first:
        x_prompt2, x_sample2, pos = xs_in
        in_specs = [pl.BlockSpec((TT, D), lambda i, t: (jnp.minimum(i, CTX_TILES - 1), 0)),
                    pl.BlockSpec((TT, D), lambda i, t: (jnp.maximum(i - CTX_TILES, 0), 0)),
                    pl.BlockSpec((TT, D), lambda i, t: (jnp.maximum(i - CTX_TILES, 0) % LAT_TILES_PER_SEQ, 0))]
        kern = _pre_first_kernel
        out_specs = [tile(D)] + out_specs
        out_shape = [jax.ShapeDtypeStruct((N_TOK, D), F32)] + out_shape
        args = (x_prompt2, x_sample2, pos)
    else:
        in_specs = [tile(D)]
        kern = _pre_next_kernel
        args = (xs_in,)
    return pl.pallas_call(
        kern,
        grid_spec=pltpu.PrefetchScalarGridSpec(
            num_scalar_prefetch=1, grid=(N_TILES,), in_specs=in_specs + w_specs, out_specs=out_specs),
        out_shape=out_shape,
        compiler_params=_cparams("parallel"),
    )(cond_tbl, *args, mod, n1, wm, wsh, wsl)


def _fourier_kernel(x_ref, dft_ref, c_ref, s_ref, o_ref, z_scr, *, t_len, scale):
    @pl.when(pl.program_id(1) == 0)
    def _():
        for g in range(H):
            xg = x_ref[:, g * DH:(g + 1) * DH]
            z_scr[0:t_len, g * DH:(g + 1) * DH] = _dot(xg, c_ref[...]).astype(BF16)
            z_scr[t_len:2 * t_len, g * DH:(g + 1) * DH] = _dot(xg, s_ref[...]).astype(BF16)

    o_ref[...] = (_dot(dft_ref[...], z_scr[...]) * scale).astype(BF16)


def _fourier(f_all, dft, c128, s128, n_seq, t_len, row0):
    tr = min(t_len, 512)
    blk0 = row0 // t_len
    return pl.pallas_call(
        functools.partial(_fourier_kernel, t_len=t_len, scale=1.0 / math.sqrt(t_len * DH)),
        grid=(n_seq, t_len // tr),
        in_specs=[pl.BlockSpec((t_len, HW), lambda b, r: (blk0 + b, 0)),
                  pl.BlockSpec((tr, 2 * t_len), lambda b, r: (r, 0)),
                  pl.BlockSpec((DH, DH), lambda b, r: (0, 0)),
                  pl.BlockSpec((DH, DH), lambda b, r: (0, 0))],
        out_specs=pl.BlockSpec((tr, HW), lambda b, r: (b * (t_len // tr) + r, 0)),
        out_shape=jax.ShapeDtypeStruct((n_seq * t_len, HW), BF16),
        scratch_shapes=[pltpu.VMEM((2 * t_len, HW), BF16)],
        compiler_params=_cparams("parallel", "arbitrary"),
    )(f_all, dft, c128, s128)


HALO = 16


def _prep2_kernel(cur_ref, prev_ref, next_ref, sm_ref, cw_ref, prm_ref, qkv_ref, g_ref):
    i = pl.program_id(0)
    j = jnp.maximum(i - CTX_TILES, 0) % LAT_TILES_PER_SEQ
    is_lat = i >= CTX_TILES
    has_prev = jnp.logical_and(is_lat, j > 0)
    has_next = jnp.logical_and(is_lat, j < LAT_TILES_PER_SEQ - 1)
    x = cur_ref[...].astype(F32)
    row = lax.broadcasted_iota(jnp.int32, x.shape, 0)
    prev_row = jnp.where(has_prev, prev_ref[HALO - 1:HALO, :].astype(F32), 0.0)
    next_row = jnp.where(has_next, next_ref[0:1, :].astype(F32), 0.0)
    xp = jnp.where(row == 0, prev_row, pltpu.roll(x, 1, 0))
    xn = jnp.where(row == TT - 1, next_row, pltpu.roll(x, TT - 1, 0))
    y = _silu(cw_ref[0:1, :] * xp + cw_ref[1:2, :] * x + cw_ref[2:3, :] * xn)
    for h in range(2 * H):
        seg = y[:, h * DH:(h + 1) * DH]
        seg = seg * lax.rsqrt(jnp.sum(seg * seg, axis=-1, keepdims=True) + EPS)
        if h < H:
            seg = seg * (DH ** -0.5)
        qkv_ref[:, h * DH:(h + 1) * DH] = seg.astype(BF16)
    qkv_ref[:, 2 * HW:3 * HW] = y[:, 2 * HW:3 * HW].astype(BF16)
    z = sm_ref[...] + prm_ref[1:2, :]
    lane = lax.broadcasted_iota(jnp.int32, z.shape, 1)
    g_log = -jnp.exp(prm_ref[0:1, :]) * _softplus(z)
    out = jnp.where(lane < 8, g_log,
                    jnp.where(lane < 16, _sigmoid(z), jnp.where(lane < 24, z, -_softplus(-z))))
    g_ref[...] = out


def _prep2(gqkv, sm, conv_w, prm):
    nb = N_TOK // HALO
    return pl.pallas_call(
        _prep2_kernel,
        grid=(N_TILES,),
        in_specs=[pl.BlockSpec((TT, 3 * HW), lambda i: (i, 0)),
                  pl.BlockSpec((HALO, 3 * HW), lambda i: (jnp.maximum(i * (TT // HALO) - 1, 0), 0)),
                  pl.BlockSpec((HALO, 3 * HW), lambda i: (jnp.minimum((i + 1) * (TT // HALO), nb - 1), 0)),
                  pl.BlockSpec((TT, LANES), lambda i: (i, 0)),
                  pl.BlockSpec((3, 3 * HW), lambda i: (0, 0)),
                  pl.BlockSpec((8, LANES), lambda i: (0, 0))],
        out_specs=[pl.BlockSpec((TT, 3 * HW), lambda i: (i, 0)),
                   pl.BlockSpec((TT, LANES), lambda i: (i, 0))],
        out_shape=[jax.ShapeDtypeStruct((N_TOK, 3 * HW), BF16),
                   jax.ShapeDtypeStruct((N_TOK, LANES), F32)],
        compiler_params=_cparams("parallel"),
    )(gqkv, gqkv, gqkv, sm, conv_w, prm)


def _scan_tables():
    blk_f = np.zeros(N_CHUNKS, np.int32)
    blk_b = np.zeros(N_CHUNKS, np.int32)
    seq = np.zeros(N_CHUNKS, np.int32)
    first = np.zeros(N_CHUNKS, np.int32)
    last = np.zeros(N_CHUNKS, np.int32)
    step = 0
    base = 0
    sid = 0
    for n_seq, t_len in ((B_CTX, T_CTX), (B_LAT, T_LAT)):
        nc = t_len // CH
        for _ in range(n_seq):
            for c in range(nc):
                blk_f[step] = base + c
                blk_b[step] = base + nc - 1 - c
                seq[step] = sid
                first[step] = int(c == 0)
                last[step] = int(c == nc - 1)
                step += 1
            base += nc
            sid += 1
    return blk_f, blk_b, seq, first, last


def _ones_where(mask):
    return jnp.where(mask, 1.0, 0.0).astype(BF16)


def _scan_order_mask(d, n=CH):
    row = lax.broadcasted_iota(jnp.int32, (n, n), 0)
    col = lax.broadcasted_iota(jnp.int32, (n, n), 1)
    return (col <= row) if d == 0 else (col >= row)


def _row_of(col_vec):
    return jnp.broadcast_to(col_vec, (col_vec.shape[0], LANES)).T[0:1, :]


HC = H * CH


def _unit_tri_inverses(lbs, row, col):
    same = lambda sh: jnp.right_shift(row, sh) == jnp.right_shift(col, sh)
    zero = jnp.zeros_like(lbs[0])
    eye = jnp.where(row == col, 1.0, 0.0).astype(BF16)
    ms = [jnp.where(same(3), -lb, zero) for lb in lbs]
    m2s = [_dot(m, m).astype(BF16) for m in ms]
    m4s = [_dot(m2, m2).astype(BF16) for m2 in m2s]
    xs = [eye + m for m in ms]
    xs = [(x.astype(F32) + _dot(x, m2)).astype(BF16) for x, m2 in zip(xs, m2s)]
    xs = [(x.astype(F32) + _dot(x, m4)).astype(BF16) for x, m4 in zip(xs, m4s)]
    for sh in (3, 4, 5):
        pair = jnp.logical_and(same(sh + 1), jnp.logical_not(same(sh)))
        exs = [_dot(jnp.where(pair, lb, zero), x).astype(BF16) for lb, x in zip(lbs, xs)]
        fills = [_dot(x, ex) for x, ex in zip(xs, exs)]
        xs = [jnp.where(pair, (-fill).astype(BF16), x) for fill, x in zip(fills, xs)]
    return xs


def _head_stack(ref, r0, base):
    return jnp.concatenate([ref[r0:r0 + CH, base + h * DH:base + (h + 1) * DH] for h in range(H)], axis=0)


def _gdn_wy_kernel(qkv_ref, g_ref, u_ref, w_ref, qg_ref, kg_ref, attn_ref, gl_ref):
    row = lax.broadcasted_iota(jnp.int32, (HC, HC), 0)
    col = lax.broadcasted_iota(jnp.int32, (HC, HC), 1)
    same_head = jnp.right_shift(row, 6) == jnp.right_shift(col, 6)
    lane_head = jnp.right_shift(lax.broadcasted_iota(jnp.int32, (CH, HC), 1), 6)
    subs = range(TT // CH)
    chains = [(s, d) for s in subs for d in range(2)]
    g = [g_ref[s * CH:(s + 1) * CH, :] for s in subs]
    q_st = [_head_stack(qkv_ref, s * CH, 0) for s in subs]
    k_st = [_head_stack(qkv_ref, s * CH, HW) for s in subs]
    v_st = [_head_stack(qkv_ref, s * CH, 2 * HW).astype(F32) for s in subs]
    kf = [k.astype(F32) for k in k_st]
    kk = [_dot_nt(k, k) for k in k_st]
    qk = [_dot_nt(q, k) for q, k in zip(q_st, k_st)]
    order = [_ones_where(_scan_order_mask(d)) for d in range(2)]
    g3 = [_split3(a) for a in g]
    cols = [_dot(order[d], g3[s][0]) + (_dot(order[d], g3[s][1]) + _dot(order[d], g3[s][2]))
            for s, d in chains]
    incl = [jnp.logical_and(same_head, (col <= row) if d == 0 else (col >= row)) for d in range(2)]
    strict = [jnp.logical_and(same_head, (col < row) if d == 0 else (col > row)) for d in range(2)]
    gc, beta, g_last, decay, l_hi, l_lo, egc, rhs = [], [], [], [], [], [], [], []
    for c, (s, d) in enumerate(chains):
        last = CH - 1 if d == 0 else 0
        lanes = [d * H + h for h in range(H)]
        gc.append(jnp.concatenate([cols[c][:, l:l + 1] for l in lanes], axis=0))
        beta.append(jnp.concatenate([g[s][:, 8 + l:9 + l] for l in lanes], axis=0))
        g_last.append(jnp.concatenate(
            [jnp.broadcast_to(cols[c][last:last + 1, l:l + 1], (CH, 1)) for l in lanes], axis=0))
        gr = _row_of(gc[c])
        decay.append(jnp.where(incl[d], jnp.exp(jnp.where(incl[d], gc[c] - gr, 0.0)), 0.0))
        lmat = jnp.where(strict[d], beta[c] * kk[s] * decay[c], 0.0)
        l_hi.append(lmat.astype(BF16))
        l_lo.append((lmat - l_hi[c].astype(F32)).astype(BF16))
        egc.append(jnp.exp(gc[c]))
        rhs.append(jnp.concatenate([v_st[s] * beta[c], kf[s] * (beta[c] * egc[c])], axis=1))
        gl_rows = [jnp.broadcast_to(cols[c][last:last + 1, l:l + 1], (1, LANES)) for l in lanes]
        gl_ref[d, s * 8:(s + 1) * 8, :] = jnp.concatenate(gl_rows + [jnp.zeros((8 - H, LANES), F32)], axis=0)
    t_inv = _unit_tri_inverses(l_hi, row, col)
    y = [_dot(t, r.astype(BF16)) for t, r in zip(t_inv, rhs)]
    ys = [_split2(a) for a in y]
    ly = [_dot(l_hi[c], ys[c][0]) + (_dot(l_hi[c], ys[c][1]) + _dot(l_lo[c], ys[c][0])) for c in range(len(chains))]
    resid = [(rhs[c] - (y[c] + ly[c])).astype(BF16) for c in range(len(chains))]
    y = [y[c] + _dot(t_inv[c], resid[c]) for c in range(len(chains))]
    for c, (s, d) in enumerate(chains):
        r0 = s * CH
        attn = qk[s] * decay[c]
        a64 = jnp.zeros((CH, HC), F32)
        for h in range(H):
            a64 = jnp.where(lane_head == h, attn[h * CH:(h + 1) * CH, :], a64)
        attn_ref[d, r0:r0 + CH, :] = a64.astype(BF16)
        qg = (q_st[s].astype(F32) * egc[c]).astype(BF16)
        kg = (kf[s] * jnp.exp(g_last[c] - gc[c])).astype(BF16)
        for h in range(H):
            rs = slice(h * CH, (h + 1) * CH)
            ls = slice(h * DH, (h + 1) * DH)
            u_ref[d, r0:r0 + CH, ls] = y[c][rs, 0:DH]
            w_ref[d, r0:r0 + CH, ls] = y[c][rs, DH:2 * DH].astype(BF16)
            qg_ref[d, r0:r0 + CH, ls] = qg[rs, :]
            kg_ref[d, r0:r0 + CH, ls] = kg[rs, :]


def _gdn_wy(qkv, gates):
    spec = lambda w: pl.BlockSpec((2, TT, w), lambda i: (0, i, 0))
    shp = lambda w, dt: jax.ShapeDtypeStruct((2, N_TOK, w), dt)
    return pl.pallas_call(
        _gdn_wy_kernel,
        grid=(N_TILES,),
        in_specs=[pl.BlockSpec((TT, 3 * HW), lambda i: (i, 0)), pl.BlockSpec((TT, LANES), lambda i: (i, 0))],
        out_specs=[spec(HW), spec(HW), spec(HW), spec(HW), spec(HC),
                   pl.BlockSpec((2, 8 * (TT // CH), LANES), lambda i: (0, i, 0))],
        out_shape=[shp(HW, F32), shp(HW, BF16), shp(HW, BF16), shp(HW, BF16), shp(HC, BF16),
                   jax.ShapeDtypeStruct((2, 8 * N_CHUNKS, LANES), F32)],
        compiler_params=_cparams("parallel"),
    )(qkv, gates)


def _gdn_rec_body(ins, of_ref, ob_ref, s_scr):
    chains = [(d, h) for d in range(2) for h in range(H)]
    ls = [slice(h * DH, (h + 1) * DH) for h in range(H)]
    s = [s_scr[d, h] for d, h in chains]
    sb = [a.astype(BF16) for a in s]
    ws = [_dot(ins[6 * d + 1][0, :, ls[h]], sb[c]) for c, (d, h) in enumerate(chains)]
    inter = [_dot(ins[6 * d + 2][0, :, ls[h]], sb[c]) for c, (d, h) in enumerate(chains)]
    vb = [(ins[6 * d][0, :, ls[h]] - ws[c]).astype(BF16) for c, (d, h) in enumerate(chains)]
    intra = [_dot(ins[6 * d + 4][0, :, h * CH:(h + 1) * CH], vb[c]) for c, (d, h) in enumerate(chains)]
    upd = [_dot_tn(ins[6 * d + 3][0, :, ls[h]], vb[c]) for c, (d, h) in enumerate(chains)]
    for c, (d, h) in enumerate(chains):
        o_ref = of_ref if d == 0 else ob_ref
        o_ref[:, ls[h]] = inter[c] + intra[c]
        s_scr[d, h] = s[c] * jnp.exp(ins[6 * d + 5][0, h:h + 1, :]) + upd[c]


def _mlstm_body(qkv_refs, g_refs, o_refs, c_scr, n_scr, m_scr):
    scale = DH ** -0.5
    chains = [(d, h) for d in range(2) for h in range(H)]
    nch = range(len(chains))
    incl = [_scan_order_mask(d) for d in range(2)]
    g = [g_refs[d][...] for d in range(2)]
    cols = [_dot_exact_lhs(_ones_where(incl[d]), g[d]) for d in range(2)]
    g_t = [jnp.concatenate([g[d], cols[d]], axis=0).T for d in range(2)]
    q = [qkv_refs[d][:, h * DH:(h + 1) * DH] for d, h in chains]
    k = [qkv_refs[d][:, HW + h * DH:HW + (h + 1) * DH] for d, h in chains]
    v = [qkv_refs[d][:, 2 * HW + h * DH:2 * HW + (h + 1) * DH] for d, h in chains]
    qk = [_dot_nt(q[c], k[c]) * scale for c in nch]
    cst = [c_scr[d, h] for d, h in chains]
    qc = [_dot(q[c], cst[c].astype(BF16)) for c in nch]
    lf = [24 + d * H + h for d, h in chains]
    li = [16 + d * H + h for d, h in chains]
    dd = [d for d, _ in chains]
    last = [CH - 1 if d == 0 else 0 for d in dd]
    bc = [cols[dd[c]][:, lf[c]:lf[c] + 1] for c in nch]
    br = [g_t[dd[c]][lf[c]:lf[c] + 1, CH:2 * CH] for c in nch]
    ig_c = [g[dd[c]][:, li[c]:li[c] + 1] for c in nch]
    ig_r = [g_t[dd[c]][li[c]:li[c] + 1, 0:CH] for c in nch]
    dlog = [jnp.where(incl[dd[c]], bc[c] - br[c] + ig_r[c], -jnp.inf) for c in nch]
    dmax = [jnp.max(a, axis=1, keepdims=True) for a in dlog]
    b_last = [bc[c][last[c]:last[c] + 1, :] for c in nch]
    w_max = [jnp.max(b_last[c] - br[c] + ig_r[c], axis=1, keepdims=True) for c in nch]
    m = [m_scr[d, h:h + 1, 0:1] for d, h in chains]
    n = [n_scr[d, h:h + 1, :] for d, h in chains]
    m_inter = [bc[c] + m[c] for c in nch]
    m_row = [jnp.maximum(dmax[c], m_inter[c]) for c in nch]
    s = [qk[c] * jnp.exp(dlog[c] - m_row[c]) for c in nch]
    sv = [_dot(s[c].astype(BF16), v[c]) for c in nch]
    e_inter = [jnp.exp(m_inter[c] - m_row[c]) for c in nch]
    qn = [jnp.sum(q[c].astype(F32) * n[c], axis=1, keepdims=True) for c in nch]
    s_sum = [jnp.sum(a, axis=1, keepdims=True) for a in s]
    den = [e_inter[c] * qn[c] + s_sum[c] for c in nch]
    inv = [1.0 / jnp.maximum(jnp.abs(den[c]), jnp.exp(-m_row[c])) for c in nch]
    m_new = [jnp.maximum(b_last[c] + m[c], w_max[c]) for c in nch]
    f_c = [jnp.exp(b_last[c] + m[c] - m_new[c]) for c in nch]
    kw = [k[c].astype(F32) * (jnp.exp(b_last[c] - bc[c] + ig_c[c] - m_new[c]) * scale) for c in nch]
    kv = [_dot_tn(kw[c].astype(BF16), v[c]) for c in nch]
    for c, (d, h) in enumerate(chains):
        o_refs[d][:, h * DH:(h + 1) * DH] = (e_inter[c] * qc[c] + sv[c]) * inv[c]
        n_scr[d, h:h + 1, :] = f_c[c] * n[c] + jnp.sum(kw[c], axis=0, keepdims=True)
        m_scr[d, h:h + 1, :] = jnp.broadcast_to(m_new[c], (1, LANES))
        c_scr[d, h] = f_c[c] * cst[c] + kv[c]


def _rec_kernel(bf_ref, bb_ref, seq_ref, first_ref, last_ref, *refs):
    gdn_ins, (qkvf_ref, gf_ref, qkvb_ref, gb_ref, s0_ref, c0_ref, n0_ref, m0_ref) = refs[:12], refs[12:20]
    of_ref, ob_ref, hf_ref, hb_ref, sfin_ref, cfin_ref, nfin_ref, mfin_ref = refs[20:28]
    s_scr, c_scr, n_scr, m_scr = refs[28:]
    t = pl.program_id(0)

    @pl.when(first_ref[t] == 1)
    def _():
        s_scr[...] = s0_ref[0]
        c_scr[...] = c0_ref[0]
        n_scr[...] = n0_ref[0]
        m_scr[...] = m0_ref[0]

    _gdn_rec_body(gdn_ins, of_ref, ob_ref, s_scr)
    _mlstm_body((qkvf_ref, qkvb_ref), (gf_ref, gb_ref), (hf_ref, hb_ref), c_scr, n_scr, m_scr)

    @pl.when(last_ref[t] == 1)
    def _():
        sfin_ref[0] = s_scr[...]
        cfin_ref[0] = c_scr[...]
        nfin_ref[0] = n_scr[...]
        mfin_ref[0] = m_scr[...]


def _rec_scan(tables, wy, qkv, gates, s0, c0, n0, m0):
    fwd = lambda t, bf, bb, *_: (bf[t], 0)
    bwd = lambda t, bf, bb, *_: (bb[t], 0)

    def dir_specs(d):
        idx = (lambda t, bf, bb, *_: (0, bf[t], 0)) if d == 0 else (lambda t, bf, bb, *_: (1, bb[t], 0))
        return [pl.BlockSpec((1, CH, HW), idx)] * 4 + [pl.BlockSpec((1, CH, HC), idx),
                                                        pl.BlockSpec((1, 8, LANES), idx)]
    vec = pl.BlockSpec((1, 2, H, LANES), lambda t, bf, bb, sq, *_: (sq[t], 0, 0, 0))
    mat = pl.BlockSpec((1, 2, H, DH, DH), lambda t, bf, bb, sq, *_: (sq[t], 0, 0, 0, 0))
    tok = jax.ShapeDtypeStruct((N_TOK, HW), F32)
    mats = jax.ShapeDtypeStruct((N_SEQ, 2, H, DH, DH), F32)
    vecs = jax.ShapeDtypeStruct((N_SEQ, 2, H, LANES), F32)
    return pl.pallas_call(
        _rec_kernel,
        grid_spec=pltpu.PrefetchScalarGridSpec(
            num_scalar_prefetch=5, grid=(N_CHUNKS,),
            in_specs=dir_specs(0) + dir_specs(1)
            + [pl.BlockSpec((CH, 3 * HW), fwd), pl.BlockSpec((CH, LANES), fwd),
               pl.BlockSpec((CH, 3 * HW), bwd), pl.BlockSpec((CH, LANES), bwd), mat, mat, vec, vec],
            out_specs=[pl.BlockSpec((CH, HW), fwd), pl.BlockSpec((CH, HW), bwd),
                       pl.BlockSpec((CH, HW), fwd), pl.BlockSpec((CH, HW), bwd), mat, mat, vec, vec],
            scratch_shapes=[pltpu.VMEM((2, H, DH, DH), F32), pltpu.VMEM((2, H, DH, DH), F32),
                            pltpu.VMEM((2, H, LANES), F32), pltpu.VMEM((2, H, LANES), F32)]),
        out_shape=[tok, tok, tok, tok, mats, mats, vecs, vecs],
        compiler_params=_cparams("arbitrary"),
    )(*tables, *wy, *wy, qkv, gates, qkv, gates, s0, c0, n0, m0)


def _post_kernel(tbl_ref, x_ref, mod_ref, yfc_ref, yfl_ref, of_ref, ob_ref, hf_ref, hb_ref,
                 gz_ref, mo_ref, gates_ref, gn_ref, mn_ref, wbf_ref, wbg_ref, wbm_ref, wo_ref,
                 n2_ref, rwh_ref, rwl_ref, rb_ref,
                 xo_ref, h2_ref, info_ref, infot_ref, stats_ref, cnt_scr):
    i = pl.program_id(0)

    @pl.when(i == 0)
    def _():
        cnt_scr[...] = jnp.zeros_like(cnt_scr)

    yf = jnp.where(i < CTX_TILES, yfc_ref[...], yfl_ref[...])
    og = of_ref[...] + ob_ref[...]
    hm = hf_ref[...] + hb_ref[...]
    og_parts, hm_parts = [], []
    for h in range(H):
        sl = slice(h * DH, (h + 1) * DH)
        og_parts.append(_rms(og[:, sl]) * gn_ref[...] * _silu(gz_ref[:, sl].astype(F32)))
        hm_parts.append(_rms(hm[:, sl]) * mn_ref[:, sl] * _sigmoid(mo_ref[:, sl].astype(F32)))
    ogb = jnp.concatenate(og_parts, axis=1).astype(BF16)
    hmb = jnp.concatenate(hm_parts, axis=1).astype(BF16)
    y = (_sigmoid(gates_ref[:, 0:D].astype(F32)) * _dot(yf, wbf_ref[...])
         + _sigmoid(gates_ref[:, D:2 * D].astype(F32)) * _dot(ogb, wbg_ref[...])
         + _sigmoid(gates_ref[:, 2 * D:3 * D].astype(F32)) * _dot(hmb, wbm_ref[...]))
    x = x_ref[...] + mod_ref[0, 2:3, :] * _dot(y.astype(BF16), wo_ref[...])
    xo_ref[...] = x
    h2 = _rms(x) * n2_ref[...]
    h2 = h2 * (1.0 + mod_ref[0, 4:5, :]) + mod_ref[0, 3:4, :]
    hb2, hl2 = _split2(h2)
    h2_ref[...] = hb2
    logits = (_dot(hb2, rwh_ref[...]) + (_dot(hb2, rwl_ref[...]) + _dot(hl2, rwh_ref[...]))
              + rb_ref[...])
    lane = lax.broadcasted_iota(jnp.int32, logits.shape, 1).astype(F32)
    work = jnp.where(lane < N_EXP, logits, -jnp.inf)
    sel = jnp.zeros(logits.shape, F32)
    hits, exps = [], []
    top0 = None
    denom = None
    for kk in range(TOP_K):
        mx = jnp.max(work, axis=1, keepdims=True)
        first = jnp.min(jnp.where(work == mx, lane, float(LANES)), axis=1, keepdims=True)
        hit = lane == first
        if kk == 0:
            top0 = mx
        e = jnp.exp(mx - top0)
        denom = e if kk == 0 else denom + e
        hits.append(hit)
        exps.append(e)
        sel = jnp.where(hit, 1.0, sel)
        work = jnp.where(hit, -jnp.inf, work)
    r_i = lax.broadcasted_iota(jnp.int32, (TT, TT), 0)
    c_i = lax.broadcasted_iota(jnp.int32, (TT, TT), 1)
    within = _dot(_ones_where(c_i < r_i), sel.astype(BF16))
    tile_cnt = jnp.sum(sel, axis=0, keepdims=True)
    a_i = lax.broadcasted_iota(jnp.int32, (LANES, LANES), 0)
    b_i = lax.broadcasted_iota(jnp.int32, (LANES, LANES), 1)
    tile_off = _dot(jnp.broadcast_to(tile_cnt, (8, LANES)).astype(BF16), _ones_where(a_i < b_i))[0:1, :]
    local = tile_off + within
    info = jnp.zeros(logits.shape, F32)
    for kk in range(TOP_K):
        pos = jnp.sum(jnp.where(hits[kk], local, 0.0), axis=1, keepdims=True)
        info = jnp.where(lane == float(kk), pos, info)
        info = jnp.where(lane == float(TOP_K + kk), exps[kk] / denom, info)
    info_ref[...] = info
    infot_ref[...] = info.T[0:8, :]
    stats_ref[0] = jnp.concatenate([cnt_scr[0:1, :], tile_cnt, tile_off, jnp.zeros((5, LANES), F32)], axis=0)
    cnt_scr[0:1, :] = cnt_scr[0:1, :] + tile_cnt


def _post(cond_tbl, x, mod, yfc, yfl, o_f, o_b, h_f, h_b, gz, mo, gates, gn, mn, wbf, wbg, wbm, wo,
          n2, rwh, rwl, rb):
    tile = lambda w: pl.BlockSpec((TT, w), lambda i, t: (i, 0))
    const = lambda shape: pl.BlockSpec(shape, lambda i, t: (0,) * len(shape))
    in_specs = [tile(D), pl.BlockSpec((1, N_MOD, D), lambda i, t: (t[i], 0, 0)),
                pl.BlockSpec((TT, HW), lambda i, t: (jnp.minimum(i, CTX_TILES - 1), 0)),
                pl.BlockSpec((TT, HW), lambda i, t: (jnp.maximum(i - CTX_TILES, 0), 0)),
                tile(HW), tile(HW), tile(HW), tile(HW),
                tile(HW), tile(HW), tile(3 * D),
                const((1, DH)), const((1, HW)), const((HW, D)), const((HW, D)), const((HW, D)),
                const((D, D)), const((1, D)), const((D, LANES)), const((D, LANES)), const((1, LANES))]
    return pl.pallas_call(
        _post_kernel,
        grid_spec=pltpu.PrefetchScalarGridSpec(
            num_scalar_prefetch=1, grid=(N_TILES,), in_specs=in_specs,
            out_specs=[tile(D), tile(D), tile(LANES), pl.BlockSpec((8, TT), lambda i, t: (0, i)),
                       pl.BlockSpec((1, 8, LANES), lambda i, t: (i, 0, 0))],
            scratch_shapes=[pltpu.VMEM((8, LANES), F32)]),
        out_shape=[jax.ShapeDtypeStruct((N_TOK, D), F32), jax.ShapeDtypeStruct((N_TOK, D), BF16),
                   jax.ShapeDtypeStruct((N_TOK, LANES), F32), jax.ShapeDtypeStruct((8, N_TOK), F32),
                   jax.ShapeDtypeStruct((N_TILES, 8, LANES), F32)],
        compiler_params=_cparams("arbitrary"),
    )(cond_tbl, x, mod, yfc, yfl, o_f, o_b, h_f, h_b, gz, mo, gates, gn, mn, wbf, wbg, wbm, wo,
      n2, rwh, rwl, rb)


PAIRS = TT * TOP_K
RUN_BITS = tuple(1 << b for b in range(8, -1, -1))


def _slot_rows(start_slot, n_slots):
    return pl.ds(pl.multiple_of(start_slot * 8, 8), n_slots * 8)


def _for_each_run(base, cnt_ref, fn):
    def body(e, carry):
        cnt = cnt_ref[base + e]
        for bit in RUN_BITS:
            @pl.when((cnt & bit) != 0)
            def _():
                fn(base + e, cnt & (-2 * bit), bit)
        return carry
    lax.fori_loop(0, N_EXP, body, 0)


def _zero_fill(ps_ref, pc_ref, tail_ref, xs_hbm, zbuf, zsem, start):
    def piece(dst_slot, n):
        cp = pltpu.make_async_copy(zbuf.at[pl.ds(0, n * 8)], xs_hbm.at[_slot_rows(dst_slot, n)], zsem)
        if start:
            cp.start()
        else:
            cp.wait()

    def pad(e, carry):
        cnt = pc_ref[e]
        for bit in RUN_BITS[1:]:
            @pl.when((cnt & bit) != 0)
            def _():
                piece(ps_ref[e] + (cnt & (-2 * bit)), bit)
        return carry

    def tail(t, carry):
        piece(tail_ref[0] + t * TM, TM)
        return carry

    lax.fori_loop(0, N_EXP, pad, 0)
    lax.fori_loop(0, tail_ref[1], tail, 0)


def _dispatch_kernel(ss_ref, cnt_ref, toff_ref, ps_ref, pc_ref, tail_ref, h2_ref, infot_ref, xs_hbm,
                     stage, zbuf, sem, zsem):
    i = pl.program_id(0)
    buf = i % 2
    whole = lambda b: pltpu.make_async_copy(stage.at[b], xs_hbm.at[pl.ds(0, PAIRS * 8)], sem.at[b])

    @pl.when(i >= 2)
    def _():
        whole(buf).wait()

    r = lax.broadcasted_iota(jnp.int32, (PAIRS, TT), 0).astype(F32)
    pick = r == infot_ref[0:1, :]
    for kk in range(1, TOP_K):
        pick = jnp.logical_or(pick, r == infot_ref[kk:kk + 1, :])
    rows = _dot(_ones_where(pick), h2_ref[...])
    for cc in range(8):
        stage[buf, pl.ds(cc, PAIRS, stride=8), :] = rows[:, cc * LANES:(cc + 1) * LANES]

    def send(j, o, n):
        pltpu.make_async_copy(stage.at[buf, _slot_rows(toff_ref[j] + o, n)],
                              xs_hbm.at[_slot_rows(ss_ref[j] + o, n)], sem.at[buf]).start()

    _for_each_run(i * N_EXP, cnt_ref, send)

    @pl.when(i == 0)
    def _():
        zbuf[...] = jnp.zeros_like(zbuf)
        _zero_fill(ps_ref, pc_ref, tail_ref, xs_hbm, zbuf, zsem, True)

    @pl.when(i == N_TILES - 1)
    def _():
        whole(1 - buf).wait()
        whole(buf).wait()
        _zero_fill(ps_ref, pc_ref, tail_ref, xs_hbm, zbuf, zsem, False)


def _dispatch(slot_start, seg_cnt, seg_off, pad_start, pad_cnt, tail, h2, info_t):
    return pl.pallas_call(
        _dispatch_kernel,
        grid_spec=pltpu.PrefetchScalarGridSpec(
            num_scalar_prefetch=6, grid=(N_TILES,),
            in_specs=[pl.BlockSpec((TT, D), lambda i, *_: (i, 0)), pl.BlockSpec((8, TT), lambda i, *_: (0, i))],
            out_specs=pl.BlockSpec(memory_space=pl.ANY),
            scratch_shapes=[pltpu.VMEM((2, PAIRS * 8, LANES), F32), pltpu.VMEM((TM * 8, LANES), F32),
                            pltpu.SemaphoreType.DMA((2,)), pltpu.SemaphoreType.DMA(())]),
        out_shape=jax.ShapeDtypeStruct((S_MAX * 8, LANES), F32),
        compiler_params=_cparams("arbitrary"),
    )(slot_start, seg_cnt, seg_off, pad_start, pad_cnt, tail, h2, info_t)


def _gmm_kernel(te_ref, nu_ref, valid_ref, x_ref, wg_ref, bg_ref, wu_ref, bu_ref, wd_ref, bd_ref, o_ref,
                wg_scr, wu_scr, wd_scr):
    i = pl.program_id(0)
    used = i < nu_ref[0]

    @pl.when(used)
    def _():
        e = te_ref[i]
        e_prev = te_ref[jnp.maximum(i - 1, 0)]

        @pl.when(jnp.logical_or(i == 0, e != e_prev))
        def _():
            wg_scr[...] = wg_ref[0, 0].astype(BF16)
            wu_scr[...] = wu_ref[0, 0].astype(BF16)
            wd_scr[...] = wd_ref[0, 0].astype(BF16)

        x = jnp.concatenate([x_ref[pl.ds(cc, TM, stride=8), :] for cc in range(8)], axis=1)
        row = lax.broadcasted_iota(jnp.int32, (TM, 1), 0)
        x = jnp.where(row < valid_ref[i], x, 0.0).astype(BF16)
        acts = []
        for j in range(D // FF_CHUNK):
            cs = slice(j * FF_CHUNK, (j + 1) * FF_CHUNK)
            gate = jnp.minimum(_dot(x, wg_scr[:, cs]) + bg_ref[0, 0, :, cs], SWIGLU_LIMIT)
            up = jnp.clip(_dot(x, wu_scr[:, cs]) + bu_ref[0, 0, :, cs], -SWIGLU_LIMIT, SWIGLU_LIMIT)
            acts.append(((up + 1.0) * gate * _sigmoid(SWIGLU_ALPHA * gate)).astype(BF16))
        y = _dot(jnp.concatenate(acts, axis=1), wd_scr[...]) + bd_ref[0, 0]
        for cc in range(8):
            o_ref[pl.ds(cc, TM, stride=8), :] = y[:, cc * LANES:(cc + 1) * LANES]

    @pl.when(jnp.logical_not(used))
    def _():
        o_ref[...] = jnp.zeros_like(o_ref)


def _gmm(layer, tile_exp, n_used, valid, xs, wg, bg, wu, bu, wd, bd):
    wspec = pl.BlockSpec((1, 1, D, D), lambda i, te, nu, va: (layer, te[i], 0, 0))
    bspec = pl.BlockSpec((1, 1, 1, D), lambda i, te, nu, va: (layer, te[i], 0, 0))
    return pl.pallas_call(
        _gmm_kernel,
        grid_spec=pltpu.PrefetchScalarGridSpec(
            num_scalar_prefetch=3, grid=(NT_MAX,),
            in_specs=[pl.BlockSpec((TM * 8, LANES), lambda i, te, nu, va: (jnp.minimum(i, nu[0] - 1), 0)),
                      wspec, bspec, wspec, bspec, wspec, bspec],
            out_specs=pl.BlockSpec((TM * 8, LANES), lambda i, te, nu, va: (i, 0)),
            scratch_shapes=[pltpu.VMEM((D, D), BF16)] * 3),
        out_shape=jax.ShapeDtypeStruct((S_MAX * 8, LANES), F32),
        compiler_params=_cparams("arbitrary"),
    )(tile_exp, n_used, valid, xs, wg, bg.reshape(DEPTH, N_EXP, 1, D), wu, bu.reshape(DEPTH, N_EXP, 1, D),
      wd, bd.reshape(DEPTH, N_EXP, 1, D))


def _combine_kernel(tbl_ref, ss_ref, cnt_ref, toff_ref, y_hbm, info_ref, x_ref, mod_ref, g_ref, *refs,
                    final):
    outs, (ybuf, sem) = refs[:-2], refs[-2:]
    i = pl.program_id(0)
    buf = i % 2

    def fetch(tile, b):
        def recv(j, o, n):
            pltpu.make_async_copy(y_hbm.at[_slot_rows(ss_ref[j] + o, n)],
                                  ybuf.at[b, _slot_rows(toff_ref[j] + o, n)], sem.at[b]).start()
        _for_each_run(tile * N_EXP, cnt_ref, recv)

    @pl.when(i == 0)
    def _():
        fetch(0, 0)

    @pl.when(i + 1 < N_TILES)
    def _():
        fetch(i + 1, 1 - buf)

    r = lax.broadcasted_iota(jnp.int32, (TT, PAIRS), 1).astype(F32)
    wm = jnp.zeros((TT, PAIRS), F32)
    for kk in range(TOP_K):
        wm = jnp.where(r == info_ref[:, kk:kk + 1], info_ref[:, TOP_K + kk:TOP_K + kk + 1], wm)
    pltpu.make_async_copy(y_hbm.at[pl.ds(0, PAIRS * 8)], ybuf.at[buf], sem.at[buf]).wait()
    y = jnp.concatenate([ybuf[buf, pl.ds(cc, PAIRS, stride=8), :] for cc in range(8)], axis=1)
    x = x_ref[...] + mod_ref[0, 5:6, :] * _dot(wm.astype(BF16), y.astype(BF16))
    if not final:
        outs[0][...] = x
    else:
        res = _rms(x) * g_ref[...]

        @pl.when(i < CTX_TILES)
        def _():
            outs[0][...] = res

        @pl.when(i >= CTX_TILES)
        def _():
            outs[1][...] = res


def _combine(final, cond_tbl, slot_start, seg_cnt, seg_off, y_sorted, info, x, mod, g):
    tile = lambda w: pl.BlockSpec((TT, w), lambda i, *_: (i, 0))
    if final:
        out_specs = [pl.BlockSpec((TT, D), lambda i, *_: (jnp.minimum(i, CTX_TILES - 1), 0)),
                     pl.BlockSpec((TT, D), lambda i, *_: (jnp.maximum(i - CTX_TILES, 0), 0))]
        out_shape = [jax.ShapeDtypeStruct((N_CTX, D), F32), jax.ShapeDtypeStruct((N_LAT, D), F32)]
    else:
        out_specs = [tile(D)]
        out_shape = [jax.ShapeDtypeStruct((N_TOK, D), F32)]
    return pl.pallas_call(
        functools.partial(_combine_kernel, final=final),
        grid_spec=pltpu.PrefetchScalarGridSpec(
            num_scalar_prefetch=4, grid=(N_TILES,),
            in_specs=[pl.BlockSpec(memory_space=pl.ANY), tile(LANES), tile(D),
                      pl.BlockSpec((1, N_MOD, D), lambda i, t, *_: (t[i], 0, 0)),
                      pl.BlockSpec((1, D), lambda i, *_: (0, 0))],
            out_specs=out_specs,
            scratch_shapes=[pltpu.VMEM((2, PAIRS * 8, LANES), F32), pltpu.SemaphoreType.DMA((2,))]),
        out_shape=out_shape,
        compiler_params=_cparams("arbitrary"),
    )(cond_tbl, slot_start, seg_cnt, seg_off, y_sorted, info, x, mod, g)


def _dft_consts(t_len):
    k = np.arange(t_len, dtype=np.int64)
    ang = 2.0 * np.pi * ((k[:, None] * k[None, :]) % t_len).astype(np.float64) / t_len
    return np.concatenate([np.cos(ang), -np.sin(ang)], axis=1).astype(np.float32)


def _pos_table():
    quarter = D // 4
    omega = 1.0 / (10000.0 ** (np.arange(quarter, dtype=np.float32) / np.float32(quarter)))
    omega = omega.astype(np.float32).astype(np.float64)
    t = np.arange(T_LAT)
    ang_r = (t // GRID_W).astype(np.float64)[:, None] * omega
    ang_c = (t % GRID_W).astype(np.float64)[:, None] * omega
    return np.concatenate([np.sin(ang_r), np.cos(ang_r), np.sin(ang_c), np.cos(ang_c)],
                          axis=-1).astype(np.float32)


def _repack_w_in(w):
    f, gq, gk, gv, gz, ga, gb, mq, mk, mv, mo, mi, mf, g1, g2, g3 = jnp.split(
        w, np.cumsum([512, 512, 512, 512, 512, 8, 8, 512, 512, 512, 512, 8, 8, 1024, 1024])[:].tolist(),
        axis=1)
    main = jnp.concatenate([f, gq, gk, gv, gz, mq, mk, mv, mo, g1, g2, g3], axis=1).astype(BF16)
    small = jnp.concatenate([ga, gb, mi, mf, jnp.zeros((D, LANES - 32), F32)], axis=1)
    hi = small.astype(BF16)
    lo = (small - hi.astype(F32)).astype(BF16)
    return main, hi, lo


def _route_glue(stats):
    seg_base = stats[:, 0, :N_EXP].astype(jnp.int32)
    seg_cnt = stats[:, 1, :N_EXP].astype(jnp.int32)
    seg_off = stats[:, 2, :N_EXP].astype(jnp.int32)
    counts = seg_base[-1] + seg_cnt[-1]
    region = ((counts + TM - 1) // TM) * TM
    ends = jnp.cumsum(region)
    off = ends - region
    n_used = ends[-1] // TM
    tile_start = jnp.arange(NT_MAX, dtype=jnp.int32) * TM
    tile_exp = jnp.minimum(jnp.sum((tile_start[:, None] >= ends[None, :]).astype(jnp.int32), axis=1), N_EXP - 1)
    used = tile_start < ends[-1]
    tile_exp = jnp.where(used, tile_exp, jnp.max(jnp.where(used, tile_exp, 0)))
    run_end = jnp.sum(jnp.where(tile_exp[:, None] == jnp.arange(N_EXP)[None, :], (off + counts)[None, :], 0), axis=1)
    valid = jnp.clip(run_end - tile_start, 0, TM)
    slot_start = off[None, :] + seg_base
    tail = jnp.stack([ends[-1], (S_MAX - ends[-1]) // TM]).astype(jnp.int32)
    return (tile_exp.astype(jnp.int32), n_used.reshape(1).astype(jnp.int32), valid.astype(jnp.int32),
            slot_start.reshape(-1), seg_cnt.reshape(-1), seg_off.reshape(-1),
            (off + counts).astype(jnp.int32), (region - counts).astype(jnp.int32), tail)


def kernel(x_prompt, x_sample, state_gdn, state_mlstm_c, state_mlstm_n, state_mlstm_m, c, c_ctx,
           w_ada, b_ada, norm1_g, norm2_g, w_in, gdn_conv_w, gdn_a_log, gdn_dt_bias, gdn_norm_g,
           mlstm_i_bias, mlstm_f_bias, mlstm_norm_g, w_branch_f, w_branch_g, w_branch_m, w_out,
           router_w, router_b, exp_w_gate, exp_b_gate, exp_w_up, exp_b_up, exp_w_down, exp_b_down,
           final_norm_g):
    cond_tbl = jnp.asarray(np.concatenate([np.zeros(CTX_TILES, np.int32),
                                           1 + np.arange(N_TILES - CTX_TILES, dtype=np.int32) // LAT_TILES_PER_SEQ]))
    tables = tuple(jnp.asarray(t) for t in _scan_tables())
    cond8 = jnp.concatenate([c_ctx[None, :], c, jnp.zeros((8 - 1 - B_LAT, D), F32)], axis=0)
    mod_all = _ada(cond8, w_ada, b_ada).reshape(DEPTH, 8, N_MOD, D)

    pos = jnp.asarray(_pos_table())
    dft_c = jnp.asarray(_dft_consts(T_CTX)).astype(BF16)
    dft_l = jnp.asarray(_dft_consts(T_LAT)).astype(BF16)
    ang = 2.0 * np.pi * ((np.arange(DH)[:, None] * np.arange(DH)[None, :]) % DH) / DH
    c128 = jnp.asarray(np.cos(ang).astype(np.float32)).astype(BF16)
    s128 = jnp.asarray(np.sin(ang).astype(np.float32)).astype(BF16)

    zeros_s = jnp.zeros((B_CTX, 2, H, DH, DH), F32)
    zeros_v = jnp.zeros((B_CTX, 2, H, LANES), F32)
    x = None
    gdn_states, c_states, n_states, m_states = [], [], [], []
    y_prompt = y_sample = None
    for l in range(DEPTH):
        mod = mod_all[l]
        wm, wsh, wsl = _repack_w_in(w_in[l])
        if l == 0:
            outs = _pre(True, (x_prompt.reshape(N_CTX, D), x_sample.reshape(N_LAT, D), pos), mod,
                        norm1_g[l][None, :], wm, wsh, wsl, cond_tbl)
            x, outs = outs[0], outs[1:]
        else:
            outs = _pre(False, x, mod, norm1_g[l][None, :], wm, wsh, wsl, cond_tbl)
        f_all, gqkv, gz, mqkv, mo, gates, sm = outs

        yfc = _fourier(f_all, dft_c, c128, s128, B_CTX, T_CTX, 0)
        yfl = _fourier(f_all, dft_l, c128, s128, B_LAT, T_LAT, N_CTX)

        prm = jnp.zeros((8, LANES), F32)
        prm = prm.at[0, 0:8].set(gdn_a_log[l].reshape(-1))
        prm = prm.at[1, 0:8].set(gdn_dt_bias[l].reshape(-1))
        prm = prm.at[1, 16:24].set(mlstm_i_bias[l].reshape(-1))
        prm = prm.at[1, 24:32].set(mlstm_f_bias[l].reshape(-1))
        qkv_c, gts = _prep2(gqkv, sm, gdn_conv_w[l], prm)

        s0 = jnp.concatenate([zeros_s, state_gdn[:, l]], axis=0)
        c0 = jnp.concatenate([zeros_s, state_mlstm_c[:, l]], axis=0)
        n0 = jnp.concatenate([zeros_v, state_mlstm_n[:, l]], axis=0)
        m0 = jnp.concatenate([zeros_v, jnp.broadcast_to(state_mlstm_m[:, l][..., None],
                                                         (B_LAT, 2, H, LANES))], axis=0)
        o_f, o_b, h_f, h_b, s_fin, c_fin, n_fin, m_fin = _rec_scan(
            tables, _gdn_wy(qkv_c, gts), mqkv, gts, s0, c0, n0, m0)
        gdn_states.append(s_fin[:B_CTX])
        c_states.append(c_fin[:B_CTX])
        n_states.append(n_fin[:B_CTX])
        m_states.append(m_fin[:B_CTX, :, :, 0])

        rw = jnp.concatenate([router_w[l], jnp.zeros((D, LANES - N_EXP), F32)], axis=1)
        rwh = rw.astype(BF16)
        rwl = (rw - rwh.astype(F32)).astype(BF16)
        rb = jnp.concatenate([router_b[l], jnp.zeros((LANES - N_EXP,), F32)])[None, :]
        x, h2, info, info_t, stats = _post(
            cond_tbl, x, mod, yfc, yfl, o_f, o_b, h_f, h_b, gz, mo, gates,
            gdn_norm_g[l][None, :], mlstm_norm_g[l][None, :],
            w_branch_f[l].astype(BF16), w_branch_g[l].astype(BF16), w_branch_m[l].astype(BF16),
            w_out[l].astype(BF16), norm2_g[l][None, :], rwh, rwl, rb)

        tile_exp, n_used, valid, slot_start, seg_cnt, seg_off, pad_start, pad_cnt, tail = _route_glue(stats)
        xs_sorted = _dispatch(slot_start, seg_cnt, seg_off, pad_start, pad_cnt, tail, h2, info_t)
        y_sorted = _gmm(l, tile_exp, n_used, valid, xs_sorted, exp_w_gate, exp_b_gate, exp_w_up,
                        exp_b_up, exp_w_down, exp_b_down)
        res = _combine(l + 1 == DEPTH, cond_tbl, slot_start, seg_cnt, seg_off, y_sorted, info, x, mod,
                       final_norm_g[None, :])
        if l + 1 < DEPTH:
            x = res[0]
        else:
            y_prompt, y_sample = res

    return (y_prompt.reshape(B_CTX, T_CTX, D), y_sample.reshape(B_LAT, T_LAT, D),
            jnp.stack(gdn_states, axis=1), jnp.stack(c_states, axis=1),
            jnp.stack(n_states, axis=1), jnp.stack(m_states, axis=1))
```

```python
import functools
import math

import numpy as np
import jax
import jax.numpy as jnp
from jax import lax
from jax.experimental import pallas as pl
from jax.experimental.pallas import tpu as pltpu

F32 = jnp.float32
BF16 = jnp.bfloat16

D = 1024
DEPTH = 2
B_CTX, T_CTX = 16, 256
B_LAT, T_LAT = 4, 2048
N_CTX = B_CTX * T_CTX
N_LAT = B_LAT * T_LAT
N_TOK = N_CTX + N_LAT
N_SEQ = B_CTX + B_LAT
GRID_W = 64
H = 4
DH = 128
HW = H * DH
CH = 64
TT = 256
N_TILES = N_TOK // TT
CTX_TILES = N_CTX // TT
LAT_TILES_PER_SEQ = T_LAT // TT
N_CHUNKS = N_TOK // CH
N_EXP = 32
TOP_K = 4
N_MOD = 6
EPS = 1e-6
SWIGLU_ALPHA = 1.702
SWIGLU_LIMIT = 7.0
LANES = 128
TM = 256
FF_CHUNK = 256
S_MAX = N_TOK * TOP_K + N_EXP * (TM - 1)
S_MAX = ((S_MAX + TM - 1) // TM) * TM
NT_MAX = S_MAX // TM

W_MAIN = 512 + 1536 + 512 + 1536 + 512 + 3072


def _dot(a, b):
    return jnp.dot(a, b, preferred_element_type=F32)


def _dot_nt(a, b):
    return lax.dot_general(a, b, (((1,), (1,)), ((), ())), preferred_element_type=F32)


def _dot_tn(a, b):
    return lax.dot_general(a, b, (((0,), (0,)), ((), ())), preferred_element_type=F32)


def _split2(a):
    hi = a.astype(BF16)
    lo = (a - hi.astype(F32)).astype(BF16)
    return hi, lo


def _split3(a):
    hi = a.astype(BF16)
    r = a - hi.astype(F32)
    mid = r.astype(BF16)
    lo = (r - mid.astype(F32)).astype(BF16)
    return hi, mid, lo


def _dot3(a, b):
    ah, al = _split2(a)
    bh, bl = _split2(b)
    return _dot(ah, bh) + (_dot(ah, bl) + _dot(al, bh))


def _dot_exact_lhs(a_bf16, b):
    bh, bm, bl = _split3(b)
    return _dot(a_bf16, bh) + (_dot(a_bf16, bm) + _dot(a_bf16, bl))


def _sigmoid(x):
    return 0.5 * jnp.tanh(0.5 * x) + 0.5


def _silu(x):
    return x * _sigmoid(x)


def _softplus(x):
    return jnp.maximum(x, 0.0) + jnp.log(1.0 + jnp.exp(-jnp.abs(x)))


def _rms(x):
    return x * lax.rsqrt(jnp.mean(x * x, axis=-1, keepdims=True) + EPS)


def _cparams(*sem):
    return pltpu.CompilerParams(dimension_semantics=tuple(sem))


def _ada_kernel(c_ref, w_ref, b_ref, o_ref):
    o_ref[0] = _dot3(_silu(c_ref[...]), w_ref[0]) + b_ref[0]


def _ada(cond8, w_ada, b_ada):
    nb = 1536
    return pl.pallas_call(
        _ada_kernel,
        grid=(DEPTH, N_MOD * D // nb),
        in_specs=[pl.BlockSpec((8, D), lambda l, j: (0, 0)),
                  pl.BlockSpec((1, D, nb), lambda l, j: (l, 0, j)),
                  pl.BlockSpec((1, 1, nb), lambda l, j: (l, 0, j))],
        out_specs=pl.BlockSpec((1, 8, nb), lambda l, j: (l, 0, j)),
        out_shape=jax.ShapeDtypeStruct((DEPTH, 8, N_MOD * D), F32),
        compiler_params=_cparams("parallel", "parallel"),
    )(cond8, w_ada, b_ada.reshape(DEPTH, 1, N_MOD * D))


def _pre_body(x, mod_ref, n1_ref, wm_ref, wsh_ref, wsl_ref, outs):
    f_ref, gqkv_ref, gz_ref, mqkv_ref, mo_ref, gates_ref, sm_ref = outs
    h = _rms(x) * n1_ref[...]
    h = h * (1.0 + mod_ref[0, 1:2, :]) + mod_ref[0, 0:1, :]
    hb, hl = _split2(h)
    off = 0
    for ref, width in ((f_ref, 512), (gqkv_ref, 1536), (gz_ref, 512), (mqkv_ref, 1536),
                       (mo_ref, 512), (gates_ref, 3072)):
        ref[...] = _dot(hb, wm_ref[:, off:off + width]).astype(BF16)
        off += width
    sm_ref[...] = _dot(hb, wsh_ref[...]) + (_dot(hb, wsl_ref[...]) + _dot(hl, wsh_ref[...]))


def _pre_first_kernel(tbl_ref, xp_ref, xs_ref, pos_ref, mod_ref, n1_ref, wm_ref, wsh_ref, wsl_ref,
                      x_out_ref, *outs):
    i = pl.program_id(0)
    x = jnp.where(i < CTX_TILES, xp_ref[...], xs_ref[...] + pos_ref[...])
    x_out_ref[...] = x
    _pre_body(x, mod_ref, n1_ref, wm_ref, wsh_ref, wsl_ref, outs)


def _pre_next_kernel(tbl_ref, x_ref, mod_ref, n1_ref, wm_ref, wsh_ref, wsl_ref, *outs):
    _pre_body(x_ref[...], mod_ref, n1_ref, wm_ref, wsh_ref, wsl_ref, outs)


_PRE_OUT_WIDTHS = ((512, BF16), (1536, BF16), (512, BF16), (1536, BF16), (512, BF16), (3072, BF16),
                   (LANES, F32))


def _pre(first, xs_in, mod, n1, wm, wsh, wsl, cond_tbl):
    tile = lambda w: pl.BlockSpec((TT, w), lambda i, t: (i, 0))
    const = lambda shape: pl.BlockSpec(shape, lambda i, t: (0,) * len(shape))
    w_specs = [pl.BlockSpec((1, N_MOD, D), lambda i, t: (t[i], 0, 0)), const((1, D)),
               const((D, W_MAIN)), const((D, LANES)), const((D, LANES))]
    out_specs = [tile(w) for w, _ in _PRE_OUT_WIDTHS]
    out_shape = [jax.ShapeDtypeStruct((N_TOK, w), dt) for w, dt in _PRE_OUT_WIDTHS]
    if first:
        x_prompt2, x_sample2, pos = xs_in
        in_specs = [pl.BlockSpec((TT, D), lambda i, t: (jnp.minimum(i, CTX_TILES - 1), 0)),
                    pl.BlockSpec((TT, D), lambda i, t: (jnp.maximum(i - CTX_TILES, 0), 0)),
                    pl.BlockSpec((TT, D), lambda i, t: (jnp.maximum(i - CTX_TILES, 0) % LAT_TILES_PER_SEQ, 0))]
        kern = _pre_first_kernel
        out_specs = [tile(D)] + out_specs
        out_shape = [jax.ShapeDtypeStruct((N_TOK, D), F32)] + out_shape
        args = (x_prompt2, x_sample2, pos)
    else:
        in_specs = [tile(D)]
        kern = _pre_next_kernel
        args = (xs_in,)
    return pl.pallas_call(
        kern,
        grid_spec=pltpu.PrefetchScalarGridSpec(
            num_scalar_prefetch=1, grid=(N_TILES,), in_specs=in_specs + w_specs, out_specs=out_specs),
        out_shape=out_shape,
        compiler_params=_cparams("parallel"),
    )(cond_tbl, *args, mod, n1, wm, wsh, wsl)


def _fourier_kernel(x_ref, dft_ref, c_ref, s_ref, o_ref, z_scr, *, t_len, scale):
    @pl.when(pl.program_id(1) == 0)
    def _():
        for g in range(H):
            xg = x_ref[:, g * DH:(g + 1) * DH]
            z_scr[0:t_len, g * DH:(g + 1) * DH] = _dot(xg, c_ref[...]).astype(BF16)
            z_scr[t_len:2 * t_len, g * DH:(g + 1) * DH] = _dot(xg, s_ref[...]).astype(BF16)

    o_ref[...] = (_dot(dft_ref[...], z_scr[...]) * scale).astype(BF16)


def _fourier(f_all, dft, c128, s128, n_seq, t_len, row0):
    tr = min(t_len, 512)
    blk0 = row0 // t_len
    return pl.pallas_call(
        functools.partial(_fourier_kernel, t_len=t_len, scale=1.0 / math.sqrt(t_len * DH)),
        grid=(n_seq, t_len // tr),
        in_specs=[pl.BlockSpec((t_len, HW), lambda b, r: (blk0 + b, 0)),
                  pl.BlockSpec((tr, 2 * t_len), lambda b, r: (r, 0)),
                  pl.BlockSpec((DH, DH), lambda b, r: (0, 0)),
                  pl.BlockSpec((DH, DH), lambda b, r: (0, 0))],
        out_specs=pl.BlockSpec((tr, HW), lambda b, r: (b * (t_len // tr) + r, 0)),
        out_shape=jax.ShapeDtypeStruct((n_seq * t_len, HW), BF16),
        scratch_shapes=[pltpu.VMEM((2 * t_len, HW), BF16)],
        compiler_params=_cparams("parallel", "arbitrary"),
    )(f_all, dft, c128, s128)


HALO = 16


def _prep2_kernel(cur_ref, prev_ref, next_ref, sm_ref, cw_ref, prm_ref, qkv_ref, g_ref):
    i = pl.program_id(0)
    j = jnp.maximum(i - CTX_TILES, 0) % LAT_TILES_PER_SEQ
    is_lat = i >= CTX_TILES
    has_prev = jnp.logical_and(is_lat, j > 0)
    has_next = jnp.logical_and(is_lat, j < LAT_TILES_PER_SEQ - 1)
    x = cur_ref[...].astype(F32)
    row = lax.broadcasted_iota(jnp.int32, x.shape, 0)
    prev_row = jnp.where(has_prev, prev_ref[HALO - 1:HALO, :].astype(F32), 0.0)
    next_row = jnp.where(has_next, next_ref[0:1, :].astype(F32), 0.0)
    xp = jnp.where(row == 0, prev_row, pltpu.roll(x, 1, 0))
    xn = jnp.where(row == TT - 1, next_row, pltpu.roll(x, TT - 1, 0))
    y = _silu(cw_ref[0:1, :] * xp + cw_ref[1:2, :] * x + cw_ref[2:3, :] * xn)
    for h in range(2 * H):
        seg = y[:, h * DH:(h + 1) * DH]
        seg = seg * lax.rsqrt(jnp.sum(seg * seg, axis=-1, keepdims=True) + EPS)
        if h < H:
            seg = seg * (DH ** -0.5)
        qkv_ref[:, h * DH:(h + 1) * DH] = seg.astype(BF16)
    qkv_ref[:, 2 * HW:3 * HW] = y[:, 2 * HW:3 * HW].astype(BF16)
    z = sm_ref[...] + prm_ref[1:2, :]
    lane = lax.broadcasted_iota(jnp.int32, z.shape, 1)
    g_log = -jnp.exp(prm_ref[0:1, :]) * _softplus(z)
    out = jnp.where(lane < 8, g_log,
                    jnp.where(lane < 16, _sigmoid(z), jnp.where(lane < 24, z, -_softplus(-z))))
    g_ref[...] = out


def _prep2(gqkv, sm, conv_w, prm):
    nb = N_TOK // HALO
    return pl.pallas_call(
        _prep2_kernel,
        grid=(N_TILES,),
        in_specs=[pl.BlockSpec((TT, 3 * HW), lambda i: (i, 0)),
                  pl.BlockSpec((HALO, 3 * HW), lambda i: (jnp.maximum(i * (TT // HALO) - 1, 0), 0)),
                  pl.BlockSpec((HALO, 3 * HW), lambda i: (jnp.minimum((i + 1) * (TT // HALO), nb - 1), 0)),
                  pl.BlockSpec((TT, LANES), lambda i: (i, 0)),
                  pl.BlockSpec((3, 3 * HW), lambda i: (0, 0)),
                  pl.BlockSpec((8, LANES), lambda i: (0, 0))],
        out_specs=[pl.BlockSpec((TT, 3 * HW), lambda i: (i, 0)),
                   pl.BlockSpec((TT, LANES), lambda i: (i, 0))],
        out_shape=[jax.ShapeDtypeStruct((N_TOK, 3 * HW), BF16),
                   jax.ShapeDtypeStruct((N_TOK, LANES), F32)],
        compiler_params=_cparams("parallel"),
    )(gqkv, gqkv, gqkv, sm, conv_w, prm)


def _scan_tables():
    blk_f = np.zeros(N_CHUNKS, np.int32)
    blk_b = np.zeros(N_CHUNKS, np.int32)
    seq = np.zeros(N_CHUNKS, np.int32)
    first = np.zeros(N_CHUNKS, np.int32)
    last = np.zeros(N_CHUNKS, np.int32)
    step = 0
    base = 0
    sid = 0
    for n_seq, t_len in ((B_CTX, T_CTX), (B_LAT, T_LAT)):
        nc = t_len // CH
        for _ in range(n_seq):
            for c in range(nc):
                blk_f[step] = base + c
                blk_b[step] = base + nc - 1 - c
                seq[step] = sid
                first[step] = int(c == 0)
                last[step] = int(c == nc - 1)
                step += 1
            base += nc
            sid += 1
    return blk_f, blk_b, seq, first, last


def _ones_where(mask):
    return jnp.where(mask, 1.0, 0.0).astype(BF16)


def _scan_order_mask(d, n=CH):
    row = lax.broadcasted_iota(jnp.int32, (n, n), 0)
    col = lax.broadcasted_iota(jnp.int32, (n, n), 1)
    return (col <= row) if d == 0 else (col >= row)


def _row_of(col_vec):
    return jnp.broadcast_to(col_vec, (col_vec.shape[0], LANES)).T[0:1, :]


HC = H * CH


def _unit_tri_inverses(lbs, row, col):
    same = lambda sh: jnp.right_shift(row, sh) == jnp.right_shift(col, sh)
    zero = jnp.zeros_like(lbs[0])
    eye = jnp.where(row == col, 1.0, 0.0).astype(BF16)
    ms = [jnp.where(same(3), -lb, zero) for lb in lbs]
    m2s = [_dot(m, m).astype(BF16) for m in ms]
    m4s = [_dot(m2, m2).astype(BF16) for m2 in m2s]
    xs = [eye + m for m in ms]
    xs = [(x.astype(F32) + _dot(x, m2)).astype(BF16) for x, m2 in zip(xs, m2s)]
    xs = [(x.astype(F32) + _dot(x, m4)).astype(BF16) for x, m4 in zip(xs, m4s)]
    for sh in (3, 4, 5):
        pair = jnp.logical_and(same(sh + 1), jnp.logical_not(same(sh)))
        exs = [_dot(jnp.where(pair, lb, zero), x).astype(BF16) for lb, x in zip(lbs, xs)]
        fills = [_dot(x, ex) for x, ex in zip(xs, exs)]
        xs = [jnp.where(pair, (-fill).astype(BF16), x) for fill, x in zip(fills, xs)]
    return xs


def _head_stack(ref, r0, base):
    return jnp.concatenate([ref[r0:r0 + CH, base + h * DH:base + (h + 1) * DH] for h in range(H)], axis=0)


def _gdn_wy_kernel(qkv_ref, g_ref, u_ref, w_ref, qg_ref, kg_ref, attn_ref, gl_ref):
    row = lax.broadcasted_iota(jnp.int32, (HC, HC), 0)
    col = lax.broadcasted_iota(jnp.int32, (HC, HC), 1)
    same_head = jnp.right_shift(row, 6) == jnp.right_shift(col, 6)
    lane_head = jnp.right_shift(lax.broadcasted_iota(jnp.int32, (CH, HC), 1), 6)
    subs = range(TT // CH)
    chains = [(s, d) for s in subs for d in range(2)]
    g = [g_ref[s * CH:(s + 1) * CH, :] for s in subs]
    q_st = [_head_stack(qkv_ref, s * CH, 0) for s in subs]
    k_st = [_head_stack(qkv_ref, s * CH, HW) for s in subs]
    v_st = [_head_stack(qkv_ref, s * CH, 2 * HW).astype(F32) for s in subs]
    kf = [k.astype(F32) for k in k_st]
    kk = [_dot_nt(k, k) for k in k_st]
    qk = [_dot_nt(q, k) for q, k in zip(q_st, k_st)]
    order = [_ones_where(_scan_order_mask(d)) for d in range(2)]
    g3 = [_split3(a) for a in g]
    cols = [_dot(order[d], g3[s][0]) + (_dot(order[d], g3[s][1]) + _dot(order[d], g3[s][2]))
            for s, d in chains]
    incl = [jnp.logical_and(same_head, (col <= row) if d == 0 else (col >= row)) for d in range(2)]
    strict = [jnp.logical_and(same_head, (col < row) if d == 0 else (col > row)) for d in range(2)]
    gc, beta, g_last, decay, l_hi, l_lo, egc, rhs = [], [], [], [], [], [], [], []
    for c, (s, d) in enumerate(chains):
        last = CH - 1 if d == 0 else 0
        lanes = [d * H + h for h in range(H)]
        gc.append(jnp.concatenate([cols[c][:, l:l + 1] for l in lanes], axis=0))
        beta.append(jnp.concatenate([g[s][:, 8 + l:9 + l] for l in lanes], axis=0))
        g_last.append(jnp.concatenate(
            [jnp.broadcast_to(cols[c][last:last + 1, l:l + 1], (CH, 1)) for l in lanes], axis=0))
        gr = _row_of(gc[c])
        decay.append(jnp.where(incl[d], jnp.exp(jnp.where(incl[d], gc[c] - gr, 0.0)), 0.0))
        lmat = jnp.where(strict[d], beta[c] * kk[s] * decay[c], 0.0)
        l_hi.append(lmat.astype(BF16))
        l_lo.append((lmat - l_hi[c].astype(F32)).astype(BF16))
        egc.append(jnp.exp(gc[c]))
        rhs.append(jnp.concatenate([v_st[s] * beta[c], kf[s] * (beta[c] * egc[c])], axis=1))
        gl_rows = [jnp.broadcast_to(cols[c][last:last + 1, l:l + 1], (1, LANES)) for l in lanes]
        gl_ref[d, s * 8:(s + 1) * 8, :] = jnp.concatenate(gl_rows + [jnp.zeros((8 - H, LANES), F32)], axis=0)
    t_inv = _unit_tri_inverses(l_hi, row, col)
    y = [_dot(t, r.astype(BF16)) for t, r in zip(t_inv, rhs)]
    ys = [_split2(a) for a in y]
    ly = [_dot(l_hi[c], ys[c][0]) + (_dot(l_hi[c], ys[c][1]) + _dot(l_lo[c], ys[c][0])) for c in range(len(chains))]
    resid = [(rhs[c] - (y[c] + ly[c])).astype(BF16) for c in range(len(chains))]
    y = [y[c] + _dot(t_inv[c], resid[c]) for c in range(len(chains))]
    for c, (s, d) in enumerate(chains):
        r0 = s * CH
        attn = qk[s] * decay[c]
        a64 = jnp.zeros((CH, HC), F32)
        for h in range(H):
            a64 = jnp.where(lane_head == h, attn[h * CH:(h + 1) * CH, :], a64)
        attn_ref[d, r0:r0 + CH, :] = a64.astype(BF16)
        qg = (q_st[s].astype(F32) * egc[c]).astype(BF16)
        kg = (kf[s] * jnp.exp(g_last[c] - gc[c])).astype(BF16)
        for h in range(H):
            rs = slice(h * CH, (h + 1) * CH)
            ls = slice(h * DH, (h + 1) * DH)
            u_ref[d, r0:r0 + CH, ls] = y[c][rs, 0:DH]
            w_ref[d, r0:r0 + CH, ls] = y[c][rs, DH:2 * DH].astype(BF16)
            qg_ref[d, r0:r0 + CH, ls] = qg[rs, :]
            kg_ref[d, r0:r0 + CH, ls] = kg[rs, :]


def _gdn_wy(qkv, gates):
    spec = lambda w: pl.BlockSpec((2, TT, w), lambda i: (0, i, 0))
    shp = lambda w, dt: jax.ShapeDtypeStruct((2, N_TOK, w), dt)
    return pl.pallas_call(
        _gdn_wy_kernel,
        grid=(N_TILES,),
        in_specs=[pl.BlockSpec((TT, 3 * HW), lambda i: (i, 0)), pl.BlockSpec((TT, LANES), lambda i: (i, 0))],
        out_specs=[spec(HW), spec(HW), spec(HW), spec(HW), spec(HC),
                   pl.BlockSpec((2, 8 * (TT // CH), LANES), lambda i: (0, i, 0))],
        out_shape=[shp(HW, F32), shp(HW, BF16), shp(HW, BF16), shp(HW, BF16), shp(HC, BF16),
                   jax.ShapeDtypeStruct((2, 8 * N_CHUNKS, LANES), F32)],
        compiler_params=_cparams("parallel"),
    )(qkv, gates)


def _gdn_rec_body(ins, of_ref, ob_ref, s_scr):
    chains = [(d, h) for d in range(2) for h in range(H)]
    ls = [slice(h * DH, (h + 1) * DH) for h in range(H)]
    s = [s_scr[d, h] for d, h in chains]
    sb = [a.astype(BF16) for a in s]
    ws = [_dot(ins[6 * d + 1][0, :, ls[h]], sb[c]) for c, (d, h) in enumerate(chains)]
    inter = [_dot(ins[6 * d + 2][0, :, ls[h]], sb[c]) for c, (d, h) in enumerate(chains)]
    vb = [(ins[6 * d][0, :, ls[h]] - ws[c]).astype(BF16) for c, (d, h) in enumerate(chains)]
    intra = [_dot(ins[6 * d + 4][0, :, h * CH:(h + 1) * CH], vb[c]) for c, (d, h) in enumerate(chains)]
    upd = [_dot_tn(ins[6 * d + 3][0, :, ls[h]], vb[c]) for c, (d, h) in enumerate(chains)]
    for c, (d, h) in enumerate(chains):
        o_ref = of_ref if d == 0 else ob_ref
        o_ref[:, ls[h]] = inter[c] + intra[c]
        s_scr[d, h] = s[c] * jnp.exp(ins[6 * d + 5][0, h:h + 1, :]) + upd[c]


def _mlstm_body(qkv_refs, g_refs, o_refs, c_scr, n_scr, m_scr):
    scale = DH ** -0.5
    chains = [(d, h) for d in range(2) for h in range(H)]
    nch = range(len(chains))
    incl = [_scan_order_mask(d) for d in range(2)]
    g = [g_refs[d][...] for d in range(2)]
    cols = [_dot_exact_lhs(_ones_where(incl[d]), g[d]) for d in range(2)]
    g_t = [jnp.concatenate([g[d], cols[d]], axis=0).T for d in range(2)]
    q = [qkv_refs[d][:, h * DH:(h + 1) * DH] for d, h in chains]
    k = [qkv_refs[d][:, HW + h * DH:HW + (h + 1) * DH] for d, h in chains]
    v = [qkv_refs[d][:, 2 * HW + h * DH:2 * HW + (h + 1) * DH] for d, h in chains]
    qk = [_dot_nt(q[c], k[c]) * scale for c in nch]
    cst = [c_scr[d, h] for d, h in chains]
    qc = [_dot(q[c], cst[c].astype(BF16)) for c in nch]
    lf = [24 + d * H + h for d, h in chains]
    li = [16 + d * H + h for d, h in chains]
    dd = [d for d, _ in chains]
    last = [CH - 1 if d == 0 else 0 for d in dd]
    bc = [cols[dd[c]][:, lf[c]:lf[c] + 1] for c in nch]
    br = [g_t[dd[c]][lf[c]:lf[c] + 1, CH:2 * CH] for c in nch]
    ig_c = [g[dd[c]][:, li[c]:li[c] + 1] for c in nch]
    ig_r = [g_t[dd[c]][li[c]:li[c] + 1, 0:CH] for c in nch]
    dlog = [jnp.where(incl[dd[c]], bc[c] - br[c] + ig_r[c], -jnp.inf) for c in nch]
    dmax = [jnp.max(a, axis=1, keepdims=True) for a in dlog]
    b_last = [bc[c][last[c]:last[c] + 1, :] for c in nch]
    w_max = [jnp.max(b_last[c] - br[c] + ig_r[c], axis=1, keepdims=True) for c in nch]
    m = [m_scr[d, h:h + 1, 0:1] for d, h in chains]
    n = [n_scr[d, h:h + 1, :] for d, h in chains]
    m_inter = [bc[c] + m[c] for c in nch]
    m_row = [jnp.maximum(dmax[c], m_inter[c]) for c in nch]
    s = [qk[c] * jnp.exp(dlog[c] - m_row[c]) for c in nch]
    sv = [_dot(s[c].astype(BF16), v[c]) for c in nch]
    e_inter = [jnp.exp(m_inter[c] - m_row[c]) for c in nch]
    qn = [jnp.sum(q[c].astype(F32) * n[c], axis=1, keepdims=True) for c in nch]
    s_sum = [jnp.sum(a, axis=1, keepdims=True) for a in s]
    den = [e_inter[c] * qn[c] + s_sum[c] for c in nch]
    inv = [1.0 / jnp.maximum(jnp.abs(den[c]), jnp.exp(-m_row[c])) for c in nch]
    m_new = [jnp.maximum(b_last[c] + m[c], w_max[c]) for c in nch]
    f_c = [jnp.exp(b_last[c] + m[c] - m_new[c]) for c in nch]
    kw = [k[c].astype(F32) * (jnp.exp(b_last[c] - bc[c] + ig_c[c] - m_new[c]) * scale) for c in nch]
    kv = [_dot_tn(kw[c].astype(BF16), v[c]) for c in nch]
    for c, (d, h) in enumerate(chains):
        o_refs[d][:, h * DH:(h + 1) * DH] = (e_inter[c] * qc[c] + sv[c]) * inv[c]
        n_scr[d, h:h + 1, :] = f_c[c] * n[c] + jnp.sum(kw[c], axis=0, keepdims=True)
        m_scr[d, h:h + 1, :] = jnp.broadcast_to(m_new[c], (1, LANES))
        c_scr[d, h] = f_c[c] * cst[c] + kv[c]


def _rec_kernel(bf_ref, bb_ref, seq_ref, first_ref, last_ref, *refs):
    gdn_ins, (qkvf_ref, gf_ref, qkvb_ref, gb_ref, s0_ref, c0_ref, n0_ref, m0_ref) = refs[:12], refs[12:20]
    of_ref, ob_ref, hf_ref, hb_ref, sfin_ref, cfin_ref, nfin_ref, mfin_ref = refs[20:28]
    s_scr, c_scr, n_scr, m_scr = refs[28:]
    t = pl.program_id(0)

    @pl.when(first_ref[t] == 1)
    def _():
        s_scr[...] = s0_ref[0]
        c_scr[...] = c0_ref[0]
        n_scr[...] = n0_ref[0]
        m_scr[...] = m0_ref[0]

    _gdn_rec_body(gdn_ins, of_ref, ob_ref, s_scr)
    _mlstm_body((qkvf_ref, qkvb_ref), (gf_ref, gb_ref), (hf_ref, hb_ref), c_scr, n_scr, m_scr)

    @pl.when(last_ref[t] == 1)
    def _():
        sfin_ref[0] = s_scr[...]
        cfin_ref[0] = c_scr[...]
        nfin_ref[0] = n_scr[...]
        mfin_ref[0] = m_scr[...]


def _rec_scan(tables, wy, qkv, gates, s0, c0, n0, m0):
    fwd = lambda t, bf, bb, *_: (bf[t], 0)
    bwd = lambda t, bf, bb, *_: (bb[t], 0)

    def dir_specs(d):
        idx = (lambda t, bf, bb, *_: (0, bf[t], 0)) if d == 0 else (lambda t, bf, bb, *_: (1, bb[t], 0))
        return [pl.BlockSpec((1, CH, HW), idx)] * 4 + [pl.BlockSpec((1, CH, HC), idx),
                                                        pl.BlockSpec((1, 8, LANES), idx)]
    vec = pl.BlockSpec((1, 2, H, LANES), lambda t, bf, bb, sq, *_: (sq[t], 0, 0, 0))
    mat = pl.BlockSpec((1, 2, H, DH, DH), lambda t, bf, bb, sq, *_: (sq[t], 0, 0, 0, 0))
    tok = jax.ShapeDtypeStruct((N_TOK, HW), F32)
    mats = jax.ShapeDtypeStruct((N_SEQ, 2, H, DH, DH), F32)
    vecs = jax.ShapeDtypeStruct((N_SEQ, 2, H, LANES), F32)
    return pl.pallas_call(
        _rec_kernel,
        grid_spec=pltpu.PrefetchScalarGridSpec(
            num_scalar_prefetch=5, grid=(N_CHUNKS,),
            in_specs=dir_specs(0) + dir_specs(1)
            + [pl.BlockSpec((CH, 3 * HW), fwd), pl.BlockSpec((CH, LANES), fwd),
               pl.BlockSpec((CH, 3 * HW), bwd), pl.BlockSpec((CH, LANES), bwd), mat, mat, vec, vec],
            out_specs=[pl.BlockSpec((CH, HW), fwd), pl.BlockSpec((CH, HW), bwd),
                       pl.BlockSpec((CH, HW), fwd), pl.BlockSpec((CH, HW), bwd), mat, mat, vec, vec],
            scratch_shapes=[pltpu.VMEM((2, H, DH, DH), F32), pltpu.VMEM((2, H, DH, DH), F32),
                            pltpu.VMEM((2, H, LANES), F32), pltpu.VMEM((2, H, LANES), F32)]),
        out_shape=[tok, tok, tok, tok, mats, mats, vecs, vecs],
        compiler_params=_cparams("arbitrary"),
    )(*tables, *wy, *wy, qkv, gates, qkv, gates, s0, c0, n0, m0)


def _post_kernel(tbl_ref, x_ref, mod_ref, yfc_ref, yfl_ref, of_ref, ob_ref, hf_ref, hb_ref,
                 gz_ref, mo_ref, gates_ref, gn_ref, mn_ref, wbf_ref, wbg_ref, wbm_ref, wo_ref,
                 n2_ref, rwh_ref, rwl_ref, rb_ref,
                 xo_ref, h2_ref, info_ref, infot_ref, stats_ref, cnt_scr):
    i = pl.program_id(0)

    @pl.when(i == 0)
    def _():
        cnt_scr[...] = jnp.zeros_like(cnt_scr)

    yf = jnp.where(i < CTX_TILES, yfc_ref[...], yfl_ref[...])
    og = of_ref[...] + ob_ref[...]
    hm = hf_ref[...] + hb_ref[...]
    og_parts, hm_parts = [], []
    for h in range(H):
        sl = slice(h * DH, (h + 1) * DH)
        og_parts.append(_rms(og[:, sl]) * gn_ref[...] * _silu(gz_ref[:, sl].astype(F32)))
        hm_parts.append(_rms(hm[:, sl]) * mn_ref[:, sl] * _sigmoid(mo_ref[:, sl].astype(F32)))
    ogb = jnp.concatenate(og_parts, axis=1).astype(BF16)
    hmb = jnp.concatenate(hm_parts, axis=1).astype(BF16)
    y = (_sigmoid(gates_ref[:, 0:D].astype(F32)) * _dot(yf, wbf_ref[...])
         + _sigmoid(gates_ref[:, D:2 * D].astype(F32)) * _dot(ogb, wbg_ref[...])
         + _sigmoid(gates_ref[:, 2 * D:3 * D].astype(F32)) * _dot(hmb, wbm_ref[...]))
    x = x_ref[...] + mod_ref[0, 2:3, :] * _dot(y.astype(BF16), wo_ref[...])
    xo_ref[...] = x
    h2 = _rms(x) * n2_ref[...]
    h2 = h2 * (1.0 + mod_ref[0, 4:5, :]) + mod_ref[0, 3:4, :]
    hb2, hl2 = _split2(h2)
    h2_ref[...] = hb2
    logits = (_dot(hb2, rwh_ref[...]) + (_dot(hb2, rwl_ref[...]) + _dot(hl2, rwh_ref[...]))
              + rb_ref[...])
    lane = lax.broadcasted_iota(jnp.int32, logits.shape, 1).astype(F32)
    work = jnp.where(lane < N_EXP, logits, -jnp.inf)
    sel = jnp.zeros(logits.shape, F32)
    hits, exps = [], []
    top0 = None
    denom = None
    for kk in range(TOP_K):
        mx = jnp.max(work, axis=1, keepdims=True)
        first = jnp.min(jnp.where(work == mx, lane, float(LANES)), axis=1, keepdims=True)
        hit = lane == first
        if kk == 0:
            top0 = mx
        e = jnp.exp(mx - top0)
        denom = e if kk == 0 else denom + e
        hits.append(hit)
        exps.append(e)
        sel = jnp.where(hit, 1.0, sel)
        work = jnp.where(hit, -jnp.inf, work)
    r_i = lax.broadcasted_iota(jnp.int32, (TT, TT), 0)
    c_i = lax.broadcasted_iota(jnp.int32, (TT, TT), 1)
    within = _dot(_ones_where(c_i < r_i), sel.astype(BF16))
    tile_cnt = jnp.sum(sel, axis=0, keepdims=True)
    a_i = lax.broadcasted_iota(jnp.int32, (LANES, LANES), 0)
    b_i = lax.broadcasted_iota(jnp.int32, (LANES, LANES), 1)
    tile_off = _dot(jnp.broadcast_to(tile_cnt, (8, LANES)).astype(BF16), _ones_where(a_i < b_i))[0:1, :]
    local = tile_off + within
    info = jnp.zeros(logits.shape, F32)
    for kk in range(TOP_K):
        pos = jnp.sum(jnp.where(hits[kk], local, 0.0), axis=1, keepdims=True)
        info = jnp.where(lane == float(kk), pos, info)
        info = jnp.where(lane == float(TOP_K + kk), exps[kk] / denom, info)
    info_ref[...] = info
    infot_ref[...] = info.T[0:8, :]
    stats_ref[0] = jnp.concatenate([cnt_scr[0:1, :], tile_cnt, tile_off, jnp.zeros((5, LANES), F32)], axis=0)
    cnt_scr[0:1, :] = cnt_scr[0:1, :] + tile_cnt


def _post(cond_tbl, x, mod, yfc, yfl, o_f, o_b, h_f, h_b, gz, mo, gates, gn, mn, wbf, wbg, wbm, wo,
          n2, rwh, rwl, rb):
    tile = lambda w: pl.BlockSpec((TT, w), lambda i, t: (i, 0))
    const = lambda shape: pl.BlockSpec(shape, lambda i, t: (0,) * len(shape))
    in_specs = [tile(D), pl.BlockSpec((1, N_MOD, D), lambda i, t: (t[i], 0, 0)),
                pl.BlockSpec((TT, HW), lambda i, t: (jnp.minimum(i, CTX_TILES - 1), 0)),
                pl.BlockSpec((TT, HW), lambda i, t: (jnp.maximum(i - CTX_TILES, 0), 0)),
                tile(HW), tile(HW), tile(HW), tile(HW),
                tile(HW), tile(HW), tile(3 * D),
                const((1, DH)), const((1, HW)), const((HW, D)), const((HW, D)), const((HW, D)),
                const((D, D)), const((1, D)), const((D, LANES)), const((D, LANES)), const((1, LANES))]
    return pl.pallas_call(
        _post_kernel,
        grid_spec=pltpu.PrefetchScalarGridSpec(
            num_scalar_prefetch=1, grid=(N_TILES,), in_specs=in_specs,
            out_specs=[tile(D), tile(D), tile(LANES), pl.BlockSpec((8, TT), lambda i, t: (0, i)),
                       pl.BlockSpec((1, 8, LANES), lambda i, t: (i, 0, 0))],
            scratch_shapes=[pltpu.VMEM((8, LANES), F32)]),
        out_shape=[jax.ShapeDtypeStruct((N_TOK, D), F32), jax.ShapeDtypeStruct((N_TOK, D), BF16),
                   jax.ShapeDtypeStruct((N_TOK, LANES), F32), jax.ShapeDtypeStruct((8, N_TOK), F32),
                   jax.ShapeDtypeStruct((N_TILES, 8, LANES), F32)],
        compiler_params=_cparams("arbitrary"),
    )(cond_tbl, x, mod, yfc, yfl, o_f, o_b, h_f, h_b, gz, mo, gates, gn, mn, wbf, wbg, wbm, wo,
      n2, rwh, rwl, rb)


PAIRS = TT * TOP_K
RUN_BITS = tuple(1 << b for b in range(8, -1, -1))


def _slot_rows(start_slot, n_slots):
    return pl.ds(pl.multiple_of(start_slot * 8, 8), n_slots * 8)


def _for_each_run(base, cnt_ref, fn):
    def body(e, carry):
        cnt = cnt_ref[base + e]
        for bit in RUN_BITS:
            @pl.when((cnt & bit) != 0)
            def _():
                fn(base + e, cnt & (-2 * bit), bit)
        return carry
    lax.fori_loop(0, N_EXP, body, 0)


def _zero_fill(ps_ref, pc_ref, tail_ref, xs_hbm, zbuf, zsem, start):
    def piece(dst_slot, n):
        cp = pltpu.make_async_copy(zbuf.at[pl.ds(0, n * 8)], xs_hbm.at[_slot_rows(dst_slot, n)], zsem)
        if start:
            cp.start()
        else:
            cp.wait()

    def pad(e, carry):
        cnt = pc_ref[e]
        for bit in RUN_BITS[1:]:
            @pl.when((cnt & bit) != 0)
            def _():
                piece(ps_ref[e] + (cnt & (-2 * bit)), bit)
        return carry

    def tail(t, carry):
        piece(tail_ref[0] + t * TM, TM)
        return carry

    lax.fori_loop(0, N_EXP, pad, 0)
    lax.fori_loop(0, tail_ref[1], tail, 0)


def _dispatch_kernel(ss_ref, cnt_ref, toff_ref, ps_ref, pc_ref, tail_ref, h2_ref, infot_ref, xs_hbm,
                     stage, zbuf, sem, zsem):
    i = pl.program_id(0)
    buf = i % 2
    whole = lambda b: pltpu.make_async_copy(stage.at[b], xs_hbm.at[pl.ds(0, PAIRS * 8)], sem.at[b])

    @pl.when(i >= 2)
    def _():
        whole(buf).wait()

    r = lax.broadcasted_iota(jnp.int32, (PAIRS, TT), 0).astype(F32)
    pick = r == infot_ref[0:1, :]
    for kk in range(1, TOP_K):
        pick = jnp.logical_or(pick, r == infot_ref[kk:kk + 1, :])
    rows = _dot(_ones_where(pick), h2_ref[...])
    for cc in range(8):
        stage[buf, pl.ds(cc, PAIRS, stride=8), :] = rows[:, cc * LANES:(cc + 1) * LANES]

    def send(j, o, n):
        pltpu.make_async_copy(stage.at[buf, _slot_rows(toff_ref[j] + o, n)],
                              xs_hbm.at[_slot_rows(ss_ref[j] + o, n)], sem.at[buf]).start()

    _for_each_run(i * N_EXP, cnt_ref, send)

    @pl.when(i == 0)
    def _():
        zbuf[...] = jnp.zeros_like(zbuf)
        _zero_fill(ps_ref, pc_ref, tail_ref, xs_hbm, zbuf, zsem, True)

    @pl.when(i == N_TILES - 1)
    def _():
        whole(1 - buf).wait()
        whole(buf).wait()
        _zero_fill(ps_ref, pc_ref, tail_ref, xs_hbm, zbuf, zsem, False)


def _dispatch(slot_start, seg_cnt, seg_off, pad_start, pad_cnt, tail, h2, info_t):
    return pl.pallas_call(
        _dispatch_kernel,
        grid_spec=pltpu.PrefetchScalarGridSpec(
            num_scalar_prefetch=6, grid=(N_TILES,),
            in_specs=[pl.BlockSpec((TT, D), lambda i, *_: (i, 0)), pl.BlockSpec((8, TT), lambda i, *_: (0, i))],
            out_specs=pl.BlockSpec(memory_space=pl.ANY),
            scratch_shapes=[pltpu.VMEM((2, PAIRS * 8, LANES), F32), pltpu.VMEM((TM * 8, LANES), F32),
                            pltpu.SemaphoreType.DMA((2,)), pltpu.SemaphoreType.DMA(())]),
        out_shape=jax.ShapeDtypeStruct((S_MAX * 8, LANES), F32),
        compiler_params=_cparams("arbitrary"),
    )(slot_start, seg_cnt, seg_off, pad_start, pad_cnt, tail, h2, info_t)


N_MAT = 3


def _gmm_kernel(te_ref, nu_ref, valid_ref, nxt_ref, x_ref, wg_hbm, bg_ref, wu_hbm, bu_ref, wd_hbm, bd_ref, o_ref,
                wbuf, wg_scr, wu_scr, wd_scr, slot_ref, wsem, *, layer):
    i = pl.program_id(0)
    used = i < nu_ref[0]

    def fetch(e, slot, start):
        for m, w_hbm in enumerate((wg_hbm, wu_hbm, wd_hbm)):
            cp = pltpu.make_async_copy(w_hbm.at[layer, e], wbuf.at[slot * N_MAT + m], wsem.at[slot * N_MAT + m])
            if start:
                cp.start()
            else:
                cp.wait()

    @pl.when(i == 0)
    def _():
        slot_ref[0] = 0
        fetch(te_ref[0], 0, True)

    @pl.when(used)
    def _():
        e = te_ref[i]
        e_prev = te_ref[jnp.maximum(i - 1, 0)]

        @pl.when(jnp.logical_or(i == 0, e != e_prev))
        def _():
            slot = slot_ref[0]
            fetch(e, slot, False)
            wg_scr[...] = wbuf[slot * N_MAT].astype(BF16)
            wu_scr[...] = wbuf[slot * N_MAT + 1].astype(BF16)
            wd_scr[...] = wbuf[slot * N_MAT + 2].astype(BF16)

            @pl.when(nxt_ref[i] >= 0)
            def _():
                fetch(nxt_ref[i], 1 - slot, True)

            slot_ref[0] = 1 - slot

        x = jnp.concatenate([x_ref[pl.ds(cc, TM, stride=8), :] for cc in range(8)], axis=1)
        row = lax.broadcasted_iota(jnp.int32, (TM, 1), 0)
        x = jnp.where(row < valid_ref[i], x, 0.0).astype(BF16)
        acts = []
        for j in range(D // FF_CHUNK):
            cs = slice(j * FF_CHUNK, (j + 1) * FF_CHUNK)
            gate = jnp.minimum(_dot(x, wg_scr[:, cs]) + bg_ref[0, 0, :, cs], SWIGLU_LIMIT)
            up = jnp.clip(_dot(x, wu_scr[:, cs]) + bu_ref[0, 0, :, cs], -SWIGLU_LIMIT, SWIGLU_LIMIT)
            acts.append(((up + 1.0) * gate * _sigmoid(SWIGLU_ALPHA * gate)).astype(BF16))
        y = _dot(jnp.concatenate(acts, axis=1), wd_scr[...]) + bd_ref[0, 0]
        for cc in range(8):
            o_ref[pl.ds(cc, TM, stride=8), :] = y[:, cc * LANES:(cc + 1) * LANES]

    @pl.when(jnp.logical_not(used))
    def _():
        o_ref[...] = jnp.zeros_like(o_ref)


def _gmm(layer, tile_exp, n_used, valid, nxt_exp, xs, wg, bg, wu, bu, wd, bd):
    wspec = pl.BlockSpec(memory_space=pl.ANY)
    bspec = pl.BlockSpec((1, 1, 1, D), lambda i, te, *_: (layer, te[i], 0, 0))
    return pl.pallas_call(
        functools.partial(_gmm_kernel, layer=layer),
        grid_spec=pltpu.PrefetchScalarGridSpec(
            num_scalar_prefetch=4, grid=(NT_MAX,),
            in_specs=[pl.BlockSpec((TM * 8, LANES), lambda i, te, nu, *_: (jnp.minimum(i, nu[0] - 1), 0)),
                      wspec, bspec, wspec, bspec, wspec, bspec],
            out_specs=pl.BlockSpec((TM * 8, LANES), lambda i, *_: (i, 0)),
            scratch_shapes=[pltpu.VMEM((2 * N_MAT, D, D), F32)] + [pltpu.VMEM((D, D), BF16)] * N_MAT
            + [pltpu.SMEM((1,), jnp.int32), pltpu.SemaphoreType.DMA((2 * N_MAT,))]),
        out_shape=jax.ShapeDtypeStruct((S_MAX * 8, LANES), F32),
        compiler_params=_cparams("arbitrary"),
    )(tile_exp, n_used, valid, nxt_exp, xs, wg, bg.reshape(DEPTH, N_EXP, 1, D), wu,
      bu.reshape(DEPTH, N_EXP, 1, D), wd, bd.reshape(DEPTH, N_EXP, 1, D))


def _combine_kernel(tbl_ref, ss_ref, cnt_ref, toff_ref, y_hbm, info_ref, x_ref, mod_ref, g_ref, *refs,
                    final):
    outs, (ybuf, sem) = refs[:-2], refs[-2:]
    i = pl.program_id(0)
    buf = i % 2

    def fetch(tile, b):
        def recv(j, o, n):
            pltpu.make_async_copy(y_hbm.at[_slot_rows(ss_ref[j] + o, n)],
                                  ybuf.at[b, _slot_rows(toff_ref[j] + o, n)], sem.at[b]).start()
        _for_each_run(tile * N_EXP, cnt_ref, recv)

    @pl.when(i == 0)
    def _():
        fetch(0, 0)

    @pl.when(i + 1 < N_TILES)
    def _():
        fetch(i + 1, 1 - buf)

    r = lax.broadcasted_iota(jnp.int32, (TT, PAIRS), 1).astype(F32)
    wm = jnp.zeros((TT, PAIRS), F32)
    for kk in range(TOP_K):
        wm = jnp.where(r == info_ref[:, kk:kk + 1], info_ref[:, TOP_K + kk:TOP_K + kk + 1], wm)
    pltpu.make_async_copy(y_hbm.at[pl.ds(0, PAIRS * 8)], ybuf.at[buf], sem.at[buf]).wait()
    y = jnp.concatenate([ybuf[buf, pl.ds(cc, PAIRS, stride=8), :] for cc in range(8)], axis=1)
    x = x_ref[...] + mod_ref[0, 5:6, :] * _dot(wm.astype(BF16), y.astype(BF16))
    if not final:
        outs[0][...] = x
    else:
        res = _rms(x) * g_ref[...]

        @pl.when(i < CTX_TILES)
        def _():
            outs[0][...] = res

        @pl.when(i >= CTX_TILES)
        def _():
            outs[1][...] = res


def _combine(final, cond_tbl, slot_start, seg_cnt, seg_off, y_sorted, info, x, mod, g):
    tile = lambda w: pl.BlockSpec((TT, w), lambda i, *_: (i, 0))
    if final:
        out_specs = [pl.BlockSpec((TT, D), lambda i, *_: (jnp.minimum(i, CTX_TILES - 1), 0)),
                     pl.BlockSpec((TT, D), lambda i, *_: (jnp.maximum(i - CTX_TILES, 0), 0))]
        out_shape = [jax.ShapeDtypeStruct((N_CTX, D), F32), jax.ShapeDtypeStruct((N_LAT, D), F32)]
    else:
        out_specs = [tile(D)]
        out_shape = [jax.ShapeDtypeStruct((N_TOK, D), F32)]
    return pl.pallas_call(
        functools.partial(_combine_kernel, final=final),
        grid_spec=pltpu.PrefetchScalarGridSpec(
            num_scalar_prefetch=4, grid=(N_TILES,),
            in_specs=[pl.BlockSpec(memory_space=pl.ANY), tile(LANES), tile(D),
                      pl.BlockSpec((1, N_MOD, D), lambda i, t, *_: (t[i], 0, 0)),
                      pl.BlockSpec((1, D), lambda i, *_: (0, 0))],
            out_specs=out_specs,
            scratch_shapes=[pltpu.VMEM((2, PAIRS * 8, LANES), F32), pltpu.SemaphoreType.DMA((2,))]),
        out_shape=out_shape,
        compiler_params=_cparams("arbitrary"),
    )(cond_tbl, slot_start, seg_cnt, seg_off, y_sorted, info, x, mod, g)


def _dft_consts(t_len):
    k = np.arange(t_len, dtype=np.int64)
    ang = 2.0 * np.pi * ((k[:, None] * k[None, :]) % t_len).astype(np.float64) / t_len
    return np.concatenate([np.cos(ang), -np.sin(ang)], axis=1).astype(np.float32)


def _pos_table():
    quarter = D // 4
    omega = 1.0 / (10000.0 ** (np.arange(quarter, dtype=np.float32) / np.float32(quarter)))
    omega = omega.astype(np.float32).astype(np.float64)
    t = np.arange(T_LAT)
    ang_r = (t // GRID_W).astype(np.float64)[:, None] * omega
    ang_c = (t % GRID_W).astype(np.float64)[:, None] * omega
    return np.concatenate([np.sin(ang_r), np.cos(ang_r), np.sin(ang_c), np.cos(ang_c)],
                          axis=-1).astype(np.float32)


def _repack_w_in(w):
    a, b, c, d = 2560, 2576, 4624, 4640
    main = jnp.concatenate([w[:, :a], w[:, b:c], w[:, d:]], axis=1).astype(BF16)
    small = jnp.concatenate([w[:, a:b], w[:, c:d], jnp.zeros((D, LANES - 32), F32)], axis=1)
    hi = small.astype(BF16)
    lo = (small - hi.astype(F32)).astype(BF16)
    return main, hi, lo


def _route_glue(stats):
    seg_base = stats[:, 0, :N_EXP].astype(jnp.int32)
    seg_cnt = stats[:, 1, :N_EXP].astype(jnp.int32)
    seg_off = stats[:, 2, :N_EXP].astype(jnp.int32)
    counts = seg_base[-1] + seg_cnt[-1]
    region = ((counts + TM - 1) // TM) * TM
    ends = jnp.cumsum(region)
    off = ends - region
    n_used = ends[-1] // TM
    tile_start = jnp.arange(NT_MAX, dtype=jnp.int32) * TM
    tile_exp = jnp.minimum(jnp.sum((tile_start[:, None] >= ends[None, :]).astype(jnp.int32), axis=1), N_EXP - 1)
    used = tile_start < ends[-1]
    tile_exp = jnp.where(used, tile_exp, jnp.max(jnp.where(used, tile_exp, 0)))
    run_end = jnp.sum(jnp.where(tile_exp[:, None] == jnp.arange(N_EXP)[None, :], (off + counts)[None, :], 0), axis=1)
    valid = jnp.clip(run_end - tile_start, 0, TM)
    experts = jnp.arange(N_EXP, dtype=jnp.int32)
    later = jnp.logical_and(experts[None, :] > experts[:, None], counts[None, :] > 0)
    next_used = jnp.min(jnp.where(later, experts[None, :], N_EXP), axis=1)
    next_used = jnp.where(next_used < N_EXP, next_used, -1)
    nxt_exp = jnp.sum(jnp.where(tile_exp[:, None] == experts[None, :], next_used[None, :], 0), axis=1)
    slot_start = off[None, :] + seg_base
    tail = jnp.stack([ends[-1], (S_MAX - ends[-1]) // TM]).astype(jnp.int32)
    return (tile_exp.astype(jnp.int32), n_used.reshape(1).astype(jnp.int32), valid.astype(jnp.int32),
            nxt_exp.astype(jnp.int32), slot_start.reshape(-1), seg_cnt.reshape(-1), seg_off.reshape(-1),
            (off + counts).astype(jnp.int32), (region - counts).astype(jnp.int32), tail)


def kernel(x_prompt, x_sample, state_gdn, state_mlstm_c, state_mlstm_n, state_mlstm_m, c, c_ctx,
           w_ada, b_ada, norm1_g, norm2_g, w_in, gdn_conv_w, gdn_a_log, gdn_dt_bias, gdn_norm_g,
           mlstm_i_bias, mlstm_f_bias, mlstm_norm_g, w_branch_f, w_branch_g, w_branch_m, w_out,
           router_w, router_b, exp_w_gate, exp_b_gate, exp_w_up, exp_b_up, exp_w_down, exp_b_down,
           final_norm_g):
    cond_tbl = jnp.asarray(np.concatenate([np.zeros(CTX_TILES, np.int32),
                                           1 + np.arange(N_TILES - CTX_TILES, dtype=np.int32) // LAT_TILES_PER_SEQ]))
    tables = tuple(jnp.asarray(t) for t in _scan_tables())
    cond8 = jnp.concatenate([c_ctx[None, :], c, jnp.zeros((8 - 1 - B_LAT, D), F32)], axis=0)
    mod_all = _ada(cond8, w_ada, b_ada).reshape(DEPTH, 8, N_MOD, D)

    pos = jnp.asarray(_pos_table())
    dft_c = jnp.asarray(_dft_consts(T_CTX)).astype(BF16)
    dft_l = jnp.asarray(_dft_consts(T_LAT)).astype(BF16)
    ang = 2.0 * np.pi * ((np.arange(DH)[:, None] * np.arange(DH)[None, :]) % DH) / DH
    c128 = jnp.asarray(np.cos(ang).astype(np.float32)).astype(BF16)
    s128 = jnp.asarray(np.sin(ang).astype(np.float32)).astype(BF16)

    zeros_s = jnp.zeros((B_CTX, 2, H, DH, DH), F32)
    zeros_v = jnp.zeros((B_CTX, 2, H, LANES), F32)
    x = None
    gdn_states, c_states, n_states, m_states = [], [], [], []
    y_prompt = y_sample = None
    for l in range(DEPTH):
        mod = mod_all[l]
        wm, wsh, wsl = _repack_w_in(w_in[l])
        if l == 0:
            outs = _pre(True, (x_prompt.reshape(N_CTX, D), x_sample.reshape(N_LAT, D), pos), mod,
                        norm1_g[l][None, :], wm, wsh, wsl, cond_tbl)
            x, outs = outs[0], outs[1:]
        else:
            outs = _pre(False, x, mod, norm1_g[l][None, :], wm, wsh, wsl, cond_tbl)
        f_all, gqkv, gz, mqkv, mo, gates, sm = outs

        yfc = _fourier(f_all, dft_c, c128, s128, B_CTX, T_CTX, 0)
        yfl = _fourier(f_all, dft_l, c128, s128, B_LAT, T_LAT, N_CTX)

        prm = jnp.zeros((8, LANES), F32)
        prm = prm.at[0, 0:8].set(gdn_a_log[l].reshape(-1))
        prm = prm.at[1, 0:8].set(gdn_dt_bias[l].reshape(-1))
        prm = prm.at[1, 16:24].set(mlstm_i_bias[l].reshape(-1))
        prm = prm.at[1, 24:32].set(mlstm_f_bias[l].reshape(-1))
        qkv_c, gts = _prep2(gqkv, sm, gdn_conv_w[l], prm)

        s0 = jnp.concatenate([zeros_s, state_gdn[:, l]], axis=0)
        c0 = jnp.concatenate([zeros_s, state_mlstm_c[:, l]], axis=0)
        n0 = jnp.concatenate([zeros_v, state_mlstm_n[:, l]], axis=0)
        m0 = jnp.concatenate([zeros_v, jnp.broadcast_to(state_mlstm_m[:, l][..., None],
                                                         (B_LAT, 2, H, LANES))], axis=0)
        o_f, o_b, h_f, h_b, s_fin, c_fin, n_fin, m_fin = _rec_scan(
            tables, _gdn_wy(qkv_c, gts), mqkv, gts, s0, c0, n0, m0)
        gdn_states.append(s_fin[:B_CTX])
        c_states.append(c_fin[:B_CTX])
        n_states.append(n_fin[:B_CTX])
        m_states.append(m_fin[:B_CTX, :, :, 0])

        rw = jnp.concatenate([router_w[l], jnp.zeros((D, LANES - N_EXP), F32)], axis=1)
        rwh = rw.astype(BF16)
        rwl = (rw - rwh.astype(F32)).astype(BF16)
        rb = jnp.concatenate([router_b[l], jnp.zeros((LANES - N_EXP,), F32)])[None, :]
        x, h2, info, info_t, stats = _post(
            cond_tbl, x, mod, yfc, yfl, o_f, o_b, h_f, h_b, gz, mo, gates,
            gdn_norm_g[l][None, :], mlstm_norm_g[l][None, :],
            w_branch_f[l].astype(BF16), w_branch_g[l].astype(BF16), w_branch_m[l].astype(BF16),
            w_out[l].astype(BF16), norm2_g[l][None, :], rwh, rwl, rb)

        (tile_exp, n_used, valid, nxt_exp, slot_start, seg_cnt, seg_off, pad_start, pad_cnt,
         tail) = _route_glue(stats)
        xs_sorted = _dispatch(slot_start, seg_cnt, seg_off, pad_start, pad_cnt, tail, h2, info_t)
        y_sorted = _gmm(l, tile_exp, n_used, valid, nxt_exp, xs_sorted, exp_w_gate, exp_b_gate, exp_w_up,
                        exp_b_up, exp_w_down, exp_b_down)
        res = _combine(l + 1 == DEPTH, cond_tbl, slot_start, seg_cnt, seg_off, y_sorted, info, x, mod,
                       final_norm_g[None, :])
        if l + 1 < DEPTH:
            x = res[0]
        else:
            y_prompt, y_sample = res

    return (y_prompt.reshape(B_CTX, T_CTX, D), y_sample.reshape(B_LAT, T_LAT, D),
            jnp.stack(gdn_states, axis=1), jnp.stack(c_states, axis=1),
            jnp.stack(n_states, axis=1), jnp.stack(m_states, axis=1))
```

```python
import functools
import math

import numpy as np
import jax
import jax.numpy as jnp
from jax import lax
from jax.experimental import pallas as pl
from jax.experimental.pallas import tpu as pltpu

F32 = jnp.float32
BF16 = jnp.bfloat16

D = 1024
DEPTH = 2
B_CTX, T_CTX = 16, 256
B_LAT, T_LAT = 4, 2048
N_CTX = B_CTX * T_CTX
N_LAT = B_LAT * T_LAT
N_TOK = N_CTX + N_LAT
N_SEQ = B_CTX + B_LAT
GRID_W = 64
H = 4
DH = 128
HW = H * DH
CH = 64
TT = 256
N_TILES = N_TOK // TT
CTX_TILES = N_CTX // TT
LAT_TILES_PER_SEQ = T_LAT // TT
N_CHUNKS = N_TOK // CH
N_EXP = 32
TOP_K = 4
N_MOD = 6
EPS = 1e-6
SWIGLU_ALPHA = 1.702
SWIGLU_LIMIT = 7.0
LANES = 128
TM = 256
FF_CHUNK = 256
S_MAX = N_TOK * TOP_K + N_EXP * (TM - 1)
S_MAX = ((S_MAX + TM - 1) // TM) * TM
NT_MAX = S_MAX // TM

W_MAIN = 512 + 1536 + 512 + 1536 + 512 + 3072


def _dot(a, b):
    return jnp.dot(a, b, preferred_element_type=F32)


def _dot_nt(a, b):
    return lax.dot_general(a, b, (((1,), (1,)), ((), ())), preferred_element_type=F32)


def _dot_tn(a, b):
    return lax.dot_general(a, b, (((0,), (0,)), ((), ())), preferred_element_type=F32)


def _split2(a):
    hi = a.astype(BF16)
    lo = (a - hi.astype(F32)).astype(BF16)
    return hi, lo


def _split3(a):
    hi = a.astype(BF16)
    r = a - hi.astype(F32)
    mid = r.astype(BF16)
    lo = (r - mid.astype(F32)).astype(BF16)
    return hi, mid, lo


def _dot3(a, b):
    ah, al = _split2(a)
    bh, bl = _split2(b)
    return _dot(ah, bh) + (_dot(ah, bl) + _dot(al, bh))


def _dot_exact_lhs(a_bf16, b):
    bh, bm, bl = _split3(b)
    return _dot(a_bf16, bh) + (_dot(a_bf16, bm) + _dot(a_bf16, bl))


def _sigmoid(x):
    return 0.5 * jnp.tanh(0.5 * x) + 0.5


def _silu(x):
    return x * _sigmoid(x)


def _softplus(x):
    return jnp.maximum(x, 0.0) + jnp.log(1.0 + jnp.exp(-jnp.abs(x)))


def _rms(x):
    return x * lax.rsqrt(jnp.mean(x * x, axis=-1, keepdims=True) + EPS)


def _cparams(*sem):
    return pltpu.CompilerParams(dimension_semantics=tuple(sem))


def _ada_kernel(c_ref, w_ref, b_ref, o_ref):
    o_ref[0] = _dot3(_silu(c_ref[...]), w_ref[0]) + b_ref[0]


def _ada(cond8, w_ada, b_ada):
    nb = 1536
    return pl.pallas_call(
        _ada_kernel,
        grid=(DEPTH, N_MOD * D // nb),
        in_specs=[pl.BlockSpec((8, D), lambda l, j: (0, 0)),
                  pl.BlockSpec((1, D, nb), lambda l, j: (l, 0, j)),
                  pl.BlockSpec((1, 1, nb), lambda l, j: (l, 0, j))],
        out_specs=pl.BlockSpec((1, 8, nb), lambda l, j: (l, 0, j)),
        out_shape=jax.ShapeDtypeStruct((DEPTH, 8, N_MOD * D), F32),
        compiler_params=_cparams("parallel", "parallel"),
    )(cond8, w_ada, b_ada.reshape(DEPTH, 1, N_MOD * D))


def _pre_body(x, mod_ref, n1_ref, wm_ref, wsh_ref, wsl_ref, outs):
    f_ref, gqkv_ref, gz_ref, mqkv_ref, mo_ref, gates_ref, sm_ref = outs
    h = _rms(x) * n1_ref[...]
    h = h * (1.0 + mod_ref[0, 1:2, :]) + mod_ref[0, 0:1, :]
    hb, hl = _split2(h)
    off = 0
    for ref, width in ((f_ref, 512), (gqkv_ref, 1536), (gz_ref, 512), (mqkv_ref, 1536),
                       (mo_ref, 512), (gates_ref, 3072)):
        ref[...] = _dot(hb, wm_ref[0, :, off:off + width]).astype(BF16)
        off += width
    sm_ref[...] = _dot(hb, wsh_ref[0]) + (_dot(hb, wsl_ref[0]) + _dot(hl, wsh_ref[0]))


def _pre_first_kernel(tbl_ref, xp_ref, xs_ref, pos_ref, mod_ref, n1_ref, wm_ref, wsh_ref, wsl_ref,
                      x_out_ref, *outs):
    i = pl.program_id(0)
    x = jnp.where(i < CTX_TILES, xp_ref[...], xs_ref[...] + pos_ref[...])
    x_out_ref[...] = x
    _pre_body(x, mod_ref, n1_ref, wm_ref, wsh_ref, wsl_ref, outs)


def _pre_next_kernel(tbl_ref, x_ref, mod_ref, n1_ref, wm_ref, wsh_ref, wsl_ref, *outs):
    _pre_body(x_ref[...], mod_ref, n1_ref, wm_ref, wsh_ref, wsl_ref, outs)


_PRE_OUT_WIDTHS = ((512, BF16), (1536, BF16), (512, BF16), (1536, BF16), (512, BF16), (3072, BF16),
                   (LANES, F32))


def _pre(layer, xs_in, mod, n1, wm, wsh, wsl, cond_tbl):
    first = layer == 0
    tile = lambda w: pl.BlockSpec((TT, w), lambda i, t: (i, 0))
    const = lambda shape: pl.BlockSpec(shape, lambda i, t: (0,) * len(shape))
    of_layer = lambda width: pl.BlockSpec((1, D, width), lambda i, t: (layer, 0, 0))
    w_specs = [pl.BlockSpec((1, N_MOD, D), lambda i, t: (t[i], 0, 0)), const((1, D)),
               of_layer(W_MAIN), of_layer(LANES), of_layer(LANES)]
    out_specs = [tile(w) for w, _ in _PRE_OUT_WIDTHS]
    out_shape = [jax.ShapeDtypeStruct((N_TOK, w), dt) for w, dt in _PRE_OUT_WIDTHS]
    if first:
        x_prompt2, x_sample2, pos = xs_in
        in_specs = [pl.BlockSpec((TT, D), lambda i, t: (jnp.minimum(i, CTX_TILES - 1), 0)),
                    pl.BlockSpec((TT, D), lambda i, t: (jnp.maximum(i - CTX_TILES, 0), 0)),
                    pl.BlockSpec((TT, D), lambda i, t: (jnp.maximum(i - CTX_TILES, 0) % LAT_TILES_PER_SEQ, 0))]
        kern = _pre_first_kernel
        out_specs = [tile(D)] + out_specs
        out_shape = [jax.ShapeDtypeStruct((N_TOK, D), F32)] + out_shape
        args = (x_prompt2, x_sample2, pos)
    else:
        in_specs = [tile(D)]
        kern = _pre_next_kernel
        args = (xs_in,)
    return pl.pallas_call(
        kern,
        grid_spec=pltpu.PrefetchScalarGridSpec(
            num_scalar_prefetch=1, grid=(N_TILES,), in_specs=in_specs + w_specs, out_specs=out_specs),
        out_shape=out_shape,
        compiler_params=_cparams("parallel"),
    )(cond_tbl, *args, mod, n1, wm, wsh, wsl)


def _fourier_kernel(x_ref, dft_ref, c_ref, s_ref, o_ref, z_scr, *, t_len, scale):
    @pl.when(pl.program_id(1) == 0)
    def _():
        for g in range(H):
            xg = x_ref[:, g * DH:(g + 1) * DH]
            z_scr[0:t_len, g * DH:(g + 1) * DH] = _dot(xg, c_ref[...]).astype(BF16)
            z_scr[t_len:2 * t_len, g * DH:(g + 1) * DH] = _dot(xg, s_ref[...]).astype(BF16)

    o_ref[...] = (_dot(dft_ref[...], z_scr[...]) * scale).astype(BF16)


def _fourier(f_all, dft, c128, s128, n_seq, t_len, row0):
    tr = min(t_len, 512)
    blk0 = row0 // t_len
    return pl.pallas_call(
        functools.partial(_fourier_kernel, t_len=t_len, scale=1.0 / math.sqrt(t_len * DH)),
        grid=(n_seq, t_len // tr),
        in_specs=[pl.BlockSpec((t_len, HW), lambda b, r: (blk0 + b, 0)),
                  pl.BlockSpec((tr, 2 * t_len), lambda b, r: (r, 0)),
                  pl.BlockSpec((DH, DH), lambda b, r: (0, 0)),
                  pl.BlockSpec((DH, DH), lambda b, r: (0, 0))],
        out_specs=pl.BlockSpec((tr, HW), lambda b, r: (b * (t_len // tr) + r, 0)),
        out_shape=jax.ShapeDtypeStruct((n_seq * t_len, HW), BF16),
        scratch_shapes=[pltpu.VMEM((2 * t_len, HW), BF16)],
        compiler_params=_cparams("parallel", "arbitrary"),
    )(f_all, dft, c128, s128)


HALO = 16


def _prep2_kernel(cur_ref, prev_ref, next_ref, sm_ref, cw_ref, prm_ref, qkv_ref, g_ref):
    i = pl.program_id(0)
    j = jnp.maximum(i - CTX_TILES, 0) % LAT_TILES_PER_SEQ
    is_lat = i >= CTX_TILES
    has_prev = jnp.logical_and(is_lat, j > 0)
    has_next = jnp.logical_and(is_lat, j < LAT_TILES_PER_SEQ - 1)
    x = cur_ref[...].astype(F32)
    row = lax.broadcasted_iota(jnp.int32, x.shape, 0)
    prev_row = jnp.where(has_prev, prev_ref[HALO - 1:HALO, :].astype(F32), 0.0)
    next_row = jnp.where(has_next, next_ref[0:1, :].astype(F32), 0.0)
    xp = jnp.where(row == 0, prev_row, pltpu.roll(x, 1, 0))
    xn = jnp.where(row == TT - 1, next_row, pltpu.roll(x, TT - 1, 0))
    y = _silu(cw_ref[0:1, :] * xp + cw_ref[1:2, :] * x + cw_ref[2:3, :] * xn)
    for h in range(2 * H):
        seg = y[:, h * DH:(h + 1) * DH]
        seg = seg * lax.rsqrt(jnp.sum(seg * seg, axis=-1, keepdims=True) + EPS)
        if h < H:
            seg = seg * (DH ** -0.5)
        qkv_ref[:, h * DH:(h + 1) * DH] = seg.astype(BF16)
    qkv_ref[:, 2 * HW:3 * HW] = y[:, 2 * HW:3 * HW].astype(BF16)
    z = sm_ref[...] + prm_ref[1:2, :]
    lane = lax.broadcasted_iota(jnp.int32, z.shape, 1)
    g_log = -jnp.exp(prm_ref[0:1, :]) * _softplus(z)
    out = jnp.where(lane < 8, g_log,
                    jnp.where(lane < 16, _sigmoid(z), jnp.where(lane < 24, z, -_softplus(-z))))
    g_ref[...] = out


def _prep2(gqkv, sm, conv_w, prm):
    nb = N_TOK // HALO
    return pl.pallas_call(
        _prep2_kernel,
        grid=(N_TILES,),
        in_specs=[pl.BlockSpec((TT, 3 * HW), lambda i: (i, 0)),
                  pl.BlockSpec((HALO, 3 * HW), lambda i: (jnp.maximum(i * (TT // HALO) - 1, 0), 0)),
                  pl.BlockSpec((HALO, 3 * HW), lambda i: (jnp.minimum((i + 1) * (TT // HALO), nb - 1), 0)),
                  pl.BlockSpec((TT, LANES), lambda i: (i, 0)),
                  pl.BlockSpec((3, 3 * HW), lambda i: (0, 0)),
                  pl.BlockSpec((8, LANES), lambda i: (0, 0))],
        out_specs=[pl.BlockSpec((TT, 3 * HW), lambda i: (i, 0)),
                   pl.BlockSpec((TT, LANES), lambda i: (i, 0))],
        out_shape=[jax.ShapeDtypeStruct((N_TOK, 3 * HW), BF16),
                   jax.ShapeDtypeStruct((N_TOK, LANES), F32)],
        compiler_params=_cparams("parallel"),
    )(gqkv, gqkv, gqkv, sm, conv_w, prm)


def _scan_tables():
    blk_f = np.zeros(N_CHUNKS, np.int32)
    blk_b = np.zeros(N_CHUNKS, np.int32)
    seq = np.zeros(N_CHUNKS, np.int32)
    first = np.zeros(N_CHUNKS, np.int32)
    last = np.zeros(N_CHUNKS, np.int32)
    step = 0
    base = 0
    sid = 0
    for n_seq, t_len in ((B_CTX, T_CTX), (B_LAT, T_LAT)):
        nc = t_len // CH
        for _ in range(n_seq):
            for c in range(nc):
                blk_f[step] = base + c
                blk_b[step] = base + nc - 1 - c
                seq[step] = sid
                first[step] = int(c == 0)
                last[step] = int(c == nc - 1)
                step += 1
            base += nc
            sid += 1
    return blk_f, blk_b, seq, first, last


def _ones_where(mask):
    return jnp.where(mask, 1.0, 0.0).astype(BF16)


def _scan_order_mask(d, n=CH):
    row = lax.broadcasted_iota(jnp.int32, (n, n), 0)
    col = lax.broadcasted_iota(jnp.int32, (n, n), 1)
    return (col <= row) if d == 0 else (col >= row)


def _row_of(col_vec):
    return jnp.broadcast_to(col_vec, (col_vec.shape[0], LANES)).T[0:1, :]


HC = H * CH


def _unit_tri_inverses(lbs, row, col):
    same = lambda sh: jnp.right_shift(row, sh) == jnp.right_shift(col, sh)
    zero = jnp.zeros_like(lbs[0])
    eye = jnp.where(row == col, 1.0, 0.0).astype(BF16)
    ms = [jnp.where(same(3), -lb, zero) for lb in lbs]
    m2s = [_dot(m, m).astype(BF16) for m in ms]
    m4s = [_dot(m2, m2).astype(BF16) for m2 in m2s]
    xs = [eye + m for m in ms]
    xs = [(x.astype(F32) + _dot(x, m2)).astype(BF16) for x, m2 in zip(xs, m2s)]
    xs = [(x.astype(F32) + _dot(x, m4)).astype(BF16) for x, m4 in zip(xs, m4s)]
    for sh in (3, 4, 5):
        pair = jnp.logical_and(same(sh + 1), jnp.logical_not(same(sh)))
        exs = [_dot(jnp.where(pair, lb, zero), x).astype(BF16) for lb, x in zip(lbs, xs)]
        fills = [_dot(x, ex) for x, ex in zip(xs, exs)]
        xs = [jnp.where(pair, (-fill).astype(BF16), x) for fill, x in zip(fills, xs)]
    return xs


def _head_stack(ref, r0, base):
    return jnp.concatenate([ref[r0:r0 + CH, base + h * DH:base + (h + 1) * DH] for h in range(H)], axis=0)


def _gdn_wy_kernel(qkv_ref, g_ref, u_ref, w_ref, qg_ref, kg_ref, attn_ref, gl_ref):
    row = lax.broadcasted_iota(jnp.int32, (HC, HC), 0)
    col = lax.broadcasted_iota(jnp.int32, (HC, HC), 1)
    same_head = jnp.right_shift(row, 6) == jnp.right_shift(col, 6)
    lane_head = jnp.right_shift(lax.broadcasted_iota(jnp.int32, (CH, HC), 1), 6)
    subs = range(TT // CH)
    chains = [(s, d) for s in subs for d in range(2)]
    g = [g_ref[s * CH:(s + 1) * CH, :] for s in subs]
    q_st = [_head_stack(qkv_ref, s * CH, 0) for s in subs]
    k_st = [_head_stack(qkv_ref, s * CH, HW) for s in subs]
    v_st = [_head_stack(qkv_ref, s * CH, 2 * HW).astype(F32) for s in subs]
    kf = [k.astype(F32) for k in k_st]
    kk = [_dot_nt(k, k) for k in k_st]
    qk = [_dot_nt(q, k) for q, k in zip(q_st, k_st)]
    order = [_ones_where(_scan_order_mask(d)) for d in range(2)]
    g3 = [_split3(a) for a in g]
    cols = [_dot(order[d], g3[s][0]) + (_dot(order[d], g3[s][1]) + _dot(order[d], g3[s][2]))
            for s, d in chains]
    incl = [jnp.logical_and(same_head, (col <= row) if d == 0 else (col >= row)) for d in range(2)]
    strict = [jnp.logical_and(same_head, (col < row) if d == 0 else (col > row)) for d in range(2)]
    gc, beta, g_last, decay, l_hi, l_lo, egc, rhs = [], [], [], [], [], [], [], []
    for c, (s, d) in enumerate(chains):
        last = CH - 1 if d == 0 else 0
        lanes = [d * H + h for h in range(H)]
        gc.append(jnp.concatenate([cols[c][:, l:l + 1] for l in lanes], axis=0))
        beta.append(jnp.concatenate([g[s][:, 8 + l:9 + l] for l in lanes], axis=0))
        g_last.append(jnp.concatenate(
            [jnp.broadcast_to(cols[c][last:last + 1, l:l + 1], (CH, 1)) for l in lanes], axis=0))
        gr = _row_of(gc[c])
        decay.append(jnp.where(incl[d], jnp.exp(jnp.where(incl[d], gc[c] - gr, 0.0)), 0.0))
        lmat = jnp.where(strict[d], beta[c] * kk[s] * decay[c], 0.0)
        l_hi.append(lmat.astype(BF16))
        l_lo.append((lmat - l_hi[c].astype(F32)).astype(BF16))
        egc.append(jnp.exp(gc[c]))
        rhs.append(jnp.concatenate([v_st[s] * beta[c], kf[s] * (beta[c] * egc[c])], axis=1))
        gl_rows = [jnp.broadcast_to(cols[c][last:last + 1, l:l + 1], (1, LANES)) for l in lanes]
        gl_ref[d, s * 8:(s + 1) * 8, :] = jnp.concatenate(gl_rows + [jnp.zeros((8 - H, LANES), F32)], axis=0)
    t_inv = _unit_tri_inverses(l_hi, row, col)
    y = [_dot(t, r.astype(BF16)) for t, r in zip(t_inv, rhs)]
    ys = [_split2(a) for a in y]
    ly = [_dot(l_hi[c], ys[c][0]) + (_dot(l_hi[c], ys[c][1]) + _dot(l_lo[c], ys[c][0])) for c in range(len(chains))]
    resid = [(rhs[c] - (y[c] + ly[c])).astype(BF16) for c in range(len(chains))]
    y = [y[c] + _dot(t_inv[c], resid[c]) for c in range(len(chains))]
    for c, (s, d) in enumerate(chains):
        r0 = s * CH
        attn = qk[s] * decay[c]
        a64 = jnp.zeros((CH, HC), F32)
        for h in range(H):
            a64 = jnp.where(lane_head == h, attn[h * CH:(h + 1) * CH, :], a64)
        attn_ref[d, r0:r0 + CH, :] = a64.astype(BF16)
        qg = (q_st[s].astype(F32) * egc[c]).astype(BF16)
        kg = (kf[s] * jnp.exp(g_last[c] - gc[c])).astype(BF16)
        for h in range(H):
            rs = slice(h * CH, (h + 1) * CH)
            ls = slice(h * DH, (h + 1) * DH)
            u_ref[d, r0:r0 + CH, ls] = y[c][rs, 0:DH]
            w_ref[d, r0:r0 + CH, ls] = y[c][rs, DH:2 * DH].astype(BF16)
            qg_ref[d, r0:r0 + CH, ls] = qg[rs, :]
            kg_ref[d, r0:r0 + CH, ls] = kg[rs, :]


def _gdn_wy(qkv, gates):
    spec = lambda w: pl.BlockSpec((2, TT, w), lambda i: (0, i, 0))
    shp = lambda w, dt: jax.ShapeDtypeStruct((2, N_TOK, w), dt)
    return pl.pallas_call(
        _gdn_wy_kernel,
        grid=(N_TILES,),
        in_specs=[pl.BlockSpec((TT, 3 * HW), lambda i: (i, 0)), pl.BlockSpec((TT, LANES), lambda i: (i, 0))],
        out_specs=[spec(HW), spec(HW), spec(HW), spec(HW), spec(HC),
                   pl.BlockSpec((2, 8 * (TT // CH), LANES), lambda i: (0, i, 0))],
        out_shape=[shp(HW, F32), shp(HW, BF16), shp(HW, BF16), shp(HW, BF16), shp(HC, BF16),
                   jax.ShapeDtypeStruct((2, 8 * N_CHUNKS, LANES), F32)],
        compiler_params=_cparams("parallel"),
    )(qkv, gates)


def _gdn_rec_body(ins, of_ref, ob_ref, s_scr):
    chains = [(d, h) for d in range(2) for h in range(H)]
    ls = [slice(h * DH, (h + 1) * DH) for h in range(H)]
    s = [s_scr[d, h] for d, h in chains]
    sb = [a.astype(BF16) for a in s]
    ws = [_dot(ins[6 * d + 1][0, :, ls[h]], sb[c]) for c, (d, h) in enumerate(chains)]
    inter = [_dot(ins[6 * d + 2][0, :, ls[h]], sb[c]) for c, (d, h) in enumerate(chains)]
    vb = [(ins[6 * d][0, :, ls[h]] - ws[c]).astype(BF16) for c, (d, h) in enumerate(chains)]
    intra = [_dot(ins[6 * d + 4][0, :, h * CH:(h + 1) * CH], vb[c]) for c, (d, h) in enumerate(chains)]
    upd = [_dot_tn(ins[6 * d + 3][0, :, ls[h]], vb[c]) for c, (d, h) in enumerate(chains)]
    for c, (d, h) in enumerate(chains):
        o_ref = of_ref if d == 0 else ob_ref
        o_ref[:, ls[h]] = inter[c] + intra[c]
        s_scr[d, h] = s[c] * jnp.exp(ins[6 * d + 5][0, h:h + 1, :]) + upd[c]


def _mlstm_body(qkv_refs, g_refs, o_refs, c_scr, n_scr, m_scr):
    scale = DH ** -0.5
    chains = [(d, h) for d in range(2) for h in range(H)]
    nch = range(len(chains))
    incl = [_scan_order_mask(d) for d in range(2)]
    g = [g_refs[d][...] for d in range(2)]
    cols = [_dot_exact_lhs(_ones_where(incl[d]), g[d]) for d in range(2)]
    g_t = [jnp.concatenate([g[d], cols[d]], axis=0).T for d in range(2)]
    q = [qkv_refs[d][:, h * DH:(h + 1) * DH] for d, h in chains]
    k = [qkv_refs[d][:, HW + h * DH:HW + (h + 1) * DH] for d, h in chains]
    v = [qkv_refs[d][:, 2 * HW + h * DH:2 * HW + (h + 1) * DH] for d, h in chains]
    qk = [_dot_nt(q[c], k[c]) * scale for c in nch]
    cst = [c_scr[d, h] for d, h in chains]
    qc = [_dot(q[c], cst[c].astype(BF16)) for c in nch]
    lf = [24 + d * H + h for d, h in chains]
    li = [16 + d * H + h for d, h in chains]
    dd = [d for d, _ in chains]
    last = [CH - 1 if d == 0 else 0 for d in dd]
    bc = [cols[dd[c]][:, lf[c]:lf[c] + 1] for c in nch]
    br = [g_t[dd[c]][lf[c]:lf[c] + 1, CH:2 * CH] for c in nch]
    ig_c = [g[dd[c]][:, li[c]:li[c] + 1] for c in nch]
    ig_r = [g_t[dd[c]][li[c]:li[c] + 1, 0:CH] for c in nch]
    dlog = [jnp.where(incl[dd[c]], bc[c] - br[c] + ig_r[c], -jnp.inf) for c in nch]
    dmax = [jnp.max(a, axis=1, keepdims=True) for a in dlog]
    b_last = [bc[c][last[c]:last[c] + 1, :] for c in nch]
    w_max = [jnp.max(b_last[c] - br[c] + ig_r[c], axis=1, keepdims=True) for c in nch]
    m = [m_scr[d, h:h + 1, 0:1] for d, h in chains]
    n = [n_scr[d, h:h + 1, :] for d, h in chains]
    m_inter = [bc[c] + m[c] for c in nch]
    m_row = [jnp.maximum(dmax[c], m_inter[c]) for c in nch]
    s = [qk[c] * jnp.exp(dlog[c] - m_row[c]) for c in nch]
    sv = [_dot(s[c].astype(BF16), v[c]) for c in nch]
    e_inter = [jnp.exp(m_inter[c] - m_row[c]) for c in nch]
    qn = [jnp.sum(q[c].astype(F32) * n[c], axis=1, keepdims=True) for c in nch]
    s_sum = [jnp.sum(a, axis=1, keepdims=True) for a in s]
    den = [e_inter[c] * qn[c] + s_sum[c] for c in nch]
    inv = [1.0 / jnp.maximum(jnp.abs(den[c]), jnp.exp(-m_row[c])) for c in nch]
    m_new = [jnp.maximum(b_last[c] + m[c], w_max[c]) for c in nch]
    f_c = [jnp.exp(b_last[c] + m[c] - m_new[c]) for c in nch]
    kw = [k[c].astype(F32) * (jnp.exp(b_last[c] - bc[c] + ig_c[c] - m_new[c]) * scale) for c in nch]
    kv = [_dot_tn(kw[c].astype(BF16), v[c]) for c in nch]
    for c, (d, h) in enumerate(chains):
        o_refs[d][:, h * DH:(h + 1) * DH] = (e_inter[c] * qc[c] + sv[c]) * inv[c]
        n_scr[d, h:h + 1, :] = f_c[c] * n[c] + jnp.sum(kw[c], axis=0, keepdims=True)
        m_scr[d, h:h + 1, :] = jnp.broadcast_to(m_new[c], (1, LANES))
        c_scr[d, h] = f_c[c] * cst[c] + kv[c]


def _rec_kernel(bf_ref, bb_ref, seq_ref, first_ref, last_ref, *refs):
    gdn_ins, (qkvf_ref, gf_ref, qkvb_ref, gb_ref, s0_ref, c0_ref, n0_ref, m0_ref) = refs[:12], refs[12:20]
    of_ref, ob_ref, hf_ref, hb_ref, sfin_ref, cfin_ref, nfin_ref, mfin_ref = refs[20:28]
    s_scr, c_scr, n_scr, m_scr = refs[28:]
    t = pl.program_id(0)

    @pl.when(first_ref[t] == 1)
    def _():
        s_scr[...] = s0_ref[0]
        c_scr[...] = c0_ref[0]
        n_scr[...] = n0_ref[0]
        m_scr[...] = m0_ref[0]

    _gdn_rec_body(gdn_ins, of_ref, ob_ref, s_scr)
    _mlstm_body((qkvf_ref, qkvb_ref), (gf_ref, gb_ref), (hf_ref, hb_ref), c_scr, n_scr, m_scr)

    @pl.when(last_ref[t] == 1)
    def _():
        sfin_ref[0] = s_scr[...]
        cfin_ref[0] = c_scr[...]
        nfin_ref[0] = n_scr[...]
        mfin_ref[0] = m_scr[...]


def _rec_scan(tables, wy, qkv, gates, s0, c0, n0, m0):
    fwd = lambda t, bf, bb, *_: (bf[t], 0)
    bwd = lambda t, bf, bb, *_: (bb[t], 0)

    def dir_specs(d):
        idx = (lambda t, bf, bb, *_: (0, bf[t], 0)) if d == 0 else (lambda t, bf, bb, *_: (1, bb[t], 0))
        return [pl.BlockSpec((1, CH, HW), idx)] * 4 + [pl.BlockSpec((1, CH, HC), idx),
                                                        pl.BlockSpec((1, 8, LANES), idx)]
    vec = pl.BlockSpec((1, 2, H, LANES), lambda t, bf, bb, sq, *_: (sq[t], 0, 0, 0))
    mat = pl.BlockSpec((1, 2, H, DH, DH), lambda t, bf, bb, sq, *_: (sq[t], 0, 0, 0, 0))
    tok = jax.ShapeDtypeStruct((N_TOK, HW), F32)
    mats = jax.ShapeDtypeStruct((N_SEQ, 2, H, DH, DH), F32)
    vecs = jax.ShapeDtypeStruct((N_SEQ, 2, H, LANES), F32)
    return pl.pallas_call(
        _rec_kernel,
        grid_spec=pltpu.PrefetchScalarGridSpec(
            num_scalar_prefetch=5, grid=(N_CHUNKS,),
            in_specs=dir_specs(0) + dir_specs(1)
            + [pl.BlockSpec((CH, 3 * HW), fwd), pl.BlockSpec((CH, LANES), fwd),
               pl.BlockSpec((CH, 3 * HW), bwd), pl.BlockSpec((CH, LANES), bwd), mat, mat, vec, vec],
            out_specs=[pl.BlockSpec((CH, HW), fwd), pl.BlockSpec((CH, HW), bwd),
                       pl.BlockSpec((CH, HW), fwd), pl.BlockSpec((CH, HW), bwd), mat, mat, vec, vec],
            scratch_shapes=[pltpu.VMEM((2, H, DH, DH), F32), pltpu.VMEM((2, H, DH, DH), F32),
                            pltpu.VMEM((2, H, LANES), F32), pltpu.VMEM((2, H, LANES), F32)]),
        out_shape=[tok, tok, tok, tok, mats, mats, vecs, vecs],
        compiler_params=_cparams("arbitrary"),
    )(*tables, *wy, *wy, qkv, gates, qkv, gates, s0, c0, n0, m0)


def _post_kernel(tbl_ref, x_ref, mod_ref, yfc_ref, yfl_ref, of_ref, ob_ref, hf_ref, hb_ref,
                 gz_ref, mo_ref, gates_ref, gn_ref, mn_ref, wbf_ref, wbg_ref, wbm_ref, wo_ref,
                 n2_ref, rwh_ref, rwl_ref, rb_ref,
                 xo_ref, h2_ref, info_ref, infot_ref, stats_ref, cnt_scr):
    i = pl.program_id(0)

    @pl.when(i == 0)
    def _():
        cnt_scr[...] = jnp.zeros_like(cnt_scr)

    yf = jnp.where(i < CTX_TILES, yfc_ref[...], yfl_ref[...])
    og = of_ref[...] + ob_ref[...]
    hm = hf_ref[...] + hb_ref[...]
    og_parts, hm_parts = [], []
    for h in range(H):
        sl = slice(h * DH, (h + 1) * DH)
        og_parts.append(_rms(og[:, sl]) * gn_ref[...] * _silu(gz_ref[:, sl].astype(F32)))
        hm_parts.append(_rms(hm[:, sl]) * mn_ref[:, sl] * _sigmoid(mo_ref[:, sl].astype(F32)))
    ogb = jnp.concatenate(og_parts, axis=1).astype(BF16)
    hmb = jnp.concatenate(hm_parts, axis=1).astype(BF16)
    y = (_sigmoid(gates_ref[:, 0:D].astype(F32)) * _dot(yf, wbf_ref[...])
         + _sigmoid(gates_ref[:, D:2 * D].astype(F32)) * _dot(ogb, wbg_ref[...])
         + _sigmoid(gates_ref[:, 2 * D:3 * D].astype(F32)) * _dot(hmb, wbm_ref[...]))
    x = x_ref[...] + mod_ref[0, 2:3, :] * _dot(y.astype(BF16), wo_ref[...])
    xo_ref[...] = x
    h2 = _rms(x) * n2_ref[...]
    h2 = h2 * (1.0 + mod_ref[0, 4:5, :]) + mod_ref[0, 3:4, :]
    hb2, hl2 = _split2(h2)
    h2_ref[...] = hb2
    logits = (_dot(hb2, rwh_ref[...]) + (_dot(hb2, rwl_ref[...]) + _dot(hl2, rwh_ref[...]))
              + rb_ref[...])
    lane = lax.broadcasted_iota(jnp.int32, logits.shape, 1).astype(F32)
    work = jnp.where(lane < N_EXP, logits, -jnp.inf)
    sel = jnp.zeros(logits.shape, F32)
    hits, exps = [], []
    top0 = None
    denom = None
    for kk in range(TOP_K):
        mx = jnp.max(work, axis=1, keepdims=True)
        first = jnp.min(jnp.where(work == mx, lane, float(LANES)), axis=1, keepdims=True)
        hit = lane == first
        if kk == 0:
            top0 = mx
        e = jnp.exp(mx - top0)
        denom = e if kk == 0 else denom + e
        hits.append(hit)
        exps.append(e)
        sel = jnp.where(hit, 1.0, sel)
        work = jnp.where(hit, -jnp.inf, work)
    r_i = lax.broadcasted_iota(jnp.int32, (TT, TT), 0)
    c_i = lax.broadcasted_iota(jnp.int32, (TT, TT), 1)
    within = _dot(_ones_where(c_i < r_i), sel.astype(BF16))
    tile_cnt = jnp.sum(sel, axis=0, keepdims=True)
    a_i = lax.broadcasted_iota(jnp.int32, (LANES, LANES), 0)
    b_i = lax.broadcasted_iota(jnp.int32, (LANES, LANES), 1)
    tile_off = _dot(jnp.broadcast_to(tile_cnt, (8, LANES)).astype(BF16), _ones_where(a_i < b_i))[0:1, :]
    local = tile_off + within
    info = jnp.zeros(logits.shape, F32)
    for kk in range(TOP_K):
        pos = jnp.sum(jnp.where(hits[kk], local, 0.0), axis=1, keepdims=True)
        info = jnp.where(lane == float(kk), pos, info)
        info = jnp.where(lane == float(TOP_K + kk), exps[kk] / denom, info)
    info_ref[...] = info
    infot_ref[...] = info.T[0:8, :]
    stats_ref[0] = jnp.concatenate([cnt_scr[0:1, :], tile_cnt, tile_off, jnp.zeros((5, LANES), F32)], axis=0)
    cnt_scr[0:1, :] = cnt_scr[0:1, :] + tile_cnt


def _post(cond_tbl, x, mod, yfc, yfl, o_f, o_b, h_f, h_b, gz, mo, gates, gn, mn, wbf, wbg, wbm, wo,
          n2, rwh, rwl, rb):
    tile = lambda w: pl.BlockSpec((TT, w), lambda i, t: (i, 0))
    const = lambda shape: pl.BlockSpec(shape, lambda i, t: (0,) * len(shape))
    in_specs = [tile(D), pl.BlockSpec((1, N_MOD, D), lambda i, t: (t[i], 0, 0)),
                pl.BlockSpec((TT, HW), lambda i, t: (jnp.minimum(i, CTX_TILES - 1), 0)),
                pl.BlockSpec((TT, HW), lambda i, t: (jnp.maximum(i - CTX_TILES, 0), 0)),
                tile(HW), tile(HW), tile(HW), tile(HW),
                tile(HW), tile(HW), tile(3 * D),
                const((1, DH)), const((1, HW)), const((HW, D)), const((HW, D)), const((HW, D)),
                const((D, D)), const((1, D)), const((D, LANES)), const((D, LANES)), const((1, LANES))]
    return pl.pallas_call(
        _post_kernel,
        grid_spec=pltpu.PrefetchScalarGridSpec(
            num_scalar_prefetch=1, grid=(N_TILES,), in_specs=in_specs,
            out_specs=[tile(D), tile(D), tile(LANES), pl.BlockSpec((8, TT), lambda i, t: (0, i)),
                       pl.BlockSpec((1, 8, LANES), lambda i, t: (i, 0, 0))],
            scratch_shapes=[pltpu.VMEM((8, LANES), F32)]),
        out_shape=[jax.ShapeDtypeStruct((N_TOK, D), F32), jax.ShapeDtypeStruct((N_TOK, D), BF16),
                   jax.ShapeDtypeStruct((N_TOK, LANES), F32), jax.ShapeDtypeStruct((8, N_TOK), F32),
                   jax.ShapeDtypeStruct((N_TILES, 8, LANES), F32)],
        compiler_params=_cparams("arbitrary"),
    )(cond_tbl, x, mod, yfc, yfl, o_f, o_b, h_f, h_b, gz, mo, gates, gn, mn, wbf, wbg, wbm, wo,
      n2, rwh, rwl, rb)


PAIRS = TT * TOP_K
RUN_BITS = tuple(1 << b for b in range(8, -1, -1))


def _slot_rows(start_slot, n_slots):
    return pl.ds(pl.multiple_of(start_slot * 8, 8), n_slots * 8)


def _for_each_run(base, cnt_ref, fn):
    def body(e, carry):
        cnt = cnt_ref[base + e]
        for bit in RUN_BITS:
            @pl.when((cnt & bit) != 0)
            def _():
                fn(base + e, cnt & (-2 * bit), bit)
        return carry
    lax.fori_loop(0, N_EXP, body, 0)


def _zero_fill(ps_ref, pc_ref, tail_ref, xs_hbm, zbuf, zsem, start):
    def piece(dst_slot, n):
        cp = pltpu.make_async_copy(zbuf.at[pl.ds(0, n * 8)], xs_hbm.at[_slot_rows(dst_slot, n)], zsem)
        if start:
            cp.start()
        else:
            cp.wait()

    def pad(e, carry):
        cnt = pc_ref[e]
        for bit in RUN_BITS[1:]:
            @pl.when((cnt & bit) != 0)
            def _():
                piece(ps_ref[e] + (cnt & (-2 * bit)), bit)
        return carry

    def tail(t, carry):
        piece(tail_ref[0] + t * TM, TM)
        return carry

    lax.fori_loop(0, N_EXP, pad, 0)
    lax.fori_loop(0, tail_ref[1], tail, 0)


def _dispatch_kernel(ss_ref, cnt_ref, toff_ref, ps_ref, pc_ref, tail_ref, h2_ref, infot_ref, xs_hbm,
                     stage, zbuf, sem, zsem):
    i = pl.program_id(0)
    buf = i % 2
    whole = lambda b: pltpu.make_async_copy(stage.at[b], xs_hbm.at[pl.ds(0, PAIRS * 8)], sem.at[b])

    @pl.when(i >= 2)
    def _():
        whole(buf).wait()

    r = lax.broadcasted_iota(jnp.int32, (PAIRS, TT), 0).astype(F32)
    pick = r == infot_ref[0:1, :]
    for kk in range(1, TOP_K):
        pick = jnp.logical_or(pick, r == infot_ref[kk:kk + 1, :])
    rows = _dot(_ones_where(pick), h2_ref[...])
    for cc in range(8):
        stage[buf, pl.ds(cc, PAIRS, stride=8), :] = rows[:, cc * LANES:(cc + 1) * LANES]

    def send(j, o, n):
        pltpu.make_async_copy(stage.at[buf, _slot_rows(toff_ref[j] + o, n)],
                              xs_hbm.at[_slot_rows(ss_ref[j] + o, n)], sem.at[buf]).start()

    _for_each_run(i * N_EXP, cnt_ref, send)

    @pl.when(i == 0)
    def _():
        zbuf[...] = jnp.zeros_like(zbuf)
        _zero_fill(ps_ref, pc_ref, tail_ref, xs_hbm, zbuf, zsem, True)

    @pl.when(i == N_TILES - 1)
    def _():
        whole(1 - buf).wait()
        whole(buf).wait()
        _zero_fill(ps_ref, pc_ref, tail_ref, xs_hbm, zbuf, zsem, False)


def _dispatch(slot_start, seg_cnt, seg_off, pad_start, pad_cnt, tail, h2, info_t):
    return pl.pallas_call(
        _dispatch_kernel,
        grid_spec=pltpu.PrefetchScalarGridSpec(
            num_scalar_prefetch=6, grid=(N_TILES,),
            in_specs=[pl.BlockSpec((TT, D), lambda i, *_: (i, 0)), pl.BlockSpec((8, TT), lambda i, *_: (0, i))],
            out_specs=pl.BlockSpec(memory_space=pl.ANY),
            scratch_shapes=[pltpu.VMEM((2, PAIRS * 8, LANES), F32), pltpu.VMEM((TM * 8, LANES), F32),
                            pltpu.SemaphoreType.DMA((2,)), pltpu.SemaphoreType.DMA(())]),
        out_shape=jax.ShapeDtypeStruct((S_MAX * 8, LANES), F32),
        compiler_params=_cparams("arbitrary"),
    )(slot_start, seg_cnt, seg_off, pad_start, pad_cnt, tail, h2, info_t)


N_MAT = 3


def _gmm_kernel(te_ref, nu_ref, valid_ref, nxt_ref, x_ref, wg_hbm, bg_ref, wu_hbm, bu_ref, wd_hbm, bd_ref, o_ref,
                wbuf, wg_scr, wu_scr, wd_scr, slot_ref, wsem, *, layer):
    i = pl.program_id(0)
    used = i < nu_ref[0]

    def fetch(e, slot, start):
        for m, w_hbm in enumerate((wg_hbm, wu_hbm, wd_hbm)):
            cp = pltpu.make_async_copy(w_hbm.at[layer, e], wbuf.at[slot * N_MAT + m], wsem.at[slot * N_MAT + m])
            if start:
                cp.start()
            else:
                cp.wait()

    @pl.when(i == 0)
    def _():
        slot_ref[0] = 0
        fetch(te_ref[0], 0, True)

    @pl.when(used)
    def _():
        e = te_ref[i]
        e_prev = te_ref[jnp.maximum(i - 1, 0)]

        @pl.when(jnp.logical_or(i == 0, e != e_prev))
        def _():
            slot = slot_ref[0]
            fetch(e, slot, False)
            wg_scr[...] = wbuf[slot * N_MAT].astype(BF16)
            wu_scr[...] = wbuf[slot * N_MAT + 1].astype(BF16)
            wd_scr[...] = wbuf[slot * N_MAT + 2].astype(BF16)

            @pl.when(nxt_ref[i] >= 0)
            def _():
                fetch(nxt_ref[i], 1 - slot, True)

            slot_ref[0] = 1 - slot

        x = jnp.concatenate([x_ref[pl.ds(cc, TM, stride=8), :] for cc in range(8)], axis=1)
        row = lax.broadcasted_iota(jnp.int32, (TM, 1), 0)
        x = jnp.where(row < valid_ref[i], x, 0.0).astype(BF16)
        acts = []
        for j in range(D // FF_CHUNK):
            cs = slice(j * FF_CHUNK, (j + 1) * FF_CHUNK)
            gate = jnp.minimum(_dot(x, wg_scr[:, cs]) + bg_ref[0, 0, :, cs], SWIGLU_LIMIT)
            up = jnp.clip(_dot(x, wu_scr[:, cs]) + bu_ref[0, 0, :, cs], -SWIGLU_LIMIT, SWIGLU_LIMIT)
            acts.append(((up + 1.0) * gate * _sigmoid(SWIGLU_ALPHA * gate)).astype(BF16))
        y = _dot(jnp.concatenate(acts, axis=1), wd_scr[...]) + bd_ref[0, 0]
        for cc in range(8):
            o_ref[pl.ds(cc, TM, stride=8), :] = y[:, cc * LANES:(cc + 1) * LANES]

    @pl.when(jnp.logical_not(used))
    def _():
        o_ref[...] = jnp.zeros_like(o_ref)


def _gmm(layer, tile_exp, n_used, valid, nxt_exp, xs, wg, bg, wu, bu, wd, bd):
    wspec = pl.BlockSpec(memory_space=pl.ANY)
    bspec = pl.BlockSpec((1, 1, 1, D), lambda i, te, *_: (layer, te[i], 0, 0))
    return pl.pallas_call(
        functools.partial(_gmm_kernel, layer=layer),
        grid_spec=pltpu.PrefetchScalarGridSpec(
            num_scalar_prefetch=4, grid=(NT_MAX,),
            in_specs=[pl.BlockSpec((TM * 8, LANES), lambda i, te, nu, *_: (jnp.minimum(i, nu[0] - 1), 0)),
                      wspec, bspec, wspec, bspec, wspec, bspec],
            out_specs=pl.BlockSpec((TM * 8, LANES), lambda i, *_: (i, 0)),
            scratch_shapes=[pltpu.VMEM((2 * N_MAT, D, D), F32)] + [pltpu.VMEM((D, D), BF16)] * N_MAT
            + [pltpu.SMEM((1,), jnp.int32), pltpu.SemaphoreType.DMA((2 * N_MAT,))]),
        out_shape=jax.ShapeDtypeStruct((S_MAX * 8, LANES), F32),
        compiler_params=_cparams("arbitrary"),
    )(tile_exp, n_used, valid, nxt_exp, xs, wg, bg.reshape(DEPTH, N_EXP, 1, D), wu,
      bu.reshape(DEPTH, N_EXP, 1, D), wd, bd.reshape(DEPTH, N_EXP, 1, D))


def _combine_kernel(tbl_ref, ss_ref, cnt_ref, toff_ref, y_hbm, info_ref, x_ref, mod_ref, g_ref, *refs,
                    final):
    outs, (ybuf, sem) = refs[:-2], refs[-2:]
    i = pl.program_id(0)
    buf = i % 2

    def fetch(tile, b):
        def recv(j, o, n):
            pltpu.make_async_copy(y_hbm.at[_slot_rows(ss_ref[j] + o, n)],
                                  ybuf.at[b, _slot_rows(toff_ref[j] + o, n)], sem.at[b]).start()
        _for_each_run(tile * N_EXP, cnt_ref, recv)

    @pl.when(i == 0)
    def _():
        fetch(0, 0)

    @pl.when(i + 1 < N_TILES)
    def _():
        fetch(i + 1, 1 - buf)

    r = lax.broadcasted_iota(jnp.int32, (TT, PAIRS), 1).astype(F32)
    wm = jnp.zeros((TT, PAIRS), F32)
    for kk in range(TOP_K):
        wm = jnp.where(r == info_ref[:, kk:kk + 1], info_ref[:, TOP_K + kk:TOP_K + kk + 1], wm)
    pltpu.make_async_copy(y_hbm.at[pl.ds(0, PAIRS * 8)], ybuf.at[buf], sem.at[buf]).wait()
    y = jnp.concatenate([ybuf[buf, pl.ds(cc, PAIRS, stride=8), :] for cc in range(8)], axis=1)
    x = x_ref[...] + mod_ref[0, 5:6, :] * _dot(wm.astype(BF16), y.astype(BF16))
    if not final:
        outs[0][...] = x
    else:
        res = _rms(x) * g_ref[...]

        @pl.when(i < CTX_TILES)
        def _():
            outs[0][...] = res

        @pl.when(i >= CTX_TILES)
        def _():
            outs[1][...] = res


def _combine(final, cond_tbl, slot_start, seg_cnt, seg_off, y_sorted, info, x, mod, g):
    tile = lambda w: pl.BlockSpec((TT, w), lambda i, *_: (i, 0))
    if final:
        out_specs = [pl.BlockSpec((TT, D), lambda i, *_: (jnp.minimum(i, CTX_TILES - 1), 0)),
                     pl.BlockSpec((TT, D), lambda i, *_: (jnp.maximum(i - CTX_TILES, 0), 0))]
        out_shape = [jax.ShapeDtypeStruct((N_CTX, D), F32), jax.ShapeDtypeStruct((N_LAT, D), F32)]
    else:
        out_specs = [tile(D)]
        out_shape = [jax.ShapeDtypeStruct((N_TOK, D), F32)]
    return pl.pallas_call(
        functools.partial(_combine_kernel, final=final),
        grid_spec=pltpu.PrefetchScalarGridSpec(
            num_scalar_prefetch=4, grid=(N_TILES,),
            in_specs=[pl.BlockSpec(memory_space=pl.ANY), tile(LANES), tile(D),
                      pl.BlockSpec((1, N_MOD, D), lambda i, t, *_: (t[i], 0, 0)),
                      pl.BlockSpec((1, D), lambda i, *_: (0, 0))],
            out_specs=out_specs,
            scratch_shapes=[pltpu.VMEM((2, PAIRS * 8, LANES), F32), pltpu.SemaphoreType.DMA((2,))]),
        out_shape=out_shape,
        compiler_params=_cparams("arbitrary"),
    )(cond_tbl, slot_start, seg_cnt, seg_off, y_sorted, info, x, mod, g)


def _dft_consts(t_len):
    k = np.arange(t_len, dtype=np.int64)
    ang = 2.0 * np.pi * ((k[:, None] * k[None, :]) % t_len).astype(np.float64) / t_len
    return np.concatenate([np.cos(ang), -np.sin(ang)], axis=1).astype(np.float32)


def _pos_table():
    quarter = D // 4
    omega = 1.0 / (10000.0 ** (np.arange(quarter, dtype=np.float32) / np.float32(quarter)))
    omega = omega.astype(np.float32).astype(np.float64)
    t = np.arange(T_LAT)
    ang_r = (t // GRID_W).astype(np.float64)[:, None] * omega
    ang_c = (t % GRID_W).astype(np.float64)[:, None] * omega
    return np.concatenate([np.sin(ang_r), np.cos(ang_r), np.sin(ang_c), np.cos(ang_c)],
                          axis=-1).astype(np.float32)


def _repack_w_in(w):
    a, b, c, d = 2560, 2576, 4624, 4640
    main = jnp.concatenate([w[..., :a], w[..., b:c], w[..., d:]], axis=-1).astype(BF16)
    small = jnp.concatenate([w[..., a:b], w[..., c:d], jnp.zeros((DEPTH, D, LANES - 32), F32)], axis=-1)
    hi = small.astype(BF16)
    lo = (small - hi.astype(F32)).astype(BF16)
    return main, hi, lo


def _route_glue(stats):
    seg_base = stats[:, 0, :N_EXP].astype(jnp.int32)
    seg_cnt = stats[:, 1, :N_EXP].astype(jnp.int32)
    seg_off = stats[:, 2, :N_EXP].astype(jnp.int32)
    counts = seg_base[-1] + seg_cnt[-1]
    region = ((counts + TM - 1) // TM) * TM
    ends = jnp.cumsum(region)
    off = ends - region
    n_used = ends[-1] // TM
    tile_start = jnp.arange(NT_MAX, dtype=jnp.int32) * TM
    tile_exp = jnp.minimum(jnp.sum((tile_start[:, None] >= ends[None, :]).astype(jnp.int32), axis=1), N_EXP - 1)
    used = tile_start < ends[-1]
    tile_exp = jnp.where(used, tile_exp, jnp.max(jnp.where(used, tile_exp, 0)))
    run_end = jnp.sum(jnp.where(tile_exp[:, None] == jnp.arange(N_EXP)[None, :], (off + counts)[None, :], 0), axis=1)
    valid = jnp.clip(run_end - tile_start, 0, TM)
    experts = jnp.arange(N_EXP, dtype=jnp.int32)
    later = jnp.logical_and(experts[None, :] > experts[:, None], counts[None, :] > 0)
    next_used = jnp.min(jnp.where(later, experts[None, :], N_EXP), axis=1)
    next_used = jnp.where(next_used < N_EXP, next_used, -1)
    nxt_exp = jnp.sum(jnp.where(tile_exp[:, None] == experts[None, :], next_used[None, :], 0), axis=1)
    slot_start = off[None, :] + seg_base
    tail = jnp.stack([ends[-1], (S_MAX - ends[-1]) // TM]).astype(jnp.int32)
    return (tile_exp.astype(jnp.int32), n_used.reshape(1).astype(jnp.int32), valid.astype(jnp.int32),
            nxt_exp.astype(jnp.int32), slot_start.reshape(-1), seg_cnt.reshape(-1), seg_off.reshape(-1),
            (off + counts).astype(jnp.int32), (region - counts).astype(jnp.int32), tail)


def kernel(x_prompt, x_sample, state_gdn, state_mlstm_c, state_mlstm_n, state_mlstm_m, c, c_ctx,
           w_ada, b_ada, norm1_g, norm2_g, w_in, gdn_conv_w, gdn_a_log, gdn_dt_bias, gdn_norm_g,
           mlstm_i_bias, mlstm_f_bias, mlstm_norm_g, w_branch_f, w_branch_g, w_branch_m, w_out,
           router_w, router_b, exp_w_gate, exp_b_gate, exp_w_up, exp_b_up, exp_w_down, exp_b_down,
           final_norm_g):
    cond_tbl = jnp.asarray(np.concatenate([np.zeros(CTX_TILES, np.int32),
                                           1 + np.arange(N_TILES - CTX_TILES, dtype=np.int32) // LAT_TILES_PER_SEQ]))
    tables = tuple(jnp.asarray(t) for t in _scan_tables())
    cond8 = jnp.concatenate([c_ctx[None, :], c, jnp.zeros((8 - 1 - B_LAT, D), F32)], axis=0)
    mod_all = _ada(cond8, w_ada, b_ada).reshape(DEPTH, 8, N_MOD, D)

    pos = jnp.asarray(_pos_table())
    dft_c = jnp.asarray(_dft_consts(T_CTX)).astype(BF16)
    dft_l = jnp.asarray(_dft_consts(T_LAT)).astype(BF16)
    ang = 2.0 * np.pi * ((np.arange(DH)[:, None] * np.arange(DH)[None, :]) % DH) / DH
    c128 = jnp.asarray(np.cos(ang).astype(np.float32)).astype(BF16)
    s128 = jnp.asarray(np.sin(ang).astype(np.float32)).astype(BF16)

    zeros_s = jnp.zeros((B_CTX, 2, H, DH, DH), F32)
    zeros_v = jnp.zeros((B_CTX, 2, H, LANES), F32)
    x = None
    gdn_states, c_states, n_states, m_states = [], [], [], []
    y_prompt = y_sample = None
    wm, wsh, wsl = _repack_w_in(w_in)
    for l in range(DEPTH):
        mod = mod_all[l]
        if l == 0:
            outs = _pre(l, (x_prompt.reshape(N_CTX, D), x_sample.reshape(N_LAT, D), pos), mod,
                        norm1_g[l][None, :], wm, wsh, wsl, cond_tbl)
            x, outs = outs[0], outs[1:]
        else:
            outs = _pre(l, x, mod, norm1_g[l][None, :], wm, wsh, wsl, cond_tbl)
        f_all, gqkv, gz, mqkv, mo, gates, sm = outs

        yfc = _fourier(f_all, dft_c, c128, s128, B_CTX, T_CTX, 0)
        yfl = _fourier(f_all, dft_l, c128, s128, B_LAT, T_LAT, N_CTX)

        prm = jnp.zeros((8, LANES), F32)
        prm = prm.at[0, 0:8].set(gdn_a_log[l].reshape(-1))
        prm = prm.at[1, 0:8].set(gdn_dt_bias[l].reshape(-1))
        prm = prm.at[1, 16:24].set(mlstm_i_bias[l].reshape(-1))
        prm = prm.at[1, 24:32].set(mlstm_f_bias[l].reshape(-1))
        qkv_c, gts = _prep2(gqkv, sm, gdn_conv_w[l], prm)

        s0 = jnp.concatenate([zeros_s, state_gdn[:, l]], axis=0)
        c0 = jnp.concatenate([zeros_s, state_mlstm_c[:, l]], axis=0)
        n0 = jnp.concatenate([zeros_v, state_mlstm_n[:, l]], axis=0)
        m0 = jnp.concatenate([zeros_v, jnp.broadcast_to(state_mlstm_m[:, l][..., None],
                                                         (B_LAT, 2, H, LANES))], axis=0)
        o_f, o_b, h_f, h_b, s_fin, c_fin, n_fin, m_fin = _rec_scan(
            tables, _gdn_wy(qkv_c, gts), mqkv, gts, s0, c0, n0, m0)
        gdn_states.append(s_fin[:B_CTX])
        c_states.append(c_fin[:B_CTX])
        n_states.append(n_fin[:B_CTX])
        m_states.append(m_fin[:B_CTX, :, :, 0])

        rw = jnp.concatenate([router_w[l], jnp.zeros((D, LANES - N_EXP), F32)], axis=1)
        rwh = rw.astype(BF16)
        rwl = (rw - rwh.astype(F32)).astype(BF16)
        rb = jnp.concatenate([router_b[l], jnp.zeros((LANES - N_EXP,), F32)])[None, :]
        x, h2, info, info_t, stats = _post(
            cond_tbl, x, mod, yfc, yfl, o_f, o_b, h_f, h_b, gz, mo, gates,
            gdn_norm_g[l][None, :], mlstm_norm_g[l][None, :],
            w_branch_f[l].astype(BF16), w_branch_g[l].astype(BF16), w_branch_m[l].astype(BF16),
            w_out[l].astype(BF16), norm2_g[l][None, :], rwh, rwl, rb)

        (tile_exp, n_used, valid, nxt_exp, slot_start, seg_cnt, seg_off, pad_start, pad_cnt,
         tail) = _route_glue(stats)
        xs_sorted = _dispatch(slot_start, seg_cnt, seg_off, pad_start, pad_cnt, tail, h2, info_t)
        y_sorted = _gmm(l, tile_exp, n_used, valid, nxt_exp, xs_sorted, exp_w_gate, exp_b_gate, exp_w_up,
                        exp_b_up, exp_w_down, exp_b_down)
        res = _combine(l + 1 == DEPTH, cond_tbl, slot_start, seg_cnt, seg_off, y_sorted, info, x, mod,
                       final_norm_g[None, :])
        if l + 1 < DEPTH:
            x = res[0]
        else:
            y_prompt, y_sample = res

    return (y_prompt.reshape(B_CTX, T_CTX, D), y_sample.reshape(B_LAT, T_LAT, D),
            jnp.stack(gdn_states, axis=1), jnp.stack(c_states, axis=1),
            jnp.stack(n_states, axis=1), jnp.stack(m_states, axis=1))
```

```python
import functools
import math

import numpy as np
import jax
import jax.numpy as jnp
from jax import lax
from jax.experimental import pallas as pl
from jax.experimental.pallas import tpu as pltpu

F32 = jnp.float32
BF16 = jnp.bfloat16

D = 1024
DEPTH = 2
B_CTX, T_CTX = 16, 256
B_LAT, T_LAT = 4, 2048
N_CTX = B_CTX * T_CTX
N_LAT = B_LAT * T_LAT
N_TOK = N_CTX + N_LAT
N_SEQ = B_CTX + B_LAT
GRID_W = 64
H = 4
DH = 128
HW = H * DH
CH = 64
TT = 256
N_TILES = N_TOK // TT
CTX_TILES = N_CTX // TT
LAT_TILES_PER_SEQ = T_LAT // TT
N_CHUNKS = N_TOK // CH
N_EXP = 32
TOP_K = 4
N_MOD = 6
EPS = 1e-6
SWIGLU_ALPHA = 1.702
SWIGLU_LIMIT = 7.0
LANES = 128
TM = 256
FF_CHUNK = 256
S_MAX = N_TOK * TOP_K + N_EXP * (TM - 1)
S_MAX = ((S_MAX + TM - 1) // TM) * TM
NT_MAX = S_MAX // TM

W_MAIN = 512 + 1536 + 512 + 1536 + 512 + 3072


def _dot(a, b):
    return jnp.dot(a, b, preferred_element_type=F32)


def _dot_nt(a, b):
    return lax.dot_general(a, b, (((1,), (1,)), ((), ())), preferred_element_type=F32)


def _dot_tn(a, b):
    return lax.dot_general(a, b, (((0,), (0,)), ((), ())), preferred_element_type=F32)


def _split2(a):
    hi = a.astype(BF16)
    lo = (a - hi.astype(F32)).astype(BF16)
    return hi, lo


def _split3(a):
    hi = a.astype(BF16)
    r = a - hi.astype(F32)
    mid = r.astype(BF16)
    lo = (r - mid.astype(F32)).astype(BF16)
    return hi, mid, lo


def _dot3(a, b):
    ah, al = _split2(a)
    bh, bl = _split2(b)
    return _dot(ah, bh) + (_dot(ah, bl) + _dot(al, bh))


def _dot_exact_lhs(a_bf16, b):
    bh, bm, bl = _split3(b)
    return _dot(a_bf16, bh) + (_dot(a_bf16, bm) + _dot(a_bf16, bl))


def _sigmoid(x):
    return 0.5 * jnp.tanh(0.5 * x) + 0.5


def _silu(x):
    return x * _sigmoid(x)


def _softplus(x):
    return jnp.maximum(x, 0.0) + jnp.log(1.0 + jnp.exp(-jnp.abs(x)))


def _rms(x):
    return x * lax.rsqrt(jnp.mean(x * x, axis=-1, keepdims=True) + EPS)


def _cparams(*sem):
    return pltpu.CompilerParams(dimension_semantics=tuple(sem))


def _ada_kernel(c_ref, w_ref, b_ref, o_ref):
    o_ref[0] = _dot3(_silu(c_ref[...]), w_ref[0]) + b_ref[0]


def _ada(cond8, w_ada, b_ada):
    nb = 1536
    return pl.pallas_call(
        _ada_kernel,
        grid=(DEPTH, N_MOD * D // nb),
        in_specs=[pl.BlockSpec((8, D), lambda l, j: (0, 0)),
                  pl.BlockSpec((1, D, nb), lambda l, j: (l, 0, j)),
                  pl.BlockSpec((1, 1, nb), lambda l, j: (l, 0, j))],
        out_specs=pl.BlockSpec((1, 8, nb), lambda l, j: (l, 0, j)),
        out_shape=jax.ShapeDtypeStruct((DEPTH, 8, N_MOD * D), F32),
        compiler_params=_cparams("parallel", "parallel"),
    )(cond8, w_ada, b_ada.reshape(DEPTH, 1, N_MOD * D))


def _pre_body(x, mod_ref, n1_ref, wm_ref, wsh_ref, wsl_ref, outs):
    f_ref, gqkv_ref, gz_ref, mqkv_ref, mo_ref, gates_ref, sm_ref = outs
    h = _rms(x) * n1_ref[...]
    h = h * (1.0 + mod_ref[0, 1:2, :]) + mod_ref[0, 0:1, :]
    hb, hl = _split2(h)
    off = 0
    for ref, width in ((f_ref, 512), (gqkv_ref, 1536), (gz_ref, 512), (mqkv_ref, 1536),
                       (mo_ref, 512), (gates_ref, 3072)):
        ref[...] = _dot(hb, wm_ref[0, :, off:off + width]).astype(BF16)
        off += width
    sm_ref[...] = _dot(hb, wsh_ref[0]) + (_dot(hb, wsl_ref[0]) + _dot(hl, wsh_ref[0]))


def _pre_first_kernel(tbl_ref, xp_ref, xs_ref, pos_ref, mod_ref, n1_ref, wm_ref, wsh_ref, wsl_ref,
                      x_out_ref, *outs):
    i = pl.program_id(0)
    x = jnp.where(i < CTX_TILES, xp_ref[...], xs_ref[...] + pos_ref[...])
    x_out_ref[...] = x
    _pre_body(x, mod_ref, n1_ref, wm_ref, wsh_ref, wsl_ref, outs)


def _pre_next_kernel(tbl_ref, x_ref, mod_ref, n1_ref, wm_ref, wsh_ref, wsl_ref, *outs):
    _pre_body(x_ref[...], mod_ref, n1_ref, wm_ref, wsh_ref, wsl_ref, outs)


_PRE_OUT_WIDTHS = ((512, BF16), (1536, BF16), (512, BF16), (1536, BF16), (512, BF16), (3072, BF16),
                   (LANES, F32))


def _pre(layer, xs_in, mod, n1, wm, wsh, wsl, cond_tbl):
    first = layer == 0
    tile = lambda w: pl.BlockSpec((TT, w), lambda i, t: (i, 0))
    const = lambda shape: pl.BlockSpec(shape, lambda i, t: (0,) * len(shape))
    of_layer = lambda width: pl.BlockSpec((1, D, width), lambda i, t: (layer, 0, 0))
    w_specs = [pl.BlockSpec((1, N_MOD, D), lambda i, t: (t[i], 0, 0)), const((1, D)),
               of_layer(W_MAIN), of_layer(LANES), of_layer(LANES)]
    out_specs = [tile(w) for w, _ in _PRE_OUT_WIDTHS]
    out_shape = [jax.ShapeDtypeStruct((N_TOK, w), dt) for w, dt in _PRE_OUT_WIDTHS]
    if first:
        x_prompt2, x_sample2, pos = xs_in
        in_specs = [pl.BlockSpec((TT, D), lambda i, t: (jnp.minimum(i, CTX_TILES - 1), 0)),
                    pl.BlockSpec((TT, D), lambda i, t: (jnp.maximum(i - CTX_TILES, 0), 0)),
                    pl.BlockSpec((TT, D), lambda i, t: (jnp.maximum(i - CTX_TILES, 0) % LAT_TILES_PER_SEQ, 0))]
        kern = _pre_first_kernel
        out_specs = [tile(D)] + out_specs
        out_shape = [jax.ShapeDtypeStruct((N_TOK, D), F32)] + out_shape
        args = (x_prompt2, x_sample2, pos)
    else:
        in_specs = [tile(D)]
        kern = _pre_next_kernel
        args = (xs_in,)
    return pl.pallas_call(
        kern,
        grid_spec=pltpu.PrefetchScalarGridSpec(
            num_scalar_prefetch=1, grid=(N_TILES,), in_specs=in_specs + w_specs, out_specs=out_specs),
        out_shape=out_shape,
        compiler_params=_cparams("parallel"),
    )(cond_tbl, *args, mod, n1, wm, wsh, wsl)


def _fourier_kernel(x_ref, dft_ref, c_ref, s_ref, o_ref, z_scr, *, t_len, scale):
    @pl.when(pl.program_id(1) == 0)
    def _():
        for g in range(H):
            xg = x_ref[:, g * DH:(g + 1) * DH]
            z_scr[0:t_len, g * DH:(g + 1) * DH] = _dot(xg, c_ref[...]).astype(BF16)
            z_scr[t_len:2 * t_len, g * DH:(g + 1) * DH] = _dot(xg, s_ref[...]).astype(BF16)

    o_ref[...] = (_dot(dft_ref[...], z_scr[...]) * scale).astype(BF16)


def _fourier(f_all, dft, c128, s128, n_seq, t_len, row0):
    tr = min(t_len, 512)
    blk0 = row0 // t_len
    return pl.pallas_call(
        functools.partial(_fourier_kernel, t_len=t_len, scale=1.0 / math.sqrt(t_len * DH)),
        grid=(n_seq, t_len // tr),
        in_specs=[pl.BlockSpec((t_len, HW), lambda b, r: (blk0 + b, 0)),
                  pl.BlockSpec((tr, 2 * t_len), lambda b, r: (r, 0)),
                  pl.BlockSpec((DH, DH), lambda b, r: (0, 0)),
                  pl.BlockSpec((DH, DH), lambda b, r: (0, 0))],
        out_specs=pl.BlockSpec((tr, HW), lambda b, r: (b * (t_len // tr) + r, 0)),
        out_shape=jax.ShapeDtypeStruct((n_seq * t_len, HW), BF16),
        scratch_shapes=[pltpu.VMEM((2 * t_len, HW), BF16)],
        compiler_params=_cparams("parallel", "arbitrary"),
    )(f_all, dft, c128, s128)


HALO = 16


def _prep2_kernel(cur_ref, prev_ref, next_ref, sm_ref, cw_ref, prm_ref, qkv_ref, g_ref):
    i = pl.program_id(0)
    j = jnp.maximum(i - CTX_TILES, 0) % LAT_TILES_PER_SEQ
    is_lat = i >= CTX_TILES
    has_prev = jnp.logical_and(is_lat, j > 0)
    has_next = jnp.logical_and(is_lat, j < LAT_TILES_PER_SEQ - 1)
    x = cur_ref[...].astype(F32)
    row = lax.broadcasted_iota(jnp.int32, x.shape, 0)
    prev_row = jnp.where(has_prev, prev_ref[HALO - 1:HALO, :].astype(F32), 0.0)
    next_row = jnp.where(has_next, next_ref[0:1, :].astype(F32), 0.0)
    xp = jnp.where(row == 0, prev_row, pltpu.roll(x, 1, 0))
    xn = jnp.where(row == TT - 1, next_row, pltpu.roll(x, TT - 1, 0))
    y = _silu(cw_ref[0:1, :] * xp + cw_ref[1:2, :] * x + cw_ref[2:3, :] * xn)
    for h in range(2 * H):
        seg = y[:, h * DH:(h + 1) * DH]
        seg = seg * lax.rsqrt(jnp.sum(seg * seg, axis=-1, keepdims=True) + EPS)
        if h < H:
            seg = seg * (DH ** -0.5)
        qkv_ref[:, h * DH:(h + 1) * DH] = seg.astype(BF16)
    qkv_ref[:, 2 * HW:3 * HW] = y[:, 2 * HW:3 * HW].astype(BF16)
    z = sm_ref[...] + prm_ref[1:2, :]
    lane = lax.broadcasted_iota(jnp.int32, z.shape, 1)
    g_log = -jnp.exp(prm_ref[0:1, :]) * _softplus(z)
    out = jnp.where(lane < 8, g_log,
                    jnp.where(lane < 16, _sigmoid(z), jnp.where(lane < 24, z, -_softplus(-z))))
    g_ref[...] = out


def _prep2(gqkv, sm, conv_w, prm):
    nb = N_TOK // HALO
    return pl.pallas_call(
        _prep2_kernel,
        grid=(N_TILES,),
        in_specs=[pl.BlockSpec((TT, 3 * HW), lambda i: (i, 0)),
                  pl.BlockSpec((HALO, 3 * HW), lambda i: (jnp.maximum(i * (TT // HALO) - 1, 0), 0)),
                  pl.BlockSpec((HALO, 3 * HW), lambda i: (jnp.minimum((i + 1) * (TT // HALO), nb - 1), 0)),
                  pl.BlockSpec((TT, LANES), lambda i: (i, 0)),
                  pl.BlockSpec((3, 3 * HW), lambda i: (0, 0)),
                  pl.BlockSpec((8, LANES), lambda i: (0, 0))],
        out_specs=[pl.BlockSpec((TT, 3 * HW), lambda i: (i, 0)),
                   pl.BlockSpec((TT, LANES), lambda i: (i, 0))],
        out_shape=[jax.ShapeDtypeStruct((N_TOK, 3 * HW), BF16),
                   jax.ShapeDtypeStruct((N_TOK, LANES), F32)],
        compiler_params=_cparams("parallel"),
    )(gqkv, gqkv, gqkv, sm, conv_w, prm)


def _scan_tables():
    blk_f = np.zeros(N_CHUNKS, np.int32)
    blk_b = np.zeros(N_CHUNKS, np.int32)
    seq = np.zeros(N_CHUNKS, np.int32)
    first = np.zeros(N_CHUNKS, np.int32)
    last = np.zeros(N_CHUNKS, np.int32)
    step = 0
    base = 0
    sid = 0
    for n_seq, t_len in ((B_CTX, T_CTX), (B_LAT, T_LAT)):
        nc = t_len // CH
        for _ in range(n_seq):
            for c in range(nc):
                blk_f[step] = base + c
                blk_b[step] = base + nc - 1 - c
                seq[step] = sid
                first[step] = int(c == 0)
                last[step] = int(c == nc - 1)
                step += 1
            base += nc
            sid += 1
    return blk_f, blk_b, seq, first, last


def _ones_where(mask):
    return jnp.where(mask, 1.0, 0.0).astype(BF16)


def _scan_order_mask(d, n=CH):
    row = lax.broadcasted_iota(jnp.int32, (n, n), 0)
    col = lax.broadcasted_iota(jnp.int32, (n, n), 1)
    return (col <= row) if d == 0 else (col >= row)


def _row_of(col_vec):
    return jnp.broadcast_to(col_vec, (col_vec.shape[0], LANES)).T[0:1, :]


HC = H * CH


def _unit_tri_inverses(lbs, row, col):
    same = lambda sh: jnp.right_shift(row, sh) == jnp.right_shift(col, sh)
    zero = jnp.zeros_like(lbs[0])
    eye = jnp.where(row == col, 1.0, 0.0).astype(BF16)
    ms = [jnp.where(same(3), -lb, zero) for lb in lbs]
    m2s = [_dot(m, m).astype(BF16) for m in ms]
    m4s = [_dot(m2, m2).astype(BF16) for m2 in m2s]
    xs = [eye + m for m in ms]
    xs = [(x.astype(F32) + _dot(x, m2)).astype(BF16) for x, m2 in zip(xs, m2s)]
    xs = [(x.astype(F32) + _dot(x, m4)).astype(BF16) for x, m4 in zip(xs, m4s)]
    for sh in (3, 4, 5):
        pair = jnp.logical_and(same(sh + 1), jnp.logical_not(same(sh)))
        exs = [_dot(jnp.where(pair, lb, zero), x).astype(BF16) for lb, x in zip(lbs, xs)]
        fills = [_dot(x, ex) for x, ex in zip(xs, exs)]
        xs = [jnp.where(pair, (-fill).astype(BF16), x) for fill, x in zip(fills, xs)]
    return xs


def _head_stack(ref, r0, base):
    return jnp.concatenate([ref[r0:r0 + CH, base + h * DH:base + (h + 1) * DH] for h in range(H)], axis=0)


def _gdn_wy_kernel(qkv_ref, g_ref, u_ref, w_ref, qg_ref, kg_ref, attn_ref, gl_ref):
    row = lax.broadcasted_iota(jnp.int32, (HC, HC), 0)
    col = lax.broadcasted_iota(jnp.int32, (HC, HC), 1)
    same_head = jnp.right_shift(row, 6) == jnp.right_shift(col, 6)
    lane_head = jnp.right_shift(lax.broadcasted_iota(jnp.int32, (CH, HC), 1), 6)
    subs = range(TT // CH)
    chains = [(s, d) for s in subs for d in range(2)]
    g = [g_ref[s * CH:(s + 1) * CH, :] for s in subs]
    q_st = [_head_stack(qkv_ref, s * CH, 0) for s in subs]
    k_st = [_head_stack(qkv_ref, s * CH, HW) for s in subs]
    v_st = [_head_stack(qkv_ref, s * CH, 2 * HW).astype(F32) for s in subs]
    kf = [k.astype(F32) for k in k_st]
    kk = [_dot_nt(k, k) for k in k_st]
    qk = [_dot_nt(q, k) for q, k in zip(q_st, k_st)]
    order = [_ones_where(_scan_order_mask(d)) for d in range(2)]
    g3 = [_split3(a) for a in g]
    cols = [_dot(order[d], g3[s][0]) + (_dot(order[d], g3[s][1]) + _dot(order[d], g3[s][2]))
            for s, d in chains]
    incl = [jnp.logical_and(same_head, (col <= row) if d == 0 else (col >= row)) for d in range(2)]
    strict = [jnp.logical_and(same_head, (col < row) if d == 0 else (col > row)) for d in range(2)]
    gc, beta, g_last, decay, l_hi, l_lo, egc, rhs = [], [], [], [], [], [], [], []
    for c, (s, d) in enumerate(chains):
        last = CH - 1 if d == 0 else 0
        lanes = [d * H + h for h in range(H)]
        gc.append(jnp.concatenate([cols[c][:, l:l + 1] for l in lanes], axis=0))
        beta.append(jnp.concatenate([g[s][:, 8 + l:9 + l] for l in lanes], axis=0))
        g_last.append(jnp.concatenate(
            [jnp.broadcast_to(cols[c][last:last + 1, l:l + 1], (CH, 1)) for l in lanes], axis=0))
        gr = _row_of(gc[c])
        decay.append(jnp.where(incl[d], jnp.exp(jnp.where(incl[d], gc[c] - gr, 0.0)), 0.0))
        lmat = jnp.where(strict[d], beta[c] * kk[s] * decay[c], 0.0)
        l_hi.append(lmat.astype(BF16))
        l_lo.append((lmat - l_hi[c].astype(F32)).astype(BF16))
        egc.append(jnp.exp(gc[c]))
        rhs.append(jnp.concatenate([v_st[s] * beta[c], kf[s] * (beta[c] * egc[c])], axis=1))
        gl_rows = [jnp.broadcast_to(cols[c][last:last + 1, l:l + 1], (1, LANES)) for l in lanes]
        gl_ref[d, s * 8:(s + 1) * 8, :] = jnp.concatenate(gl_rows + [jnp.zeros((8 - H, LANES), F32)], axis=0)
    t_inv = _unit_tri_inverses(l_hi, row, col)
    y = [_dot(t, r.astype(BF16)) for t, r in zip(t_inv, rhs)]
    ys = [_split2(a) for a in y]
    ly = [_dot(l_hi[c], ys[c][0]) + (_dot(l_hi[c], ys[c][1]) + _dot(l_lo[c], ys[c][0])) for c in range(len(chains))]
    resid = [(rhs[c] - (y[c] + ly[c])).astype(BF16) for c in range(len(chains))]
    y = [y[c] + _dot(t_inv[c], resid[c]) for c in range(len(chains))]
    for c, (s, d) in enumerate(chains):
        r0 = s * CH
        attn = qk[s] * decay[c]
        a64 = jnp.zeros((CH, HC), F32)
        for h in range(H):
            a64 = jnp.where(lane_head == h, attn[h * CH:(h + 1) * CH, :], a64)
        attn_ref[d, r0:r0 + CH, :] = a64.astype(BF16)
        qg = (q_st[s].astype(F32) * egc[c]).astype(BF16)
        kg = (kf[s] * jnp.exp(g_last[c] - gc[c])).astype(BF16)
        for h in range(H):
            rs = slice(h * CH, (h + 1) * CH)
            ls = slice(h * DH, (h + 1) * DH)
            u_ref[d, r0:r0 + CH, ls] = y[c][rs, 0:DH]
            w_ref[d, r0:r0 + CH, ls] = y[c][rs, DH:2 * DH].astype(BF16)
            qg_ref[d, r0:r0 + CH, ls] = qg[rs, :]
            kg_ref[d, r0:r0 + CH, ls] = kg[rs, :]


def _gdn_wy(qkv, gates):
    spec = lambda w: pl.BlockSpec((2, TT, w), lambda i: (0, i, 0))
    shp = lambda w, dt: jax.ShapeDtypeStruct((2, N_TOK, w), dt)
    return pl.pallas_call(
        _gdn_wy_kernel,
        grid=(N_TILES,),
        in_specs=[pl.BlockSpec((TT, 3 * HW), lambda i: (i, 0)), pl.BlockSpec((TT, LANES), lambda i: (i, 0))],
        out_specs=[spec(HW), spec(HW), spec(HW), spec(HW), spec(HC),
                   pl.BlockSpec((2, 8 * (TT // CH), LANES), lambda i: (0, i, 0))],
        out_shape=[shp(HW, F32), shp(HW, BF16), shp(HW, BF16), shp(HW, BF16), shp(HC, BF16),
                   jax.ShapeDtypeStruct((2, 8 * N_CHUNKS, LANES), F32)],
        compiler_params=_cparams("parallel"),
    )(qkv, gates)


def _gdn_rec_body(ins, of_ref, ob_ref, s_scr):
    chains = [(d, h) for d in range(2) for h in range(H)]
    ls = [slice(h * DH, (h + 1) * DH) for h in range(H)]
    s = [s_scr[d, h] for d, h in chains]
    sb = [a.astype(BF16) for a in s]
    ws = [_dot(ins[6 * d + 1][0, :, ls[h]], sb[c]) for c, (d, h) in enumerate(chains)]
    inter = [_dot(ins[6 * d + 2][0, :, ls[h]], sb[c]) for c, (d, h) in enumerate(chains)]
    vb = [(ins[6 * d][0, :, ls[h]] - ws[c]).astype(BF16) for c, (d, h) in enumerate(chains)]
    intra = [_dot(ins[6 * d + 4][0, :, h * CH:(h + 1) * CH], vb[c]) for c, (d, h) in enumerate(chains)]
    upd = [_dot_tn(ins[6 * d + 3][0, :, ls[h]], vb[c]) for c, (d, h) in enumerate(chains)]
    for c, (d, h) in enumerate(chains):
        o_ref = of_ref if d == 0 else ob_ref
        o_ref[:, ls[h]] = inter[c] + intra[c]
        s_scr[d, h] = s[c] * jnp.exp(ins[6 * d + 5][0, h:h + 1, :]) + upd[c]


def _mlstm_body(qkv_refs, g_refs, o_refs, c_scr, n_scr, m_scr):
    scale = DH ** -0.5
    chains = [(d, h) for d in range(2) for h in range(H)]
    nch = range(len(chains))
    incl = [_scan_order_mask(d) for d in range(2)]
    g = [g_refs[d][...] for d in range(2)]
    cols = [_dot_exact_lhs(_ones_where(incl[d]), g[d]) for d in range(2)]
    g_t = [jnp.concatenate([g[d], cols[d]], axis=0).T for d in range(2)]
    q = [qkv_refs[d][:, h * DH:(h + 1) * DH] for d, h in chains]
    k = [qkv_refs[d][:, HW + h * DH:HW + (h + 1) * DH] for d, h in chains]
    v = [qkv_refs[d][:, 2 * HW + h * DH:2 * HW + (h + 1) * DH] for d, h in chains]
    qk = [_dot_nt(q[c], k[c]) * scale for c in nch]
    cst = [c_scr[d, h] for d, h in chains]
    qc = [_dot(q[c], cst[c].astype(BF16)) for c in nch]
    lf = [24 + d * H + h for d, h in chains]
    li = [16 + d * H + h for d, h in chains]
    dd = [d for d, _ in chains]
    last = [CH - 1 if d == 0 else 0 for d in dd]
    bc = [cols[dd[c]][:, lf[c]:lf[c] + 1] for c in nch]
    br = [g_t[dd[c]][lf[c]:lf[c] + 1, CH:2 * CH] for c in nch]
    ig_c = [g[dd[c]][:, li[c]:li[c] + 1] for c in nch]
    ig_r = [g_t[dd[c]][li[c]:li[c] + 1, 0:CH] for c in nch]
    dlog = [jnp.where(incl[dd[c]], bc[c] - br[c] + ig_r[c], -jnp.inf) for c in nch]
    dmax = [jnp.max(a, axis=1, keepdims=True) for a in dlog]
    b_last = [bc[c][last[c]:last[c] + 1, :] for c in nch]
    w_max = [jnp.max(b_last[c] - br[c] + ig_r[c], axis=1, keepdims=True) for c in nch]
    m = [m_scr[d, h:h + 1, 0:1] for d, h in chains]
    n = [n_scr[d, h:h + 1, :] for d, h in chains]
    m_inter = [bc[c] + m[c] for c in nch]
    m_row = [jnp.maximum(dmax[c], m_inter[c]) for c in nch]
    s = [qk[c] * jnp.exp(dlog[c] - m_row[c]) for c in nch]
    sv = [_dot(s[c].astype(BF16), v[c]) for c in nch]
    e_inter = [jnp.exp(m_inter[c] - m_row[c]) for c in nch]
    qn = [jnp.sum(q[c].astype(F32) * n[c], axis=1, keepdims=True) for c in nch]
    s_sum = [jnp.sum(a, axis=1, keepdims=True) for a in s]
    den = [e_inter[c] * qn[c] + s_sum[c] for c in nch]
    inv = [1.0 / jnp.maximum(jnp.abs(den[c]), jnp.exp(-m_row[c])) for c in nch]
    m_new = [jnp.maximum(b_last[c] + m[c], w_max[c]) for c in nch]
    f_c = [jnp.exp(b_last[c] + m[c] - m_new[c]) for c in nch]
    kw = [k[c].astype(F32) * (jnp.exp(b_last[c] - bc[c] + ig_c[c] - m_new[c]) * scale) for c in nch]
    kv = [_dot_tn(kw[c].astype(BF16), v[c]) for c in nch]
    for c, (d, h) in enumerate(chains):
        o_refs[d][:, h * DH:(h + 1) * DH] = (e_inter[c] * qc[c] + sv[c]) * inv[c]
        n_scr[d, h:h + 1, :] = f_c[c] * n[c] + jnp.sum(kw[c], axis=0, keepdims=True)
        m_scr[d, h:h + 1, :] = jnp.broadcast_to(m_new[c], (1, LANES))
        c_scr[d, h] = f_c[c] * cst[c] + kv[c]


def _rec_kernel(bf_ref, bb_ref, seq_ref, first_ref, last_ref, *refs):
    gdn_ins, (qkvf_ref, gf_ref, qkvb_ref, gb_ref, s0_ref, c0_ref, n0_ref, m0_ref) = refs[:12], refs[12:20]
    of_ref, ob_ref, hf_ref, hb_ref, sfin_ref, cfin_ref, nfin_ref, mfin_ref = refs[20:28]
    s_scr, c_scr, n_scr, m_scr = refs[28:]
    t = pl.program_id(0)

    @pl.when(first_ref[t] == 1)
    def _():
        s_scr[...] = s0_ref[0]
        c_scr[...] = c0_ref[0]
        n_scr[...] = n0_ref[0]
        m_scr[...] = m0_ref[0]

    _gdn_rec_body(gdn_ins, of_ref, ob_ref, s_scr)
    _mlstm_body((qkvf_ref, qkvb_ref), (gf_ref, gb_ref), (hf_ref, hb_ref), c_scr, n_scr, m_scr)

    @pl.when(last_ref[t] == 1)
    def _():
        sfin_ref[0] = s_scr[...]
        cfin_ref[0] = c_scr[...]
        nfin_ref[0] = n_scr[...]
        mfin_ref[0] = m_scr[...]


def _rec_scan(tables, wy, qkv, gates, s0, c0, n0, m0):
    fwd = lambda t, bf, bb, *_: (bf[t], 0)
    bwd = lambda t, bf, bb, *_: (bb[t], 0)

    def dir_specs(d):
        idx = (lambda t, bf, bb, *_: (0, bf[t], 0)) if d == 0 else (lambda t, bf, bb, *_: (1, bb[t], 0))
        return [pl.BlockSpec((1, CH, HW), idx)] * 4 + [pl.BlockSpec((1, CH, HC), idx),
                                                        pl.BlockSpec((1, 8, LANES), idx)]
    vec = pl.BlockSpec((1, 2, H, LANES), lambda t, bf, bb, sq, *_: (sq[t], 0, 0, 0))
    mat = pl.BlockSpec((1, 2, H, DH, DH), lambda t, bf, bb, sq, *_: (sq[t], 0, 0, 0, 0))
    tok = jax.ShapeDtypeStruct((N_TOK, HW), F32)
    mats = jax.ShapeDtypeStruct((N_SEQ, 2, H, DH, DH), F32)
    vecs = jax.ShapeDtypeStruct((N_SEQ, 2, H, LANES), F32)
    return pl.pallas_call(
        _rec_kernel,
        grid_spec=pltpu.PrefetchScalarGridSpec(
            num_scalar_prefetch=5, grid=(N_CHUNKS,),
            in_specs=dir_specs(0) + dir_specs(1)
            + [pl.BlockSpec((CH, 3 * HW), fwd), pl.BlockSpec((CH, LANES), fwd),
               pl.BlockSpec((CH, 3 * HW), bwd), pl.BlockSpec((CH, LANES), bwd), mat, mat, vec, vec],
            out_specs=[pl.BlockSpec((CH, HW), fwd), pl.BlockSpec((CH, HW), bwd),
                       pl.BlockSpec((CH, HW), fwd), pl.BlockSpec((CH, HW), bwd), mat, mat, vec, vec],
            scratch_shapes=[pltpu.VMEM((2, H, DH, DH), F32), pltpu.VMEM((2, H, DH, DH), F32),
                            pltpu.VMEM((2, H, LANES), F32), pltpu.VMEM((2, H, LANES), F32)]),
        out_shape=[tok, tok, tok, tok, mats, mats, vecs, vecs],
        compiler_params=_cparams("arbitrary"),
    )(*tables, *wy, *wy, qkv, gates, qkv, gates, s0, c0, n0, m0)


def _post_kernel(tbl_ref, x_ref, mod_ref, yfc_ref, yfl_ref, of_ref, ob_ref, hf_ref, hb_ref,
                 gz_ref, mo_ref, gates_ref, gn_ref, mn_ref, wbf_ref, wbg_ref, wbm_ref, wo_ref,
                 n2_ref, rwh_ref, rwl_ref, rb_ref,
                 xo_ref, h2_ref, info_ref, infot_ref, stats_ref, cnt_scr):
    i = pl.program_id(0)

    @pl.when(i == 0)
    def _():
        cnt_scr[...] = jnp.zeros_like(cnt_scr)

    yf = jnp.where(i < CTX_TILES, yfc_ref[...], yfl_ref[...])
    og = of_ref[...] + ob_ref[...]
    hm = hf_ref[...] + hb_ref[...]
    og_parts, hm_parts = [], []
    for h in range(H):
        sl = slice(h * DH, (h + 1) * DH)
        og_parts.append(_rms(og[:, sl]) * gn_ref[...] * _silu(gz_ref[:, sl].astype(F32)))
        hm_parts.append(_rms(hm[:, sl]) * mn_ref[:, sl] * _sigmoid(mo_ref[:, sl].astype(F32)))
    ogb = jnp.concatenate(og_parts, axis=1).astype(BF16)
    hmb = jnp.concatenate(hm_parts, axis=1).astype(BF16)
    y = (_sigmoid(gates_ref[:, 0:D].astype(F32)) * _dot(yf, wbf_ref[...])
         + _sigmoid(gates_ref[:, D:2 * D].astype(F32)) * _dot(ogb, wbg_ref[...])
         + _sigmoid(gates_ref[:, 2 * D:3 * D].astype(F32)) * _dot(hmb, wbm_ref[...]))
    x = x_ref[...] + mod_ref[0, 2:3, :] * _dot(y.astype(BF16), wo_ref[...])
    xo_ref[...] = x
    h2 = _rms(x) * n2_ref[...]
    h2 = h2 * (1.0 + mod_ref[0, 4:5, :]) + mod_ref[0, 3:4, :]
    hb2, hl2 = _split2(h2)
    h2_ref[...] = hb2
    logits = (_dot(hb2, rwh_ref[...]) + (_dot(hb2, rwl_ref[...]) + _dot(hl2, rwh_ref[...]))
              + rb_ref[...])
    lane = lax.broadcasted_iota(jnp.int32, logits.shape, 1).astype(F32)
    work = jnp.where(lane < N_EXP, logits, -jnp.inf)
    sel = jnp.zeros(logits.shape, F32)
    hits, exps = [], []
    top0 = None
    denom = None
    for kk in range(TOP_K):
        mx = jnp.max(work, axis=1, keepdims=True)
        first = jnp.min(jnp.where(work == mx, lane, float(LANES)), axis=1, keepdims=True)
        hit = lane == first
        if kk == 0:
            top0 = mx
        e = jnp.exp(mx - top0)
        denom = e if kk == 0 else denom + e
        hits.append(hit)
        exps.append(e)
        sel = jnp.where(hit, 1.0, sel)
        work = jnp.where(hit, -jnp.inf, work)
    r_i = lax.broadcasted_iota(jnp.int32, (TT, TT), 0)
    c_i = lax.broadcasted_iota(jnp.int32, (TT, TT), 1)
    within = _dot(_ones_where(c_i < r_i), sel.astype(BF16))
    tile_cnt = jnp.sum(sel, axis=0, keepdims=True)
    a_i = lax.broadcasted_iota(jnp.int32, (LANES, LANES), 0)
    b_i = lax.broadcasted_iota(jnp.int32, (LANES, LANES), 1)
    tile_off = _dot(jnp.broadcast_to(tile_cnt, (8, LANES)).astype(BF16), _ones_where(a_i < b_i))[0:1, :]
    local = tile_off + within
    info = jnp.zeros(logits.shape, F32)
    for kk in range(TOP_K):
        pos = jnp.sum(jnp.where(hits[kk], local, 0.0), axis=1, keepdims=True)
        info = jnp.where(lane == float(kk), pos, info)
        info = jnp.where(lane == float(TOP_K + kk), exps[kk] / denom, info)
    info_ref[...] = info
    infot_ref[...] = info.T[0:8, :]
    stats_ref[0] = jnp.concatenate([cnt_scr[0:1, :], tile_cnt, tile_off, jnp.zeros((5, LANES), F32)], axis=0)
    cnt_scr[0:1, :] = cnt_scr[0:1, :] + tile_cnt


def _post(cond_tbl, x, mod, yfc, yfl, o_f, o_b, h_f, h_b, gz, mo, gates, gn, mn, wbf, wbg, wbm, wo,
          n2, rwh, rwl, rb):
    tile = lambda w: pl.BlockSpec((TT, w), lambda i, t: (i, 0))
    const = lambda shape: pl.BlockSpec(shape, lambda i, t: (0,) * len(shape))
    in_specs = [tile(D), pl.BlockSpec((1, N_MOD, D), lambda i, t: (t[i], 0, 0)),
                pl.BlockSpec((TT, HW), lambda i, t: (jnp.minimum(i, CTX_TILES - 1), 0)),
                pl.BlockSpec((TT, HW), lambda i, t: (jnp.maximum(i - CTX_TILES, 0), 0)),
                tile(HW), tile(HW), tile(HW), tile(HW),
                tile(HW), tile(HW), tile(3 * D),
                const((1, DH)), const((1, HW)), const((HW, D)), const((HW, D)), const((HW, D)),
                const((D, D)), const((1, D)), const((D, LANES)), const((D, LANES)), const((1, LANES))]
    return pl.pallas_call(
        _post_kernel,
        grid_spec=pltpu.PrefetchScalarGridSpec(
            num_scalar_prefetch=1, grid=(N_TILES,), in_specs=in_specs,
            out_specs=[tile(D), tile(D), tile(LANES), pl.BlockSpec((8, TT), lambda i, t: (0, i)),
                       pl.BlockSpec((1, 8, LANES), lambda i, t: (i, 0, 0))],
            scratch_shapes=[pltpu.VMEM((8, LANES), F32)]),
        out_shape=[jax.ShapeDtypeStruct((N_TOK, D), F32), jax.ShapeDtypeStruct((N_TOK, D), BF16),
                   jax.ShapeDtypeStruct((N_TOK, LANES), F32), jax.ShapeDtypeStruct((8, N_TOK), F32),
                   jax.ShapeDtypeStruct((N_TILES, 8, LANES), F32)],
        compiler_params=_cparams("arbitrary"),
    )(cond_tbl, x, mod, yfc, yfl, o_f, o_b, h_f, h_b, gz, mo, gates, gn, mn, wbf, wbg, wbm, wo,
      n2, rwh, rwl, rb)


PAIRS = TT * TOP_K
RUN_BITS = tuple(1 << b for b in range(8, -1, -1))


def _slot_rows(start_slot, n_slots):
    return pl.ds(pl.multiple_of(start_slot * 8, 8), n_slots * 8)


def _for_each_run(base, cnt_ref, fn):
    def body(e2, carry):
        for par in range(2):
            e = 2 * e2 + par
            cnt = cnt_ref[base + e]
            for k, bit in enumerate(RUN_BITS):
                @pl.when((cnt & bit) != 0)
                def _():
                    fn(base + e, cnt & (-2 * bit), bit, (k + par) % 2)
        return carry
    lax.fori_loop(0, N_EXP // 2, body, 0)


def _zero_fill(ps_ref, pc_ref, tail_ref, xs_hbm, zbuf, zsem, start):
    def piece(dst_slot, n):
        cp = pltpu.make_async_copy(zbuf.at[pl.ds(0, n * 8)], xs_hbm.at[_slot_rows(dst_slot, n)], zsem)
        if start:
            cp.start()
        else:
            cp.wait()

    def pad(e, carry):
        cnt = pc_ref[e]
        for bit in RUN_BITS[1:]:
            @pl.when((cnt & bit) != 0)
            def _():
                piece(ps_ref[e] + (cnt & (-2 * bit)), bit)
        return carry

    def tail(t, carry):
        piece(tail_ref[0] + t * TM, TM)
        return carry

    lax.fori_loop(0, N_EXP, pad, 0)
    lax.fori_loop(0, tail_ref[1], tail, 0)


def _dispatch_kernel(ss_ref, cnt_ref, toff_ref, ps_ref, pc_ref, tail_ref, h2_ref, infot_ref, xs_hbm,
                     stage, zbuf, sem, zsem):
    i = pl.program_id(0)
    buf = i % 2
    whole = lambda b: pltpu.make_async_copy(stage.at[b], xs_hbm.at[pl.ds(0, PAIRS * 8)], sem.at[b])

    @pl.when(i >= 2)
    def _():
        whole(buf).wait()

    r = lax.broadcasted_iota(jnp.int32, (PAIRS, TT), 0).astype(F32)
    pick = r == infot_ref[0:1, :]
    for kk in range(1, TOP_K):
        pick = jnp.logical_or(pick, r == infot_ref[kk:kk + 1, :])
    rows = _dot(_ones_where(pick), h2_ref[...])
    for cc in range(8):
        stage[buf, pl.ds(cc, PAIRS, stride=8), :] = rows[:, cc * LANES:(cc + 1) * LANES]

    def send(j, o, n, prio):
        pltpu.make_async_copy(stage.at[buf, _slot_rows(toff_ref[j] + o, n)],
                              xs_hbm.at[_slot_rows(ss_ref[j] + o, n)], sem.at[buf]).start(priority=prio)

    _for_each_run(i * N_EXP, cnt_ref, send)

    @pl.when(i == 0)
    def _():
        zbuf[...] = jnp.zeros_like(zbuf)
        _zero_fill(ps_ref, pc_ref, tail_ref, xs_hbm, zbuf, zsem, True)

    @pl.when(i == N_TILES - 1)
    def _():
        whole(1 - buf).wait()
        whole(buf).wait()
        _zero_fill(ps_ref, pc_ref, tail_ref, xs_hbm, zbuf, zsem, False)


def _dispatch(slot_start, seg_cnt, seg_off, pad_start, pad_cnt, tail, h2, info_t):
    return pl.pallas_call(
        _dispatch_kernel,
        grid_spec=pltpu.PrefetchScalarGridSpec(
            num_scalar_prefetch=6, grid=(N_TILES,),
            in_specs=[pl.BlockSpec((TT, D), lambda i, *_: (i, 0)), pl.BlockSpec((8, TT), lambda i, *_: (0, i))],
            out_specs=pl.BlockSpec(memory_space=pl.ANY),
            scratch_shapes=[pltpu.VMEM((2, PAIRS * 8, LANES), F32), pltpu.VMEM((TM * 8, LANES), F32),
                            pltpu.SemaphoreType.DMA((2,)), pltpu.SemaphoreType.DMA(())]),
        out_shape=jax.ShapeDtypeStruct((S_MAX * 8, LANES), F32),
        compiler_params=_cparams("arbitrary"),
    )(slot_start, seg_cnt, seg_off, pad_start, pad_cnt, tail, h2, info_t)


N_MAT = 3


def _gmm_kernel(te_ref, nu_ref, valid_ref, nxt_ref, x_ref, wg_hbm, bg_ref, wu_hbm, bu_ref, wd_hbm, bd_ref, o_ref,
                wbuf, wg_scr, wu_scr, wd_scr, slot_ref, wsem, *, layer):
    i = pl.program_id(0)
    used = i < nu_ref[0]

    def fetch(e, slot, start):
        for m, w_hbm in enumerate((wg_hbm, wu_hbm, wd_hbm)):
            cp = pltpu.make_async_copy(w_hbm.at[layer, e], wbuf.at[slot * N_MAT + m], wsem.at[slot * N_MAT + m])
            if start:
                cp.start()
            else:
                cp.wait()

    @pl.when(i == 0)
    def _():
        slot_ref[0] = 0
        fetch(te_ref[0], 0, True)

    @pl.when(used)
    def _():
        e = te_ref[i]
        e_prev = te_ref[jnp.maximum(i - 1, 0)]

        @pl.when(jnp.logical_or(i == 0, e != e_prev))
        def _():
            slot = slot_ref[0]
            fetch(e, slot, False)
            wg_scr[...] = wbuf[slot * N_MAT].astype(BF16)
            wu_scr[...] = wbuf[slot * N_MAT + 1].astype(BF16)
            wd_scr[...] = wbuf[slot * N_MAT + 2].astype(BF16)

            @pl.when(nxt_ref[i] >= 0)
            def _():
                fetch(nxt_ref[i], 1 - slot, True)

            slot_ref[0] = 1 - slot

        x = jnp.concatenate([x_ref[pl.ds(cc, TM, stride=8), :] for cc in range(8)], axis=1)
        row = lax.broadcasted_iota(jnp.int32, (TM, 1), 0)
        x = jnp.where(row < valid_ref[i], x, 0.0).astype(BF16)
        acts = []
        for j in range(D // FF_CHUNK):
            cs = slice(j * FF_CHUNK, (j + 1) * FF_CHUNK)
            gate = jnp.minimum(_dot(x, wg_scr[:, cs]) + bg_ref[0, 0, :, cs], SWIGLU_LIMIT)
            up = jnp.clip(_dot(x, wu_scr[:, cs]) + bu_ref[0, 0, :, cs], -SWIGLU_LIMIT, SWIGLU_LIMIT)
            acts.append(((up + 1.0) * gate * _sigmoid(SWIGLU_ALPHA * gate)).astype(BF16))
        y = _dot(jnp.concatenate(acts, axis=1), wd_scr[...]) + bd_ref[0, 0]
        for cc in range(8):
            o_ref[pl.ds(cc, TM, stride=8), :] = y[:, cc * LANES:(cc + 1) * LANES]

    @pl.when(jnp.logical_not(used))
    def _():
        o_ref[...] = jnp.zeros_like(o_ref)


def _gmm(layer, tile_exp, n_used, valid, nxt_exp, xs, wg, bg, wu, bu, wd, bd):
    wspec = pl.BlockSpec(memory_space=pl.ANY)
    bspec = pl.BlockSpec((1, 1, 1, D), lambda i, te, *_: (layer, te[i], 0, 0))
    return pl.pallas_call(
        functools.partial(_gmm_kernel, layer=layer),
        grid_spec=pltpu.PrefetchScalarGridSpec(
            num_scalar_prefetch=4, grid=(NT_MAX,),
            in_specs=[pl.BlockSpec((TM * 8, LANES), lambda i, te, nu, *_: (jnp.minimum(i, nu[0] - 1), 0)),
                      wspec, bspec, wspec, bspec, wspec, bspec],
            out_specs=pl.BlockSpec((TM * 8, LANES), lambda i, *_: (i, 0)),
            scratch_shapes=[pltpu.VMEM((2 * N_MAT, D, D), F32)] + [pltpu.VMEM((D, D), BF16)] * N_MAT
            + [pltpu.SMEM((1,), jnp.int32), pltpu.SemaphoreType.DMA((2 * N_MAT,))]),
        out_shape=jax.ShapeDtypeStruct((S_MAX * 8, LANES), F32),
        compiler_params=_cparams("arbitrary"),
    )(tile_exp, n_used, valid, nxt_exp, xs, wg, bg.reshape(DEPTH, N_EXP, 1, D), wu,
      bu.reshape(DEPTH, N_EXP, 1, D), wd, bd.reshape(DEPTH, N_EXP, 1, D))


def _combine_kernel(tbl_ref, ss_ref, cnt_ref, toff_ref, y_hbm, info_ref, x_ref, mod_ref, g_ref, *refs,
                    final):
    outs, (ybuf, sem) = refs[:-2], refs[-2:]
    i = pl.program_id(0)
    buf = i % 2

    def fetch(tile, b):
        def recv(j, o, n, prio):
            pltpu.make_async_copy(y_hbm.at[_slot_rows(ss_ref[j] + o, n)],
                                  ybuf.at[b, _slot_rows(toff_ref[j] + o, n)], sem.at[b]).start(priority=prio)
        _for_each_run(tile * N_EXP, cnt_ref, recv)

    @pl.when(i == 0)
    def _():
        fetch(0, 0)

    @pl.when(i + 1 < N_TILES)
    def _():
        fetch(i + 1, 1 - buf)

    r = lax.broadcasted_iota(jnp.int32, (TT, PAIRS), 1).astype(F32)
    wm = jnp.zeros((TT, PAIRS), F32)
    for kk in range(TOP_K):
        wm = jnp.where(r == info_ref[:, kk:kk + 1], info_ref[:, TOP_K + kk:TOP_K + kk + 1], wm)
    pltpu.make_async_copy(y_hbm.at[pl.ds(0, PAIRS * 8)], ybuf.at[buf], sem.at[buf]).wait()
    y = jnp.concatenate([ybuf[buf, pl.ds(cc, PAIRS, stride=8), :] for cc in range(8)], axis=1)
    x = x_ref[...] + mod_ref[0, 5:6, :] * _dot(wm.astype(BF16), y.astype(BF16))
    if not final:
        outs[0][...] = x
    else:
        res = _rms(x) * g_ref[...]

        @pl.when(i < CTX_TILES)
        def _():
            outs[0][...] = res

        @pl.when(i >= CTX_TILES)
        def _():
            outs[1][...] = res


def _combine(final, cond_tbl, slot_start, seg_cnt, seg_off, y_sorted, info, x, mod, g):
    tile = lambda w: pl.BlockSpec((TT, w), lambda i, *_: (i, 0))
    if final:
        out_specs = [pl.BlockSpec((TT, D), lambda i, *_: (jnp.minimum(i, CTX_TILES - 1), 0)),
                     pl.BlockSpec((TT, D), lambda i, *_: (jnp.maximum(i - CTX_TILES, 0), 0))]
        out_shape = [jax.ShapeDtypeStruct((N_CTX, D), F32), jax.ShapeDtypeStruct((N_LAT, D), F32)]
    else:
        out_specs = [tile(D)]
        out_shape = [jax.ShapeDtypeStruct((N_TOK, D), F32)]
    return pl.pallas_call(
        functools.partial(_combine_kernel, final=final),
        grid_spec=pltpu.PrefetchScalarGridSpec(
            num_scalar_prefetch=4, grid=(N_TILES,),
            in_specs=[pl.BlockSpec(memory_space=pl.ANY), tile(LANES), tile(D),
                      pl.BlockSpec((1, N_MOD, D), lambda i, t, *_: (t[i], 0, 0)),
                      pl.BlockSpec((1, D), lambda i, *_: (0, 0))],
            out_specs=out_specs,
            scratch_shapes=[pltpu.VMEM((2, PAIRS * 8, LANES), F32), pltpu.SemaphoreType.DMA((2,))]),
        out_shape=out_shape,
        compiler_params=_cparams("arbitrary"),
    )(cond_tbl, slot_start, seg_cnt, seg_off, y_sorted, info, x, mod, g)


def _dft_consts(t_len):
    k = np.arange(t_len, dtype=np.int64)
    ang = 2.0 * np.pi * ((k[:, None] * k[None, :]) % t_len).astype(np.float64) / t_len
    return np.concatenate([np.cos(ang), -np.sin(ang)], axis=1).astype(np.float32)


def _pos_table():
    quarter = D // 4
    omega = 1.0 / (10000.0 ** (np.arange(quarter, dtype=np.float32) / np.float32(quarter)))
    omega = omega.astype(np.float32).astype(np.float64)
    t = np.arange(T_LAT)
    ang_r = (t // GRID_W).astype(np.float64)[:, None] * omega
    ang_c = (t % GRID_W).astype(np.float64)[:, None] * omega
    return np.concatenate([np.sin(ang_r), np.cos(ang_r), np.sin(ang_c), np.cos(ang_c)],
                          axis=-1).astype(np.float32)


def _repack_w_in(w):
    a, b, c, d = 2560, 2576, 4624, 4640
    main = jnp.concatenate([w[..., :a], w[..., b:c], w[..., d:]], axis=-1).astype(BF16)
    small = jnp.concatenate([w[..., a:b], w[..., c:d], jnp.zeros((DEPTH, D, LANES - 32), F32)], axis=-1)
    hi = small.astype(BF16)
    lo = (small - hi.astype(F32)).astype(BF16)
    return main, hi, lo


def _route_glue(stats):
    seg_base = stats[:, 0, :N_EXP].astype(jnp.int32)
    seg_cnt = stats[:, 1, :N_EXP].astype(jnp.int32)
    seg_off = stats[:, 2, :N_EXP].astype(jnp.int32)
    counts = seg_base[-1] + seg_cnt[-1]
    region = ((counts + TM - 1) // TM) * TM
    ends = jnp.cumsum(region)
    off = ends - region
    n_used = ends[-1] // TM
    tile_start = jnp.arange(NT_MAX, dtype=jnp.int32) * TM
    tile_exp = jnp.minimum(jnp.sum((tile_start[:, None] >= ends[None, :]).astype(jnp.int32), axis=1), N_EXP - 1)
    used = tile_start < ends[-1]
    tile_exp = jnp.where(used, tile_exp, jnp.max(jnp.where(used, tile_exp, 0)))
    run_end = jnp.sum(jnp.where(tile_exp[:, None] == jnp.arange(N_EXP)[None, :], (off + counts)[None, :], 0), axis=1)
    valid = jnp.clip(run_end - tile_start, 0, TM)
    experts = jnp.arange(N_EXP, dtype=jnp.int32)
    later = jnp.logical_and(experts[None, :] > experts[:, None], counts[None, :] > 0)
    next_used = jnp.min(jnp.where(later, experts[None, :], N_EXP), axis=1)
    next_used = jnp.where(next_used < N_EXP, next_used, -1)
    nxt_exp = jnp.sum(jnp.where(tile_exp[:, None] == experts[None, :], next_used[None, :], 0), axis=1)
    slot_start = off[None, :] + seg_base
    tail = jnp.stack([ends[-1], (S_MAX - ends[-1]) // TM]).astype(jnp.int32)
    return (tile_exp.astype(jnp.int32), n_used.reshape(1).astype(jnp.int32), valid.astype(jnp.int32),
            nxt_exp.astype(jnp.int32), slot_start.reshape(-1), seg_cnt.reshape(-1), seg_off.reshape(-1),
            (off + counts).astype(jnp.int32), (region - counts).astype(jnp.int32), tail)


def kernel(x_prompt, x_sample, state_gdn, state_mlstm_c, state_mlstm_n, state_mlstm_m, c, c_ctx,
           w_ada, b_ada, norm1_g, norm2_g, w_in, gdn_conv_w, gdn_a_log, gdn_dt_bias, gdn_norm_g,
           mlstm_i_bias, mlstm_f_bias, mlstm_norm_g, w_branch_f, w_branch_g, w_branch_m, w_out,
           router_w, router_b, exp_w_gate, exp_b_gate, exp_w_up, exp_b_up, exp_w_down, exp_b_down,
           final_norm_g):
    cond_tbl = jnp.asarray(np.concatenate([np.zeros(CTX_TILES, np.int32),
                                           1 + np.arange(N_TILES - CTX_TILES, dtype=np.int32) // LAT_TILES_PER_SEQ]))
    tables = tuple(jnp.asarray(t) for t in _scan_tables())
    cond8 = jnp.concatenate([c_ctx[None, :], c, jnp.zeros((8 - 1 - B_LAT, D), F32)], axis=0)
    mod_all = _ada(cond8, w_ada, b_ada).reshape(DEPTH, 8, N_MOD, D)

    pos = jnp.asarray(_pos_table())
    dft_c = jnp.asarray(_dft_consts(T_CTX)).astype(BF16)
    dft_l = jnp.asarray(_dft_consts(T_LAT)).astype(BF16)
    ang = 2.0 * np.pi * ((np.arange(DH)[:, None] * np.arange(DH)[None, :]) % DH) / DH
    c128 = jnp.asarray(np.cos(ang).astype(np.float32)).astype(BF16)
    s128 = jnp.asarray(np.sin(ang).astype(np.float32)).astype(BF16)

    zeros_s = jnp.zeros((B_CTX, 2, H, DH, DH), F32)
    zeros_v = jnp.zeros((B_CTX, 2, H, LANES), F32)
    x = None
    gdn_states, c_states, n_states, m_states = [], [], [], []
    y_prompt = y_sample = None
    wm, wsh, wsl = _repack_w_in(w_in)
    for l in range(DEPTH):
        mod = mod_all[l]
        if l == 0:
            outs = _pre(l, (x_prompt.reshape(N_CTX, D), x_sample.reshape(N_LAT, D), pos), mod,
                        norm1_g[l][None, :], wm, wsh, wsl, cond_tbl)
            x, outs = outs[0], outs[1:]
        else:
            outs = _pre(l, x, mod, norm1_g[l][None, :], wm, wsh, wsl, cond_tbl)
        f_all, gqkv, gz, mqkv, mo, gates, sm = outs

        yfc = _fourier(f_all, dft_c, c128, s128, B_CTX, T_CTX, 0)
        yfl = _fourier(f_all, dft_l, c128, s128, B_LAT, T_LAT, N_CTX)

        prm = jnp.zeros((8, LANES), F32)
        prm = prm.at[0, 0:8].set(gdn_a_log[l].reshape(-1))
        prm = prm.at[1, 0:8].set(gdn_dt_bias[l].reshape(-1))
        prm = prm.at[1, 16:24].set(mlstm_i_bias[l].reshape(-1))
        prm = prm.at[1, 24:32].set(mlstm_f_bias[l].reshape(-1))
        qkv_c, gts = _prep2(gqkv, sm, gdn_conv_w[l], prm)

        s0 = jnp.concatenate([zeros_s, state_gdn[:, l]], axis=0)
        c0 = jnp.concatenate([zeros_s, state_mlstm_c[:, l]], axis=0)
        n0 = jnp.concatenate([zeros_v, state_mlstm_n[:, l]], axis=0)
        m0 = jnp.concatenate([zeros_v, jnp.broadcast_to(state_mlstm_m[:, l][..., None],
                                                         (B_LAT, 2, H, LANES))], axis=0)
        o_f, o_b, h_f, h_b, s_fin, c_fin, n_fin, m_fin = _rec_scan(
            tables, _gdn_wy(qkv_c, gts), mqkv, gts, s0, c0, n0, m0)
        gdn_states.append(s_fin[:B_CTX])
        c_states.append(c_fin[:B_CTX])
        n_states.append(n_fin[:B_CTX])
        m_states.append(m_fin[:B_CTX, :, :, 0])

        rw = jnp.concatenate([router_w[l], jnp.zeros((D, LANES - N_EXP), F32)], axis=1)
        rwh = rw.astype(BF16)
        rwl = (rw - rwh.astype(F32)).astype(BF16)
        rb = jnp.concatenate([router_b[l], jnp.zeros((LANES - N_EXP,), F32)])[None, :]
        x, h2, info, info_t, stats = _post(
            cond_tbl, x, mod, yfc, yfl, o_f, o_b, h_f, h_b, gz, mo, gates,
            gdn_norm_g[l][None, :], mlstm_norm_g[l][None, :],
            w_branch_f[l].astype(BF16), w_branch_g[l].astype(BF16), w_branch_m[l].astype(BF16),
            w_out[l].astype(BF16), norm2_g[l][None, :], rwh, rwl, rb)

        (tile_exp, n_used, valid, nxt_exp, slot_start, seg_cnt, seg_off, pad_start, pad_cnt,
         tail) = _route_glue(stats)
        xs_sorted = _dispatch(slot_start, seg_cnt, seg_off, pad_start, pad_cnt, tail, h2, info_t)
        y_sorted = _gmm(l, tile_exp, n_used, valid, nxt_exp, xs_sorted, exp_w_gate, exp_b_gate, exp_w_up,
                        exp_b_up, exp_w_down, exp_b_down)
        res = _combine(l + 1 == DEPTH, cond_tbl, slot_start, seg_cnt, seg_off, y_sorted, info, x, mod,
                       final_norm_g[None, :])
        if l + 1 < DEPTH:
            x = res[0]
        else:
            y_prompt, y_sample = res

    return (y_prompt.reshape(B_CTX, T_CTX, D), y_sample.reshape(B_LAT, T_LAT, D),
            jnp.stack(gdn_states, axis=1), jnp.stack(c_states, axis=1),
            jnp.stack(n_states, axis=1), jnp.stack(m_states, axis=1))
```
